```python
import jax, jax.numpy as jnp
from jax import lax
import numpy as np

D_MODEL = 1024
BATCH = 32
SEQ = 256
DEPTH = 2
DEC_BATCH = 8
DEC_SEQ = 2048
PAST_LEN = 256

GRID_W = 64
N_MOD = 9
FFN_DIM = 2816
EPS = 1e-6
GLA_HEADS = 4
GLA_DK = 64
GLA_DV = 128
GLA_GATE_RANK = 16
GLA_TAU = 16.0
GLA_CHUNK = 64
FN_GROUPS = 4
FN_CH = 128
SWA_HEADS = 16
SWA_KV_HEADS = 4
SWA_GROUP = SWA_HEADS // SWA_KV_HEADS
SWA_HD = 64
WINDOW = 128
ATTN_BLOCK = 128
ROPE_BASE = 10000.0

GLA_QK = GLA_HEADS * GLA_DK
GLA_V = GLA_HEADS * GLA_DV
FN_W = FN_GROUPS * FN_CH
L0_SPLITS = (GLA_QK, 2 * GLA_QK, 2 * GLA_QK + GLA_V, 2 * GLA_QK + 2 * GLA_V,
             2 * GLA_QK + 2 * GLA_V + GLA_GATE_RANK, 2 * GLA_QK + 2 * GLA_V + 2 * GLA_GATE_RANK)
L0_IN_DIM = L0_SPLITS[-1] + FN_W
L0_MIX = GLA_V + FN_W
SWA_Q = SWA_HEADS * SWA_HD
SWA_KV = SWA_KV_HEADS * SWA_HD
L1_IN_DIM = SWA_Q + 2 * SWA_KV

kernel_name = "hybrid_diffusion_gla_fnet_swa_step"


def _rms(x, g):
    xf = x.astype(jnp.float32)
    y = xf * lax.rsqrt(jnp.mean(xf * xf, axis=-1, keepdims=True) + EPS)
    return (y * g.astype(jnp.float32)).astype(x.dtype)


def _modulation(cond, w, b):
    m = jax.nn.silu(cond) @ w + b
    return jnp.split(m[:, None, :], N_MOD, axis=-1)


def _ada_norm(x, g, shift, scale):
    return _rms(x, g) * (1 + scale) + shift


def _half_ffn(x, shift, scale, gate, g, w1, w3, w2):
    h = _ada_norm(x, g, shift, scale)
    f = (jax.nn.silu(h @ w1) * (h @ w3)) @ w2
    return x + 0.5 * gate * f


def _gla_chunked(q, k, v, log_a, s0):
    B, T, H, DK = q.shape
    DV = v.shape[-1]
    N = T // GLA_CHUNK
    f32 = jnp.float32
    q = q.astype(f32).reshape(B, N, GLA_CHUNK, H, DK)
    k = k.astype(f32).reshape(B, N, GLA_CHUNK, H, DK)
    v = v.astype(f32).reshape(B, N, GLA_CHUNK, H, DV)
    g = log_a.astype(f32).reshape(B, N, GLA_CHUNK, H, DK)
    b = jnp.cumsum(g, axis=2)
    b_last = b[:, :, -1:]
    q_dec = q * jnp.exp(b)
    k_inv = k * jnp.exp(-b)
    k_end = k * jnp.exp(b_last - b)
    mask = jnp.tril(jnp.ones((GLA_CHUNK, GLA_CHUNK), f32))
    attn = jnp.einsum('bnthk,bnshk->bnhts', q_dec, k_inv) * mask
    o_intra = jnp.einsum('bnhts,bnshv->bnthv', attn, v)
    delta = jnp.einsum('bnshk,bnshv->bnhkv', k_end, v)
    decay = jnp.exp(b_last[:, :, 0])

    def step(S, inp):
        d, dl = inp
        return d[..., None] * S + dl, S

    s_fin, s_in = lax.scan(step, s0.astype(f32), (jnp.moveaxis(decay, 1, 0), jnp.moveaxis(delta, 1, 0)))
    s_in = jnp.moveaxis(s_in, 0, 1)
    o_inter = jnp.einsum('bnthk,bnhkv->bnthv', q_dec, s_in)
    return (o_intra + o_inter).reshape(B, T, H, DV), s_fin


def _gla_fnet_mixer(h, s0_f, s0_b, w_in, w_gf, b_gf, w_gb, b_gb, g_head, w_out):
    B, T, _ = h.shape
    p = h @ w_in
    q, k, v, og, lf, lb, u = jnp.split(p, L0_SPLITS, axis=-1)
    q = q.reshape(B, T, GLA_HEADS, GLA_DK) * (GLA_DK ** -0.5)
    k = k.reshape(B, T, GLA_HEADS, GLA_DK)
    v = v.reshape(B, T, GLA_HEADS, GLA_DV)
    la_f = (jax.nn.log_sigmoid((lf @ w_gf + b_gf).astype(jnp.float32)) / GLA_TAU).reshape(B, T, GLA_HEADS, GLA_DK)
    la_b = (jax.nn.log_sigmoid((lb @ w_gb + b_gb).astype(jnp.float32)) / GLA_TAU).reshape(B, T, GLA_HEADS, GLA_DK)
    rev = lambda z: z[:, ::-1]
    o_f, s_f = _gla_chunked(q, k, v, la_f, s0_f)
    o_b, s_b = _gla_chunked(rev(q), rev(k), rev(v), rev(la_b), s0_b)
    o = _rms(o_f + rev(o_b), g_head)
    gla_out = (o * jax.nn.silu(og.reshape(B, T, GLA_HEADS, GLA_DV).astype(jnp.float32)))
    gla_out = gla_out.reshape(B, T, GLA_V).astype(h.dtype)
    uf = u.reshape(B, T, FN_GROUPS, FN_CH).astype(jnp.float32)
    fn_out = jnp.fft.fft2(uf, axes=(1, 3), norm='ortho').real.reshape(B, T, FN_W).astype(h.dtype)
    return jnp.concatenate([gla_out, fn_out], axis=-1) @ w_out, s_f, s_b


def _sink_attend(q, k, v, valid, sink):
    s = jnp.einsum('bqhgd,bkhd->bhgqk', q, k).astype(jnp.float32)
    if valid is not None:
        s = jnp.where(valid, s, -jnp.inf)
    sk = jnp.broadcast_to(sink.astype(jnp.float32)[None, :, :, None, None], s.shape[:-1] + (1,))
    p = jax.nn.softmax(jnp.concatenate([s, sk], axis=-1), axis=-1)[..., :-1]
    return jnp.einsum('bhgqk,bkhd->bqhgd', p.astype(v.dtype), v)


def _swa_project(h, w_in):
    B, T, _ = h.shape
    q, k, v = jnp.split(h @ w_in, (SWA_Q, SWA_Q + SWA_KV), axis=-1)
    q = q.reshape(B, T, SWA_KV_HEADS, SWA_GROUP, SWA_HD) * (SWA_HD ** -0.5)
    k = k.reshape(B, T, SWA_KV_HEADS, SWA_HD)
    v = v.reshape(B, T, SWA_KV_HEADS, SWA_HD)
    return q, k, v


def _axial_rope_tables(T):
    rows = T // GRID_W
    row = jnp.repeat(jnp.arange(rows), GRID_W).astype(jnp.float32)
    col = (jnp.arange(rows * GRID_W) % GRID_W).astype(jnp.float32)
    n_freq = SWA_HD // 4
    inv = ROPE_BASE ** (-jnp.arange(n_freq, dtype=jnp.float32) / n_freq)
    ar = row[:, None] * inv
    ac = col[:, None] * inv
    ang = jnp.concatenate([ar, ar, ac, ac], axis=-1)
    return jnp.cos(ang), jnp.sin(ang)


def _rope(x, cos, sin):
    T = x.shape[1]
    shp = (1, T) + (1,) * (x.ndim - 3) + (SWA_HD,)
    a, b, c2, d = jnp.split(x, 4, axis=-1)
    rot = jnp.concatenate([-b, a, -d, c2], axis=-1)
    return (x * cos.reshape(shp) + rot * sin.reshape(shp)).astype(x.dtype)


def _swa_context(h, w_in, sink, w_out):
    B, T, _ = h.shape
    q, k, v = _swa_project(h, w_in)
    nb = T // ATTN_BLOCK
    qb = jnp.moveaxis(q.reshape(B, nb, ATTN_BLOCK, SWA_KV_HEADS, SWA_GROUP, SWA_HD), 1, 0)
    o = lax.map(lambda qi: _sink_attend(qi, k, v, None, sink), qb)
    o = jnp.moveaxis(o, 0, 1).reshape(B, T, SWA_Q)
    return o @ w_out, k, v


def _swa_latent(h, ctx_k, ctx_v, w_in, sink, w_out):
    B, T, _ = h.shape
    q, k, v = _swa_project(h, w_in)
    cos, sin = _axial_rope_tables(T)
    q = _rope(q, cos, sin)
    k = _rope(k, cos, sin)
    Lc = ctx_k.shape[1]
    nb = T // ATTN_BLOCK
    pad = ((0, 0), (ATTN_BLOCK, ATTN_BLOCK), (0, 0), (0, 0))
    kp = jnp.pad(k, pad)
    vp = jnp.pad(v, pad)
    ctx_valid = jnp.ones((ATTN_BLOCK, Lc), bool)

    def block(i):
        qi = lax.dynamic_slice_in_dim(q, i * ATTN_BLOCK, ATTN_BLOCK, axis=1)
        ki = lax.dynamic_slice_in_dim(kp, i * ATTN_BLOCK, 3 * ATTN_BLOCK, axis=1)
        vi = lax.dynamic_slice_in_dim(vp, i * ATTN_BLOCK, 3 * ATTN_BLOCK, axis=1)
        qpos = i * ATTN_BLOCK + jnp.arange(ATTN_BLOCK)
        kpos = (i - 1) * ATTN_BLOCK + jnp.arange(3 * ATTN_BLOCK)
        band = (jnp.abs(qpos[:, None] - kpos[None, :]) <= WINDOW) & (kpos >= 0)[None, :] & (kpos < T)[None, :]
        valid = jnp.concatenate([ctx_valid, band], axis=1)
        return _sink_attend(qi, jnp.concatenate([ctx_k.astype(ki.dtype), ki], axis=1),
                            jnp.concatenate([ctx_v.astype(vi.dtype), vi], axis=1), valid, sink)

    o = lax.map(block, jnp.arange(nb))
    o = jnp.moveaxis(o, 0, 1).reshape(B, T, SWA_Q)
    return o @ w_out


def setup_inputs(seed: int = 0) -> dict:
    key = jax.random.key(seed)
    ks = jax.random.split(key, 32)
    D = D_MODEL
    n = lambda k, shape, s: jax.random.normal(k, shape, jnp.float32) * s
    return {
        "x_prompt": n(ks[0], (BATCH, SEQ, D), 1.0),
        "x_sample": n(ks[1], (DEC_BATCH, DEC_SEQ, D), 1.0),
        "state_l0_gla_fwd": n(ks[2], (DEC_BATCH, GLA_HEADS, GLA_DK, GLA_DV), 0.3),
        "state_l0_gla_bwd": n(ks[3], (DEC_BATCH, GLA_HEADS, GLA_DK, GLA_DV), 0.3),
        "cache_l1_k": n(ks[4], (DEC_BATCH, PAST_LEN, SWA_KV_HEADS, SWA_HD), 1.0),
        "cache_l1_v": n(ks[5], (DEC_BATCH, PAST_LEN, SWA_KV_HEADS, SWA_HD), 1.0),
        "c": n(ks[6], (DEC_BATCH, D), 1.0),
        "c_ctx": n(ks[7], (D,), 1.0),
        "mod_w": n(ks[8], (DEPTH, D, N_MOD * D), 0.5 * D ** -0.5),
        "mod_b": n(ks[9], (DEPTH, N_MOD * D), 0.02),
        "norm_g": 1.0 + n(ks[10], (DEPTH, 3, D), 0.05),
        "ffn_w1": n(ks[11], (DEPTH, 2, D, FFN_DIM), D ** -0.5),
        "ffn_w3": n(ks[12], (DEPTH, 2, D, FFN_DIM), D ** -0.5),
        "ffn_w2": n(ks[13], (DEPTH, 2, FFN_DIM, D), FFN_DIM ** -0.5),
        "l0_w_in": n(ks[14], (D, L0_IN_DIM), D ** -0.5),
        "l0_w_gf": n(ks[15], (GLA_GATE_RANK, GLA_QK), GLA_GATE_RANK ** -0.5),
        "l0_b_gf": n(ks[16], (GLA_QK,), 0.1),
        "l0_w_gb": n(ks[17], (GLA_GATE_RANK, GLA_QK), GLA_GATE_RANK ** -0.5),
        "l0_b_gb": n(ks[18], (GLA_QK,), 0.1),
        "l0_g_head": 1.0 + n(ks[19], (GLA_DV,), 0.05),
        "l0_w_out": n(ks[20], (L0_MIX, D), L0_MIX ** -0.5),
        "l1_w_in": n(ks[21], (D, L1_IN_DIM), D ** -0.5),
        "l1_sink": n(ks[22], (SWA_KV_HEADS, SWA_GROUP), 0.5),
        "l1_w_out": n(ks[23], (SWA_Q, D), SWA_Q ** -0.5),
        "final_g": 1.0 + n(ks[24], (D,), 0.05),
    }


def reference(x_prompt, x_sample, state_l0_gla_fwd, state_l0_gla_bwd, cache_l1_k, cache_l1_v, c, c_ctx,
              mod_w, mod_b, norm_g, ffn_w1, ffn_w3, ffn_w2,
              l0_w_in, l0_w_gf, l0_b_gf, l0_w_gb, l0_b_gb, l0_g_head, l0_w_out,
              l1_w_in, l1_sink, l1_w_out, final_g):
    xp, xs = x_prompt, x_sample
    new_state = []
    for layer in range(DEPTH):
        mp = _modulation(c_ctx[None, :], mod_w[layer], mod_b[layer])
        ms = _modulation(c, mod_w[layer], mod_b[layer])
        ffn_a = (norm_g[layer, 0], ffn_w1[layer, 0], ffn_w3[layer, 0], ffn_w2[layer, 0])
        ffn_b = (norm_g[layer, 2], ffn_w1[layer, 1], ffn_w3[layer, 1], ffn_w2[layer, 1])
        xp = _half_ffn(xp, mp[0], mp[1], mp[2], *ffn_a)
        xs = _half_ffn(xs, ms[0], ms[1], ms[2], *ffn_a)
        hp = _ada_norm(xp, norm_g[layer, 1], mp[3], mp[4])
        hs = _ada_norm(xs, norm_g[layer, 1], ms[3], ms[4])
        if layer % 2 == 0:
            w = (l0_w_in, l0_w_gf, l0_b_gf, l0_w_gb, l0_b_gb, l0_g_head, l0_w_out)
            zero = jnp.zeros((xp.shape[0], GLA_HEADS, GLA_DK, GLA_DV), jnp.float32)
            op, s_f, s_b = _gla_fnet_mixer(hp, zero, zero, *w)
            os_, _, _ = _gla_fnet_mixer(hs, state_l0_gla_fwd, state_l0_gla_bwd, *w)
            new_state += [s_f, s_b]
        else:
            op, k_c, v_c = _swa_context(hp, l1_w_in, l1_sink, l1_w_out)
            os_ = _swa_latent(hs, cache_l1_k, cache_l1_v, l1_w_in, l1_sink, l1_w_out)
            new_state += [k_c, v_c]
        xp = xp + mp[5] * op
        xs = xs + ms[5] * os_
        xp = _half_ffn(xp, mp[6], mp[7], mp[8], *ffn_b)
        xs = _half_ffn(xs, ms[6], ms[7], ms[8], *ffn_b)
    y_prompt = _rms(xp, final_g)
    y_sample = _rms(xs, final_g)
    return (y_prompt, y_sample, *new_state)
```

```python
import functools

import numpy as np
import jax
import jax.numpy as jnp
from jax import lax
from jax.experimental import pallas as pl
from jax.experimental.pallas import tpu as pltpu

F32 = jnp.float32
BF16 = jnp.bfloat16

D_MODEL = 1024
FFN_DIM = 2816
N_MOD = 9
EPS = 1e-6
GRID_W = 64
GLA_HEADS = 4
GLA_DK = 64
GLA_DV = 128
GLA_GATE_RANK = 16
GLA_TAU = 16.0
GLA_CHUNK = 64
GLA_QK = GLA_HEADS * GLA_DK
GLA_V = GLA_HEADS * GLA_DV
FN_GROUPS = 4
FN_CH = 128
FN_W = FN_GROUPS * FN_CH
SWA_HEADS = 16
SWA_KV_HEADS = 4
SWA_HD = 64
SWA_Q = SWA_HEADS * SWA_HD
SWA_KV = SWA_KV_HEADS * SWA_HD
WINDOW = 128
ATTN_BLOCK = 128
ROPE_BASE = 10000.0

LANES = 128
VMEM_LIMIT = 56 * 1024 * 1024

TM = 512
TF = 256
GLA_TILE = 256
FN_ROWS = 256
MOD_ROWS = 16
MOD_TN = 1024


def _params(sem):
    return pltpu.CompilerParams(dimension_semantics=sem, vmem_limit_bytes=VMEM_LIMIT)


def _resident(shape):
    nd = len(shape)
    return pl.BlockSpec(shape, lambda *_: (0,) * nd, pipeline_mode=pl.Buffered(1))


def _dot(a, b):
    return jnp.dot(a, b, preferred_element_type=F32)


def _silu(x):
    return x * jax.nn.sigmoid(x)


def _ada_norm(x, g, shift, scale):
    ms = jnp.mean(x * x, axis=-1, keepdims=True)
    return (x * lax.rsqrt(ms + EPS) * g) * (1.0 + scale) + shift


def _mod_kernel(c_ref, w_ref, b_ref, o_ref):
    c = c_ref[...]
    s = _silu(c).astype(BF16)
    o_ref[0] = _dot(s, w_ref[0].astype(BF16)) + b_ref[0]


def _modulation(cond, mod_w, mod_b):
    depth, d, n = mod_w.shape
    return pl.pallas_call(
        _mod_kernel,
        grid=(depth, n // MOD_TN),
        in_specs=[
            pl.BlockSpec((MOD_ROWS, d), lambda l, j: (0, 0)),
            pl.BlockSpec((1, d, MOD_TN), lambda l, j: (l, 0, j)),
            pl.BlockSpec((1, 1, MOD_TN), lambda l, j: (l, 0, j)),
        ],
        out_specs=pl.BlockSpec((1, MOD_ROWS, MOD_TN), lambda l, j: (l, 0, j)),
        out_shape=jax.ShapeDtypeStruct((depth, MOD_ROWS, n), F32),
        compiler_params=_params(("arbitrary", "arbitrary")),
        name="modulation",
    )(cond, mod_w, mod_b.reshape(depth, 1, n))


def _ffn_kernel(*refs, n_mix, mrow, final):
    x_ref = refs[0]
    mix_refs = refs[1:1 + n_mix]
    pos = 1 + n_mix
    if n_mix:
        wo_ref = refs[pos]
        pos += 1
    m_ref, g_ref, w1_ref, w3_ref, w2_ref = refs[pos:pos + 5]
    pos += 5
    if final:
        fg_ref = refs[pos]
        pos += 1
    o_ref, acc_ref = refs[pos], refs[pos + 1]

    x = x_ref[...]
    if n_mix:
        off = 0
        mix = None
        for r in mix_refs:
            w = r.shape[1]
            t = _dot(r[...], wo_ref[off:off + w, :])
            mix = t if mix is None else mix + t
            off += w
        x = x + m_ref[0, 5:6, :] * mix
    o_ref[...] = x
    h = _ada_norm(x, g_ref[...], m_ref[0, mrow:mrow + 1, :], m_ref[0, mrow + 1:mrow + 2, :]).astype(BF16)
    for j in range(FFN_DIM // TF):
        a = _dot(h, w1_ref[:, j * TF:(j + 1) * TF])
        b = _dot(h, w3_ref[:, j * TF:(j + 1) * TF])
        t = _dot((_silu(a) * b).astype(BF16), w2_ref[j * TF:(j + 1) * TF, :])
        if j == 0:
            acc_ref[...] = t
        else:
            acc_ref[...] += t
    y = o_ref[...] + (0.5 * m_ref[0, mrow + 2:mrow + 3, :]) * acc_ref[...]
    if final:
        ms = jnp.mean(y * y, axis=-1, keepdims=True)
        y = y * lax.rsqrt(ms + EPS) * fg_ref[...]
    o_ref[...] = y


def _half_ffn(x, mods, bfn, g, w1, w3, w2, *, mrow, mixes=(), w_out=None, final_g=None):
    m, d = x.shape
    row = lambda i: (i, 0)
    ins = [x]
    specs = [pl.BlockSpec((TM, d), row)]
    for a in mixes:
        ins.append(a)
        specs.append(pl.BlockSpec((TM, a.shape[1]), row))
    if mixes:
        ins.append(w_out)
        specs.append(_resident(w_out.shape))
    ins += [mods, g, w1, w3, w2]
    specs += [pl.BlockSpec((1, MOD_ROWS, d), lambda i: (bfn(i), 0, 0)),
              _resident(g.shape), _resident(w1.shape), _resident(w3.shape), _resident(w2.shape)]
    if final_g is not None:
        ins.append(final_g)
        specs.append(_resident(final_g.shape))
    return pl.pallas_call(
        functools.partial(_ffn_kernel, n_mix=len(mixes), mrow=mrow, final=final_g is not None),
        grid=(m // TM,),
        in_specs=specs,
        out_specs=pl.BlockSpec((TM, d), row),
        out_shape=jax.ShapeDtypeStruct((m, d), F32),
        scratch_shapes=[pltpu.VMEM((TM, d), F32)],
        compiler_params=_params(("arbitrary",)),
        name="half_ffn",
    )(*ins)


def _l0_in_kernel(x_ref, m_ref, g_ref, wm_ref, wg_ref, wg2_ref, bg_ref,
                  q_ref, k_ref, v_ref, og_ref, la_ref, u_ref):
    h = _ada_norm(x_ref[...], g_ref[...], m_ref[0, 3:4, :], m_ref[0, 4:5, :]).astype(BF16)
    q_ref[...] = _dot(h, wm_ref[:, 0:256]) * (GLA_DK ** -0.5)
    k_ref[...] = _dot(h, wm_ref[:, 256:512])
    v_ref[...] = _dot(h, wm_ref[:, 512:1024]).astype(BF16)
    og_ref[...] = _dot(h, wm_ref[:, 1024:1536])
    u_ref[...] = _dot(h, wm_ref[:, 1536:2048]).astype(BF16)
    lr = _dot(h, wg_ref[...]).astype(BF16)
    z = _dot(lr, wg2_ref[...]) + bg_ref[...]
    log_sig = jnp.minimum(z, 0.0) - jnp.log1p(jnp.exp(-jnp.abs(z)))
    la_ref[...] = log_sig * (1.0 / GLA_TAU)


def _l0_in_proj(x, mods, bfn, g, wm, wg, wg2, bg):
    m, d = x.shape
    row = lambda i: (i, 0)
    outs = [(GLA_QK, F32), (GLA_QK, F32), (GLA_V, BF16), (GLA_V, F32), (2 * GLA_QK, F32), (FN_W, BF16)]
    return pl.pallas_call(
        _l0_in_kernel,
        grid=(m // TM,),
        in_specs=[pl.BlockSpec((TM, d), row),
                  pl.BlockSpec((1, MOD_ROWS, d), lambda i: (bfn(i), 0, 0)),
                  _resident(g.shape), _resident(wm.shape), _resident(wg.shape),
                  _resident(wg2.shape), _resident(bg.shape)],
        out_specs=[pl.BlockSpec((TM, w), row) for w, _ in outs],
        out_shape=[jax.ShapeDtypeStruct((m, w), dt) for w, dt in outs],
        compiler_params=_params(("arbitrary",)),
        name="l0_in_proj",
    )(x, mods, g, wm, wg, wg2, bg)


def _gla_tile(q, k, v, g, s_ref, reverse):
    tt = GLA_TILE
    nch = tt // GLA_CHUNK
    r = lax.broadcasted_iota(jnp.int32, (tt, tt), 0)
    c = lax.broadcasted_iota(jnp.int32, (tt, tt), 1)
    same = (r // GLA_CHUNK) == (c // GLA_CHUNK)
    m_intra = jnp.logical_and(same, (c >= r) if reverse else (c <= r))
    tri = jnp.where(m_intra, 1.0, 0.0).astype(BF16)
    g_hi = g.astype(BF16)
    r1 = g - g_hi.astype(F32)
    g_mid = r1.astype(BF16)
    g_lo = (r1 - g_mid.astype(F32)).astype(BF16)
    b = _dot(tri, g_hi) + _dot(tri, g_mid) + _dot(tri, g_lo)
    qd = q * jnp.exp(b)
    b_t = b.T
    k_t = k.T
    cc = lax.broadcasted_iota(jnp.int32, (GLA_QK, tt), 1) // GLA_CHUNK
    end_col = [ci * GLA_CHUNK + (0 if reverse else GLA_CHUNK - 1) for ci in range(nch)]
    ends = [b_t[:, e:e + 1] for e in end_col]
    bl_t = ends[nch - 1]
    for ci in range(nch - 2, -1, -1):
        bl_t = jnp.where(cc == ci, ends[ci], bl_t)
    ki_t = (k_t * jnp.exp(-b_t)).astype(BF16)
    ke_t = k_t * jnp.exp(bl_t - b_t)

    lane = lax.broadcasted_iota(jnp.int32, (tt, GLA_QK), 1) // GLA_DK
    o_intra = []
    for h in range(GLA_HEADS):
        qh = jnp.where(lane == h, qd, 0.0).astype(BF16)
        a = _dot(qh, ki_t)
        a = jnp.where(m_intra, a, 0.0).astype(BF16)
        o_intra.append(_dot(a, v[:, h * GLA_DV:(h + 1) * GLA_DV]))

    lane_c = lax.broadcasted_iota(jnp.int32, (GLA_CHUNK, GLA_QK), 1) // GLA_DK
    s = s_ref[...]
    o_inter = [None] * nch
    for ci in (range(nch - 1, -1, -1) if reverse else range(nch)):
        qd_c = qd[ci * GLA_CHUNK:(ci + 1) * GLA_CHUNK, :]
        q_bd = jnp.concatenate([jnp.where(lane_c == h, qd_c, 0.0).astype(BF16) for h in range(GLA_HEADS)], axis=0)
        o_inter[ci] = _dot(q_bd, s.astype(BF16))
        ke_c = jnp.where(cc == ci, ke_t, 0.0).astype(BF16)
        d_full = _dot(ke_c, v)
        delta = jnp.concatenate(
            [d_full[h * GLA_DK:(h + 1) * GLA_DK, h * GLA_DV:(h + 1) * GLA_DV] for h in range(GLA_HEADS)], axis=0)
        s = jnp.exp(ends[ci]) * s + delta
    s_ref[...] = s

    cols = []
    for h in range(GLA_HEADS):
        inter_h = jnp.concatenate(
            [o_inter[ci][h * GLA_CHUNK:(h + 1) * GLA_CHUNK, :] for ci in range(nch)], axis=0)
        cols.append(o_intra[h] + inter_h)
    return jnp.concatenate(cols, axis=1)


def _gla_kernel(q_ref, k_ref, v_ref, la_ref, og_ref, s0f_ref, s0b_ref, gh_ref,
                o_ref, sf_ref, sb_ref, s_ref, ob_ref, *, nt):
    p = pl.program_id(1)
    i = pl.program_id(2)

    @pl.when(p == 0)
    def _backward():
        @pl.when(i == 0)
        def _():
            s_ref[...] = s0b_ref[0]

        o = _gla_tile(q_ref[...], k_ref[...], v_ref[...], la_ref[...], s_ref, True)
        start = pl.multiple_of((nt - 1 - i) * GLA_TILE, GLA_TILE)
        ob_ref[pl.ds(start, GLA_TILE), :] = o

        @pl.when(i == nt - 1)
        def _():
            sb_ref[0] = s_ref[...]

    @pl.when(p == 1)
    def _forward():
        @pl.when(i == 0)
        def _():
            s_ref[...] = s0f_ref[0]

        o = _gla_tile(q_ref[...], k_ref[...], v_ref[...], la_ref[...], s_ref, False)
        start = pl.multiple_of(i * GLA_TILE, GLA_TILE)
        o = o + ob_ref[pl.ds(start, GLA_TILE), :]
        for h in range(GLA_HEADS):
            sl = slice(h * GLA_DV, (h + 1) * GLA_DV)
            oh = o[:, sl]
            ms = jnp.mean(oh * oh, axis=-1, keepdims=True)
            y = oh * lax.rsqrt(ms + EPS) * gh_ref[...]
            o_ref[:, sl] = (y * _silu(og_ref[:, sl])).astype(BF16)

        @pl.when(i == nt - 1)
        def _():
            sf_ref[0] = s_ref[...]


def _gla(q, k, v, la, og, s0f, s0b, g_head, batch, seq):
    nt = seq // GLA_TILE
    tile = lambda b, p, i: (b * nt + p * i + (1 - p) * (nt - 1 - i), 0)
    st = pl.BlockSpec((1, GLA_QK, GLA_DV), lambda b, p, i: (b, 0, 0))
    return pl.pallas_call(
        functools.partial(_gla_kernel, nt=nt),
        grid=(batch, 2, nt),
        in_specs=[pl.BlockSpec((GLA_TILE, GLA_QK), tile),
                  pl.BlockSpec((GLA_TILE, GLA_QK), tile),
                  pl.BlockSpec((GLA_TILE, GLA_V), tile),
                  pl.BlockSpec((GLA_TILE, GLA_QK), lambda b, p, i: (tile(b, p, i)[0], 1 - p)),
                  pl.BlockSpec((GLA_TILE, GLA_V), lambda b, p, i: (b * nt + p * i, 0)),
                  st, st, _resident(g_head.shape)],
        out_specs=[pl.BlockSpec((GLA_TILE, GLA_V), lambda b, p, i: (b * nt + p * i, 0)), st, st],
        out_shape=[jax.ShapeDtypeStruct((batch * seq, GLA_V), BF16),
                   jax.ShapeDtypeStruct((batch, GLA_QK, GLA_DV), F32),
                   jax.ShapeDtypeStruct((batch, GLA_QK, GLA_DV), F32)],
        scratch_shapes=[pltpu.VMEM((GLA_QK, GLA_DV), F32), pltpu.VMEM((seq, GLA_V), F32)],
        compiler_params=_params(("arbitrary", "arbitrary", "arbitrary")),
        name="gla",
    )(q, k, v, la, og, s0f, s0b, g_head)


def _fnet_kernel(u_ref, cc_ref, sc_ref, cs_ref, o_ref, ucs_ref, *, seq, scale):
    @pl.when(pl.program_id(1) == 0)
    def _():
        for gi in range(FN_GROUPS):
            sl = slice(gi * FN_CH, (gi + 1) * FN_CH)
            ug = u_ref[:, sl]
            ucs_ref[0:seq, sl] = _dot(ug, cc_ref[...]).astype(BF16)
            ucs_ref[seq:2 * seq, sl] = _dot(ug, sc_ref[...]).astype(BF16)

    o_ref[...] = (_dot(cs_ref[...], ucs_ref[...]) * scale).astype(BF16)


def _dft_tables(seq):
    def cs(n):
        idx = np.arange(n, dtype=np.int64)
        ang = 2.0 * np.pi * ((idx[:, None] * idx[None, :]) % n).astype(np.float64) / n
        return np.cos(ang), np.sin(ang)
    ct, st = cs(seq)
    cc, sc = cs(FN_CH)
    to = lambda a: jnp.asarray(a.astype(np.float32)).astype(BF16)
    return to(np.concatenate([ct, -st], axis=1)), to(cc), to(sc)


def _fnet(u, batch, seq):
    cs, cc, sc = _dft_tables(seq)
    rows = min(FN_ROWS, seq)
    nj = seq // rows
    return pl.pallas_call(
        functools.partial(_fnet_kernel, seq=seq, scale=float((seq * FN_CH) ** -0.5)),
        grid=(batch, nj),
        in_specs=[pl.BlockSpec((seq, FN_W), lambda b, j: (b, 0)),
                  _resident(cc.shape), _resident(sc.shape),
                  pl.BlockSpec((rows, 2 * seq), lambda b, j: (j, 0))],
        out_specs=pl.BlockSpec((rows, FN_W), lambda b, j: (b * nj + j, 0)),
        out_shape=jax.ShapeDtypeStruct((batch * seq, FN_W), BF16),
        scratch_shapes=[pltpu.VMEM((2 * seq, FN_W), BF16)],
        compiler_params=_params(("arbitrary", "arbitrary")),
        name="fnet",
    )(u, cc, sc, cs)


def _rope128(x, cos, sin_signed, odd):
    back = pltpu.roll(x, 16, 1)
    fwd = pltpu.roll(x, 112, 1)
    return x * cos + jnp.where(odd, back, fwd) * sin_signed


def _l1_in_kernel(*refs, rope):
    if rope:
        x_ref, m_ref, g_ref, w_ref, cos_ref, sin_ref, q_ref, k_ref, v_ref = refs
        cos, sin = cos_ref[...], sin_ref[...]
        odd = (lax.broadcasted_iota(jnp.int32, cos.shape, 1) // 16) % 2 == 1
    else:
        x_ref, m_ref, g_ref, w_ref, q_ref, k_ref, v_ref = refs
    h = _ada_norm(x_ref[...], g_ref[...], m_ref[0, 3:4, :], m_ref[0, 4:5, :]).astype(BF16)

    def project(dst, col0, width, scale):
        for j in range(width // 256):
            y = _dot(h, w_ref[:, col0 + 256 * j:col0 + 256 * (j + 1)])
            if scale != 1.0:
                y = y * scale
            for half in range(2):
                yh = y[:, LANES * half:LANES * (half + 1)]
                if rope:
                    yh = _rope128(yh, cos, sin, odd)
                lo = 256 * j + LANES * half
                dst[:, lo:lo + LANES] = yh.astype(dst.dtype)

    project(q_ref, 0, SWA_Q, SWA_HD ** -0.5)
    project(k_ref, SWA_Q, SWA_KV, 1.0)
    v_ref[...] = _dot(h, w_ref[:, SWA_Q + SWA_KV:SWA_Q + 2 * SWA_KV])


def _l1_in_proj(x, mods, bfn, g, w, rope_tabs=None, tiles_per_seq=None):
    m, d = x.shape
    row = lambda i: (i, 0)
    ins = [x, mods, g, w]
    specs = [pl.BlockSpec((TM, d), row),
             pl.BlockSpec((1, MOD_ROWS, d), lambda i: (bfn(i), 0, 0)),
             _resident(g.shape), _resident(w.shape)]
    if rope_tabs is not None:
        ins += list(rope_tabs)
        specs += [pl.BlockSpec((TM, LANES), lambda i: (i % tiles_per_seq, 0))] * 2
    return pl.pallas_call(
        functools.partial(_l1_in_kernel, rope=rope_tabs is not None),
        grid=(m // TM,),
        in_specs=specs,
        out_specs=[pl.BlockSpec((TM, SWA_Q), row), pl.BlockSpec((TM, SWA_KV), row), pl.BlockSpec((TM, SWA_KV), row)],
        out_shape=[jax.ShapeDtypeStruct((m, SWA_Q), BF16),
                   jax.ShapeDtypeStruct((m, SWA_KV), F32),
                   jax.ShapeDtypeStruct((m, SWA_KV), F32)],
        compiler_params=_params(("arbitrary",)),
        name="l1_in_proj",
    )(*ins)


def _rope_tables(seq):
    rows = seq // GRID_W
    row = jnp.repeat(jnp.arange(rows), GRID_W).astype(F32)
    col = (jnp.arange(rows * GRID_W) % GRID_W).astype(F32)
    n_freq = SWA_HD // 4
    inv = ROPE_BASE ** (-jnp.arange(n_freq, dtype=F32) / n_freq)
    ar = row[:, None] * inv
    ac = col[:, None] * inv
    ang = jnp.concatenate([ar, ar, ac, ac], axis=-1)
    cos, sin = jnp.cos(ang), jnp.sin(ang)
    sign = jnp.where((jnp.arange(SWA_HD) // n_freq) % 2 == 1, 1.0, -1.0).astype(F32)
    two = lambda t: jnp.concatenate([t, t], axis=-1)
    return two(cos), two(sin * sign)


def _attend(q_ref, ks, vs, valid, sink_ref, o_ref):
    kcat = ks[0] if len(ks) == 1 else jnp.concatenate(ks, axis=0)
    vcat = vs[0] if len(vs) == 1 else jnp.concatenate(vs, axis=0)
    n = kcat.shape[0]
    low = lax.broadcasted_iota(jnp.int32, (n, LANES), 1) < SWA_HD
    nt_dims = (((1,), (1,)), ((), ()))
    group = SWA_HEADS // SWA_KV_HEADS
    for g in range(SWA_KV_HEADS):
        sl = slice(LANES * (g // 2), LANES * (g // 2 + 1))
        kcol, vcol = kcat[:, sl], vcat[:, sl]
        ksw, vsw = pltpu.roll(kcol, SWA_HD, 1), pltpu.roll(vcol, SWA_HD, 1)
        if g % 2 == 0:
            k_lo, k_hi = jnp.where(low, kcol, 0.0), jnp.where(low, 0.0, ksw)
            v_lo, v_hi = jnp.where(low, vcol, 0.0), jnp.where(low, 0.0, vsw)
        else:
            k_lo, k_hi = jnp.where(low, ksw, 0.0), jnp.where(low, 0.0, kcol)
            v_lo, v_hi = jnp.where(low, vsw, 0.0), jnp.where(low, 0.0, vcol)
        kx = (k_lo.astype(BF16), k_hi.astype(BF16))
        vx = (v_lo.astype(BF16), v_hi.astype(BF16))
        for jj in range(group // 2):
            j = (group // 2) * g + jj
            qcol = q_ref[:, LANES * j:LANES * (j + 1)]
            acc = None
            for e in range(2):
                s = lax.dot_general(qcol, kx[e], nt_dims, preferred_element_type=F32)
                if valid is not None:
                    s = jnp.where(valid, s, -jnp.inf)
                sk = sink_ref[2 * j + e]
                mx = jnp.maximum(jnp.max(s, axis=-1, keepdims=True), sk)
                p = jnp.exp(s - mx)
                denom = jnp.sum(p, axis=-1, keepdims=True) + jnp.exp(sk - mx)
                o = _dot(p.astype(BF16), vx[e]) * (1.0 / denom)
                acc = o if acc is None else acc + o
            o_ref[:, LANES * j:LANES * (j + 1)] = acc.astype(o_ref.dtype)


def _ctx_attn_kernel(sink_ref, q_ref, k_ref, v_ref, o_ref):
    _attend(q_ref, [k_ref[...]], [v_ref[...]], None, sink_ref, o_ref)


def _ctx_attention(q, k, v, sink, batch, seq):
    blk = lambda w: pl.BlockSpec((seq, w), lambda b: (b, 0))
    return pl.pallas_call(
        _ctx_attn_kernel,
        grid=(batch,),
        in_specs=[pl.BlockSpec(memory_space=pltpu.SMEM), blk(SWA_Q), blk(SWA_KV), blk(SWA_KV)],
        out_specs=blk(SWA_Q),
        out_shape=jax.ShapeDtypeStruct((batch * seq, SWA_Q), BF16),
        compiler_params=_params(("arbitrary",)),
        name="ctx_attention",
    )(sink, q, k, v)


def _lat_attn_kernel(sink_ref, q_ref, ck_ref, cv_ref, kp_ref, kc_ref, kn_ref, vp_ref, vc_ref, vn_ref, o_ref,
                     *, seq, n_ctx):
    i = pl.program_id(1)
    blk = ATTN_BLOCK
    n = n_ctx + 3 * blk
    col = lax.broadcasted_iota(jnp.int32, (blk, n), 1)
    qpos = i * blk + lax.broadcasted_iota(jnp.int32, (blk, n), 0)
    kpos = (i - 1) * blk + (col - n_ctx)
    dist = jnp.abs(qpos - kpos)
    dist = jnp.where(kpos < 0, WINDOW + 1, dist)
    dist = jnp.where(kpos >= seq, WINDOW + 1, dist)
    dist = jnp.where(col < n_ctx, 0, dist)
    valid = dist <= WINDOW
    _attend(q_ref, [ck_ref[...], kp_ref[...], kc_ref[...], kn_ref[...]],
            [cv_ref[...], vp_ref[...], vc_ref[...], vn_ref[...]], valid, sink_ref, o_ref)


def _lat_attention(q, k, v, ctx_k, ctx_v, sink, batch, seq, n_ctx):
    nb = seq // ATTN_BLOCK
    kv = lambda f: pl.BlockSpec((ATTN_BLOCK, SWA_KV), lambda b, i: (b * nb + f(i), 0))
    prev, cur, nxt = kv(lambda i: jnp.maximum(i - 1, 0)), kv(lambda i: i), kv(lambda i: jnp.minimum(i + 1, nb - 1))
    ctx = pl.BlockSpec((n_ctx, SWA_KV), lambda b, i: (b, 0))
    qo = pl.BlockSpec((ATTN_BLOCK, SWA_Q), lambda b, i: (b * nb + i, 0))
    return pl.pallas_call(
        functools.partial(_lat_attn_kernel, seq=seq, n_ctx=n_ctx),
        grid=(batch, nb),
        in_specs=[pl.BlockSpec(memory_space=pltpu.SMEM), qo, ctx, ctx, prev, cur, nxt, prev, cur, nxt],
        out_specs=qo,
        out_shape=jax.ShapeDtypeStruct((batch * seq, SWA_Q), BF16),
        compiler_params=_params(("arbitrary", "arbitrary")),
        name="lat_attention",
    )(sink, q, ctx_k, ctx_v, k, k, k, v, v, v)


def kernel(x_prompt, x_sample, state_l0_gla_fwd, state_l0_gla_bwd, cache_l1_k, cache_l1_v, c, c_ctx, mod_w, mod_b, norm_g, ffn_w1, ffn_w3, ffn_w2, l0_w_in, l0_w_gf, l0_b_gf, l0_w_gb, l0_b_gb, l0_g_head, l0_w_out, l1_w_in, l1_sink, l1_w_out, final_g):
    bp, tp, d = x_prompt.shape
    bs, ts, _ = x_sample.shape
    n_ctx = cache_l1_k.shape[1]
    depth = mod_w.shape[0]
    ctx_row = bs

    w1, w3, w2 = ffn_w1.astype(BF16), ffn_w3.astype(BF16), ffn_w2.astype(BF16)
    gate_lo = 2 * GLA_QK + 2 * GLA_V
    gate_hi = gate_lo + 2 * GLA_GATE_RANK
    l0_wm = jnp.concatenate([l0_w_in[:, :gate_lo], l0_w_in[:, gate_hi:]], axis=1).astype(BF16)
    l0_wg = jnp.pad(l0_w_in[:, gate_lo:gate_hi], ((0, 0), (0, LANES - 2 * GLA_GATE_RANK))).astype(BF16)
    wg2 = jnp.zeros((LANES, 2 * GLA_QK), F32)
    wg2 = wg2.at[:GLA_GATE_RANK, :GLA_QK].set(l0_w_gf)
    wg2 = wg2.at[GLA_GATE_RANK:2 * GLA_GATE_RANK, GLA_QK:].set(l0_w_gb).astype(BF16)
    bg = jnp.concatenate([l0_b_gf, l0_b_gb])[None, :]
    l0_wo, l1_wi, l1_wo = l0_w_out.astype(BF16), l1_w_in.astype(BF16), l1_w_out.astype(BF16)
    g_head = l0_g_head[None, :]
    sink = l1_sink.reshape(-1)
    fin_g = final_g[None, :]

    cond = jnp.zeros((MOD_ROWS, d), F32).at[:bs].set(c).at[ctx_row].set(c_ctx)
    mods = _modulation(cond, mod_w, mod_b)
    mods = mods.reshape(depth, MOD_ROWS, N_MOD, d)
    mods = jnp.pad(mods, ((0, 0), (0, 0), (0, MOD_ROWS - N_MOD), (0, 0)))

    xp = x_prompt.reshape(bp * tp, d)
    xs = x_sample.reshape(bs * ts, d)
    tiles_s = ts // TM
    bfn_p = lambda i: ctx_row
    bfn_s = lambda i: i // tiles_s
    zero_state = jnp.zeros((bp, GLA_QK, GLA_DV), F32)
    s0f = state_l0_gla_fwd.reshape(bs, GLA_QK, GLA_DV)
    s0b = state_l0_gla_bwd.reshape(bs, GLA_QK, GLA_DV)
    rope_tabs = _rope_tables(ts)

    new_state = []
    for layer in range(depth):
        ml = mods[layer]
        ga, gm, gb = (norm_g[layer, r][None, :] for r in range(3))
        fa = (w1[layer, 0], w3[layer, 0], w2[layer, 0])
        fb = (w1[layer, 1], w3[layer, 1], w2[layer, 1])
        last = layer == depth - 1
        xp = _half_ffn(xp, ml, bfn_p, ga, *fa, mrow=0)
        xs = _half_ffn(xs, ml, bfn_s, ga, *fa, mrow=0)
        if layer % 2 == 0:
            outs = []
            for x, bfn, nb, t, sf0, sb0 in ((xp, bfn_p, bp, tp, zero_state, zero_state), (xs, bfn_s, bs, ts, s0f, s0b)):
                q, k, v, og, la, u = _l0_in_proj(x, ml, bfn, gm, l0_wm, l0_wg, wg2, bg)
                gla_out, s_f, s_b = _gla(q, k, v, la, og, sf0, sb0, g_head, nb, t)
                fn_out = _fnet(u, nb, t)
                outs.append(((gla_out, fn_out), s_f, s_b))
            (mix_p, s_f, s_b), (mix_s, _, _) = outs
            new_state += [s_f.reshape(bp, GLA_HEADS, GLA_DK, GLA_DV), s_b.reshape(bp, GLA_HEADS, GLA_DK, GLA_DV)]
            w_out = l0_wo
        else:
            qp, kp, vp = _l1_in_proj(xp, ml, bfn_p, gm, l1_wi)
            mix_p = (_ctx_attention(qp, kp, vp, sink, bp, tp),)
            qs, ks, vs = _l1_in_proj(xs, ml, bfn_s, gm, l1_wi, rope_tabs, tiles_s)
            mix_s = (_lat_attention(qs, ks, vs, cache_l1_k.reshape(bs * n_ctx, SWA_KV),
                                    cache_l1_v.reshape(bs * n_ctx, SWA_KV), sink, bs, ts, n_ctx),)
            new_state += [kp.reshape(bp, tp, SWA_KV_HEADS, SWA_HD), vp.reshape(bp, tp, SWA_KV_HEADS, SWA_HD)]
            w_out = l1_wo
        fin = fin_g if last else None
        xp = _half_ffn(xp, ml, bfn_p, gb, *fb, mrow=6, mixes=mix_p, w_out=w_out, final_g=fin)
        xs = _half_ffn(xs, ml, bfn_s, gb, *fb, mrow=6, mixes=mix_s, w_out=w_out, final_g=fin)
    return (xp.reshape(bp, tp, d), xs.reshape(bs, ts, d), *new_state)
```

```python
import functools

import numpy as np
import jax
import jax.numpy as jnp
from jax import lax
from jax.experimental import pallas as pl
from jax.experimental.pallas import tpu as pltpu

F32 = jnp.float32
BF16 = jnp.bfloat16

D_MODEL = 1024
FFN_DIM = 2816
N_MOD = 9
EPS = 1e-6
GRID_W = 64
GLA_HEADS = 4
GLA_DK = 64
GLA_DV = 128
GLA_GATE_RANK = 16
GLA_TAU = 16.0
GLA_CHUNK = 64
GLA_QK = GLA_HEADS * GLA_DK
GLA_V = GLA_HEADS * GLA_DV
FN_GROUPS = 4
FN_CH = 128
FN_W = FN_GROUPS * FN_CH
SWA_HEADS = 16
SWA_KV_HEADS = 4
SWA_HD = 64
SWA_Q = SWA_HEADS * SWA_HD
SWA_KV = SWA_KV_HEADS * SWA_HD
WINDOW = 128
ATTN_BLOCK = 128
ROPE_BASE = 10000.0
LOG2E = float(np.log2(np.e))

LANES = 128
VMEM_LIMIT = 56 * 1024 * 1024

TM = 512
TF = 256
GLA_TILE = 256
FN_ROWS = 256
MOD_ROWS = 16
MOD_TN = 1024


def _params(sem):
    return pltpu.CompilerParams(dimension_semantics=sem, vmem_limit_bytes=VMEM_LIMIT)


def _resident(shape):
    nd = len(shape)
    return pl.BlockSpec(shape, lambda *_: (0,) * nd, pipeline_mode=pl.Buffered(1))


def _dot(a, b):
    return jnp.dot(a, b, preferred_element_type=F32)


def _silu(x):
    return x * jax.nn.sigmoid(x)


def _ada_norm(x, g, shift, scale):
    ms = jnp.mean(x * x, axis=-1, keepdims=True)
    return (x * lax.rsqrt(ms + EPS) * g) * (1.0 + scale) + shift


def _mod_kernel(c_ref, w_ref, b_ref, o_ref):
    c = c_ref[...]
    s = _silu(c).astype(BF16)
    o_ref[0] = _dot(s, w_ref[0].astype(BF16)) + b_ref[0]


def _modulation(cond, mod_w, mod_b):
    depth, d, n = mod_w.shape
    return pl.pallas_call(
        _mod_kernel,
        grid=(depth, n // MOD_TN),
        in_specs=[
            pl.BlockSpec((MOD_ROWS, d), lambda l, j: (0, 0)),
            pl.BlockSpec((1, d, MOD_TN), lambda l, j: (l, 0, j)),
            pl.BlockSpec((1, 1, MOD_TN), lambda l, j: (l, 0, j)),
        ],
        out_specs=pl.BlockSpec((1, MOD_ROWS, MOD_TN), lambda l, j: (l, 0, j)),
        out_shape=jax.ShapeDtypeStruct((depth, MOD_ROWS, n), F32),
        compiler_params=_params(("arbitrary", "arbitrary")),
        name="modulation",
    )(cond, mod_w, mod_b.reshape(depth, 1, n))


def _ffn_kernel(*refs, n_mix, mrow, final):
    x_ref = refs[0]
    mix_refs = refs[1:1 + n_mix]
    pos = 1 + n_mix
    if n_mix:
        wo_ref = refs[pos]
        pos += 1
    m_ref, g_ref, w1_ref, w3_ref, w2_ref = refs[pos:pos + 5]
    pos += 5
    if final:
        fg_ref = refs[pos]
        pos += 1
    o_ref, acc_ref = refs[pos], refs[pos + 1]

    x = x_ref[...]
    if n_mix:
        off = 0
        mix = None
        for r in mix_refs:
            w = r.shape[1]
            t = _dot(r[...], wo_ref[off:off + w, :])
            mix = t if mix is None else mix + t
            off += w
        x = x + m_ref[0, 5:6, :] * mix
    o_ref[...] = x
    h = _ada_norm(x, g_ref[...], m_ref[0, mrow:mrow + 1, :], m_ref[0, mrow + 1:mrow + 2, :]).astype(BF16)
    for j in range(FFN_DIM // TF):
        a = _dot(h, w1_ref[:, j * TF:(j + 1) * TF])
        b = _dot(h, w3_ref[:, j * TF:(j + 1) * TF])
        t = _dot((_silu(a) * b).astype(BF16), w2_ref[j * TF:(j + 1) * TF, :])
        if j == 0:
            acc_ref[...] = t
        else:
            acc_ref[...] += t
    y = o_ref[...] + (0.5 * m_ref[0, mrow + 2:mrow + 3, :]) * acc_ref[...]
    if final:
        ms = jnp.mean(y * y, axis=-1, keepdims=True)
        y = y * lax.rsqrt(ms + EPS) * fg_ref[...]
    o_ref[...] = y


def _pick(arr, lead):
    tail = arr.shape[len(lead):]
    idx = tuple(lead) + (0,) * len(tail)
    return pl.BlockSpec((None,) * len(lead) + tail, lambda *_: idx, pipeline_mode=pl.Buffered(1))


def _half_ffn(x, mods, bfn, g, w1, w3, w2, wsel, *, mrow, mixes=(), w_out=None, final_g=None):
    m, d = x.shape
    row = lambda i: (i, 0)
    ins = [x]
    specs = [pl.BlockSpec((TM, d), row)]
    for a in mixes:
        ins.append(a)
        specs.append(pl.BlockSpec((TM, a.shape[1]), row))
    if mixes:
        ins.append(w_out)
        specs.append(_resident(w_out.shape))
    ins += [mods, g, w1, w3, w2]
    specs += [pl.BlockSpec((1, MOD_ROWS, d), lambda i: (bfn(i), 0, 0)),
              _resident(g.shape), _pick(w1, wsel), _pick(w3, wsel), _pick(w2, wsel)]
    if final_g is not None:
        ins.append(final_g)
        specs.append(_resident(final_g.shape))
    return pl.pallas_call(
        functools.partial(_ffn_kernel, n_mix=len(mixes), mrow=mrow, final=final_g is not None),
        grid=(m // TM,),
        in_specs=specs,
        out_specs=pl.BlockSpec((TM, d), row),
        out_shape=jax.ShapeDtypeStruct((m, d), F32),
        scratch_shapes=[pltpu.VMEM((TM, d), F32)],
        compiler_params=_params(("arbitrary",)),
        name="half_ffn",
    )(*ins)


def _l0_in_kernel(x_ref, m_ref, g_ref, wm_ref, wg_ref, wg2_ref, bg_ref,
                  q_ref, k_ref, v_ref, og_ref, la_ref, u_ref):
    h = _ada_norm(x_ref[...], g_ref[...], m_ref[0, 3:4, :], m_ref[0, 4:5, :]).astype(BF16)
    q_ref[...] = _dot(h, wm_ref[:, 0:256]) * (GLA_DK ** -0.5)
    k_ref[...] = _dot(h, wm_ref[:, 256:512])
    v_ref[...] = _dot(h, wm_ref[:, 512:1024]).astype(BF16)
    og_ref[...] = _dot(h, wm_ref[:, 1024:1536])
    u_ref[...] = _dot(h, wm_ref[:, 1536:2048]).astype(BF16)
    lr = _dot(h, wg_ref[...]).astype(BF16)
    z = _dot(lr, wg2_ref[...]) + bg_ref[...]
    log_sig = jnp.minimum(z, 0.0) - jnp.log1p(jnp.exp(-jnp.abs(z)))
    la_ref[...] = log_sig * (1.0 / GLA_TAU)


def _l0_in_proj(x, mods, bfn, g, wm, wg, wg2, bg):
    m, d = x.shape
    row = lambda i: (i, 0)
    outs = [(GLA_QK, F32), (GLA_QK, F32), (GLA_V, BF16), (GLA_V, F32), (2 * GLA_QK, F32), (FN_W, BF16)]
    return pl.pallas_call(
        _l0_in_kernel,
        grid=(m // TM,),
        in_specs=[pl.BlockSpec((TM, d), row),
                  pl.BlockSpec((1, MOD_ROWS, d), lambda i: (bfn(i), 0, 0)),
                  _resident(g.shape), _resident(wm.shape), _resident(wg.shape),
                  _resident(wg2.shape), _resident(bg.shape)],
        out_specs=[pl.BlockSpec((TM, w), row) for w, _ in outs],
        out_shape=[jax.ShapeDtypeStruct((m, w), dt) for w, dt in outs],
        compiler_params=_params(("arbitrary",)),
        name="l0_in_proj",
    )(x, mods, g, wm, wg, wg2, bg)


def _gla_tile(q, k, v, g, s_ref, reverse):
    tt = GLA_TILE
    nch = tt // GLA_CHUNK
    r = lax.broadcasted_iota(jnp.int32, (tt, tt), 0)
    c = lax.broadcasted_iota(jnp.int32, (tt, tt), 1)
    same = (r // GLA_CHUNK) == (c // GLA_CHUNK)
    m_intra = jnp.logical_and(same, (c >= r) if reverse else (c <= r))
    tri = jnp.where(m_intra, 1.0, 0.0).astype(BF16)
    g_hi = g.astype(BF16)
    r1 = g - g_hi.astype(F32)
    g_mid = r1.astype(BF16)
    g_lo = (r1 - g_mid.astype(F32)).astype(BF16)
    b = _dot(tri, g_hi) + _dot(tri, g_mid) + _dot(tri, g_lo)
    qd = q * jnp.exp(b)
    b_t = b.T
    k_t = k.T
    cc = lax.broadcasted_iota(jnp.int32, (GLA_QK, tt), 1) // GLA_CHUNK
    end_col = [ci * GLA_CHUNK + (0 if reverse else GLA_CHUNK - 1) for ci in range(nch)]
    ends = [b_t[:, e:e + 1] for e in end_col]
    bl_t = ends[nch - 1]
    for ci in range(nch - 2, -1, -1):
        bl_t = jnp.where(cc == ci, ends[ci], bl_t)
    ki_t = (k_t * jnp.exp(-b_t)).astype(BF16)
    ke_t = k_t * jnp.exp(bl_t - b_t)

    lane = lax.broadcasted_iota(jnp.int32, (tt, GLA_QK), 1) // GLA_DK
    o_intra = []
    for h in range(GLA_HEADS):
        qh = jnp.where(lane == h, qd, 0.0).astype(BF16)
        a = _dot(qh, ki_t)
        a = jnp.where(m_intra, a, 0.0).astype(BF16)
        o_intra.append(_dot(a, v[:, h * GLA_DV:(h + 1) * GLA_DV]))

    lane_c = lax.broadcasted_iota(jnp.int32, (GLA_CHUNK, GLA_QK), 1) // GLA_DK
    s = s_ref[...]
    o_inter = [None] * nch
    for ci in (range(nch - 1, -1, -1) if reverse else range(nch)):
        qd_c = qd[ci * GLA_CHUNK:(ci + 1) * GLA_CHUNK, :]
        q_bd = jnp.concatenate([jnp.where(lane_c == h, qd_c, 0.0).astype(BF16) for h in range(GLA_HEADS)], axis=0)
        o_inter[ci] = _dot(q_bd, s.astype(BF16))
        ke_c = jnp.where(cc == ci, ke_t, 0.0).astype(BF16)
        d_full = _dot(ke_c, v)
        delta = jnp.concatenate(
            [d_full[h * GLA_DK:(h + 1) * GLA_DK, h * GLA_DV:(h + 1) * GLA_DV] for h in range(GLA_HEADS)], axis=0)
        s = jnp.exp(ends[ci]) * s + delta
    s_ref[...] = s

    cols = []
    for h in range(GLA_HEADS):
        inter_h = jnp.concatenate(
            [o_inter[ci][h * GLA_CHUNK:(h + 1) * GLA_CHUNK, :] for ci in range(nch)], axis=0)
        cols.append(o_intra[h] + inter_h)
    return jnp.concatenate(cols, axis=1)


def _gla_kernel(q_ref, k_ref, v_ref, la_ref, og_ref, s0f_ref, s0b_ref, gh_ref,
                o_ref, sf_ref, sb_ref, s_ref, ob_ref, *, nt):
    p = pl.program_id(1)
    i = pl.program_id(2)

    @pl.when(p == 0)
    def _backward():
        @pl.when(i == 0)
        def _():
            s_ref[...] = s0b_ref[0]

        o = _gla_tile(q_ref[...], k_ref[...], v_ref[...], la_ref[...], s_ref, True)
        start = pl.multiple_of((nt - 1 - i) * GLA_TILE, GLA_TILE)
        ob_ref[pl.ds(start, GLA_TILE), :] = o

        @pl.when(i == nt - 1)
        def _():
            sb_ref[0] = s_ref[...]

    @pl.when(p == 1)
    def _forward():
        @pl.when(i == 0)
        def _():
            s_ref[...] = s0f_ref[0]

        o = _gla_tile(q_ref[...], k_ref[...], v_ref[...], la_ref[...], s_ref, False)
        start = pl.multiple_of(i * GLA_TILE, GLA_TILE)
        o = o + ob_ref[pl.ds(start, GLA_TILE), :]
        for h in range(GLA_HEADS):
            sl = slice(h * GLA_DV, (h + 1) * GLA_DV)
            oh = o[:, sl]
            ms = jnp.mean(oh * oh, axis=-1, keepdims=True)
            y = oh * lax.rsqrt(ms + EPS) * gh_ref[...]
            o_ref[:, sl] = (y * _silu(og_ref[:, sl])).astype(BF16)

        @pl.when(i == nt - 1)
        def _():
            sf_ref[0] = s_ref[...]


def _gla(q, k, v, la, og, s0f, s0b, g_head, batch, seq):
    nt = seq // GLA_TILE
    tile = lambda b, p, i: (b * nt + p * i + (1 - p) * (nt - 1 - i), 0)
    st = pl.BlockSpec((1, GLA_QK, GLA_DV), lambda b, p, i: (b, 0, 0))
    return pl.pallas_call(
        functools.partial(_gla_kernel, nt=nt),
        grid=(batch, 2, nt),
        in_specs=[pl.BlockSpec((GLA_TILE, GLA_QK), tile),
                  pl.BlockSpec((GLA_TILE, GLA_QK), tile),
                  pl.BlockSpec((GLA_TILE, GLA_V), tile),
                  pl.BlockSpec((GLA_TILE, GLA_QK), lambda b, p, i: (tile(b, p, i)[0], 1 - p)),
                  pl.BlockSpec((GLA_TILE, GLA_V), lambda b, p, i: (b * nt + p * i, 0)),
                  st, st, _resident(g_head.shape)],
        out_specs=[pl.BlockSpec((GLA_TILE, GLA_V), lambda b, p, i: (b * nt + p * i, 0)), st, st],
        out_shape=[jax.ShapeDtypeStruct((batch * seq, GLA_V), BF16),
                   jax.ShapeDtypeStruct((batch, GLA_QK, GLA_DV), F32),
                   jax.ShapeDtypeStruct((batch, GLA_QK, GLA_DV), F32)],
        scratch_shapes=[pltpu.VMEM((GLA_QK, GLA_DV), F32), pltpu.VMEM((seq, GLA_V), F32)],
        compiler_params=_params(("arbitrary", "arbitrary", "arbitrary")),
        name="gla",
    )(q, k, v, la, og, s0f, s0b, g_head)


def _fnet_kernel(u_ref, cc_ref, sc_ref, cs_ref, o_ref, ucs_ref, *, seq, scale):
    @pl.when(pl.program_id(1) == 0)
    def _():
        for gi in range(FN_GROUPS):
            sl = slice(gi * FN_CH, (gi + 1) * FN_CH)
            ug = u_ref[:, sl]
            ucs_ref[0:seq, sl] = _dot(ug, cc_ref[...]).astype(BF16)
            ucs_ref[seq:2 * seq, sl] = _dot(ug, sc_ref[...]).astype(BF16)

    o_ref[...] = (_dot(cs_ref[...], ucs_ref[...]) * scale).astype(BF16)


def _dft_tables(seq):
    def cs(n):
        idx = np.arange(n, dtype=np.int64)
        ang = 2.0 * np.pi * ((idx[:, None] * idx[None, :]) % n).astype(np.float64) / n
        return np.cos(ang), np.sin(ang)
    ct, st = cs(seq)
    cc, sc = cs(FN_CH)
    to = lambda a: jnp.asarray(a.astype(np.float32)).astype(BF16)
    return to(np.concatenate([ct, -st], axis=1)), to(cc), to(sc)


def _fnet(u, batch, seq):
    cs, cc, sc = _dft_tables(seq)
    rows = min(FN_ROWS, seq)
    nj = seq // rows
    return pl.pallas_call(
        functools.partial(_fnet_kernel, seq=seq, scale=float((seq * FN_CH) ** -0.5)),
        grid=(batch, nj),
        in_specs=[pl.BlockSpec((seq, FN_W), lambda b, j: (b, 0)),
                  _resident(cc.shape), _resident(sc.shape),
                  pl.BlockSpec((rows, 2 * seq), lambda b, j: (j, 0))],
        out_specs=pl.BlockSpec((rows, FN_W), lambda b, j: (b * nj + j, 0)),
        out_shape=jax.ShapeDtypeStruct((batch * seq, FN_W), BF16),
        scratch_shapes=[pltpu.VMEM((2 * seq, FN_W), BF16)],
        compiler_params=_params(("arbitrary", "arbitrary")),
        name="fnet",
    )(u, cc, sc, cs)


def _rope128(x, cos, sin_signed, odd):
    back = pltpu.roll(x, 16, 1)
    fwd = pltpu.roll(x, 112, 1)
    return x * cos + jnp.where(odd, back, fwd) * sin_signed


def _l1_in_kernel(*refs, rope):
    if rope:
        x_ref, m_ref, g_ref, w_ref, cos_ref, sin_ref, q_ref, k_ref, v_ref = refs
        cos, sin = cos_ref[...], sin_ref[...]
        odd = (lax.broadcasted_iota(jnp.int32, cos.shape, 1) // 16) % 2 == 1
    else:
        x_ref, m_ref, g_ref, w_ref, q_ref, k_ref, v_ref = refs
    h = _ada_norm(x_ref[...], g_ref[...], m_ref[0, 3:4, :], m_ref[0, 4:5, :]).astype(BF16)

    def project(dst, col0, width, scale):
        for j in range(width // 256):
            y = _dot(h, w_ref[:, col0 + 256 * j:col0 + 256 * (j + 1)])
            if scale != 1.0:
                y = y * scale
            for half in range(2):
                yh = y[:, LANES * half:LANES * (half + 1)]
                if rope:
                    yh = _rope128(yh, cos, sin, odd)
                lo = 256 * j + LANES * half
                dst[:, lo:lo + LANES] = yh.astype(dst.dtype)

    project(q_ref, 0, SWA_Q, SWA_HD ** -0.5 * LOG2E)
    project(k_ref, SWA_Q, SWA_KV, 1.0)
    v_ref[...] = _dot(h, w_ref[:, SWA_Q + SWA_KV:SWA_Q + 2 * SWA_KV])


def _l1_in_proj(x, mods, bfn, g, w, rope_tabs=None, tiles_per_seq=None):
    m, d = x.shape
    row = lambda i: (i, 0)
    ins = [x, mods, g, w]
    specs = [pl.BlockSpec((TM, d), row),
             pl.BlockSpec((1, MOD_ROWS, d), lambda i: (bfn(i), 0, 0)),
             _resident(g.shape), _resident(w.shape)]
    if rope_tabs is not None:
        ins += list(rope_tabs)
        specs += [pl.BlockSpec((TM, LANES), lambda i: (i % tiles_per_seq, 0))] * 2
    return pl.pallas_call(
        functools.partial(_l1_in_kernel, rope=rope_tabs is not None),
        grid=(m // TM,),
        in_specs=specs,
        out_specs=[pl.BlockSpec((TM, SWA_Q), row), pl.BlockSpec((TM, SWA_KV), row), pl.BlockSpec((TM, SWA_KV), row)],
        out_shape=[jax.ShapeDtypeStruct((m, SWA_Q), BF16),
                   jax.ShapeDtypeStruct((m, SWA_KV), F32),
                   jax.ShapeDtypeStruct((m, SWA_KV), F32)],
        compiler_params=_params(("arbitrary",)),
        name="l1_in_proj",
    )(*ins)


def _rope_tables(seq):
    rows = seq // GRID_W
    row = jnp.repeat(jnp.arange(rows), GRID_W).astype(F32)
    col = (jnp.arange(rows * GRID_W) % GRID_W).astype(F32)
    n_freq = SWA_HD // 4
    inv = ROPE_BASE ** (-jnp.arange(n_freq, dtype=F32) / n_freq)
    ar = row[:, None] * inv
    ac = col[:, None] * inv
    ang = jnp.concatenate([ar, ar, ac, ac], axis=-1)
    cos, sin = jnp.cos(ang), jnp.sin(ang)
    sign = jnp.where((jnp.arange(SWA_HD) // n_freq) % 2 == 1, 1.0, -1.0).astype(F32)
    two = lambda t: jnp.concatenate([t, t], axis=-1)
    return two(cos), two(sin * sign)


def _attend(q_ref, ks, vs, biases, sink_ref, o_ref):
    kcat = ks[0] if len(ks) == 1 else jnp.concatenate(ks, axis=0)
    vcat = vs[0] if len(vs) == 1 else jnp.concatenate(vs, axis=0)
    n = kcat.shape[0]
    r = q_ref.shape[0]
    nblk = n // LANES
    low = lax.broadcasted_iota(jnp.int32, (n, LANES), 1) < SWA_HD
    ones_lo = jnp.where(low, 1.0, 0.0)
    ones_hi = jnp.where(low, 0.0, 1.0)
    low_out = lax.broadcasted_iota(jnp.int32, (2 * r, LANES), 1) < SWA_HD
    top = lax.broadcasted_iota(jnp.int32, (2 * r, 1), 0) < r
    biases = {t: jnp.concatenate([b, b], axis=0) for t, b in biases.items()}
    nt_dims = (((1,), (1,)), ((), ()))
    for g in range(SWA_KV_HEADS):
        sl = slice(LANES * (g // 2), LANES * (g // 2 + 1))
        kcol, vcol = kcat[:, sl], vcat[:, sl]
        ksw, vsw = pltpu.roll(kcol, SWA_HD, 1), pltpu.roll(vcol, SWA_HD, 1)
        if g % 2 == 0:
            k_lo, k_hi = jnp.where(low, kcol, 0.0), jnp.where(low, 0.0, ksw)
            v_lo, v_hi = jnp.where(low, vcol, 0.0), jnp.where(low, 0.0, vsw)
        else:
            k_lo, k_hi = jnp.where(low, ksw, 0.0), jnp.where(low, 0.0, kcol)
            v_lo, v_hi = jnp.where(low, vsw, 0.0), jnp.where(low, 0.0, vcol)
        k_bd = jnp.concatenate([k_lo, k_hi], axis=0).astype(BF16)
        v_bd = jnp.concatenate([jnp.concatenate([v_lo, ones_lo], axis=1),
                                jnp.concatenate([v_hi, ones_hi], axis=1)], axis=0).astype(BF16)
        qs = jnp.concatenate([q_ref[:, LANES * (2 * g):LANES * (2 * g + 1)],
                              q_ref[:, LANES * (2 * g + 1):LANES * (2 * g + 2)]], axis=0)
        s = lax.dot_general(qs, k_bd, nt_dims, preferred_element_type=F32)
        probs, sink_terms = [], []
        for e in range(2):
            blocks = [s[:, e * n + LANES * t:e * n + LANES * (t + 1)] for t in range(nblk)]
            for t, b in biases.items():
                blocks[t] = blocks[t] + b
            mx = blocks[0]
            for b in blocks[1:]:
                mx = jnp.maximum(mx, b)
            sk = jnp.where(top, sink_ref[4 * g + e], sink_ref[4 * g + 2 + e]) * LOG2E
            m = jnp.maximum(jnp.max(mx, axis=-1, keepdims=True), sk)
            probs += [jnp.exp2(b - m).astype(BF16) for b in blocks]
            sink_terms.append(jnp.exp2(sk - m))
        res = _dot(jnp.concatenate(probs, axis=1), v_bd)
        denom = res[:, LANES:2 * LANES] + jnp.where(low_out, sink_terms[0], sink_terms[1])
        out = (res[:, 0:LANES] / denom).astype(o_ref.dtype)
        o_ref[:, LANES * (2 * g):LANES * (2 * g + 1)] = out[0:r]
        o_ref[:, LANES * (2 * g + 1):LANES * (2 * g + 2)] = out[r:2 * r]


def _ctx_attn_kernel(sink_ref, q_ref, k_ref, v_ref, o_ref):
    _attend(q_ref, [k_ref[...]], [v_ref[...]], {}, sink_ref, o_ref)


def _ctx_attention(q, k, v, sink, batch, seq):
    blk = lambda w: pl.BlockSpec((seq, w), lambda b: (b, 0))
    return pl.pallas_call(
        _ctx_attn_kernel,
        grid=(batch,),
        in_specs=[pl.BlockSpec(memory_space=pltpu.SMEM), blk(SWA_Q), blk(SWA_KV), blk(SWA_KV)],
        out_specs=blk(SWA_Q),
        out_shape=jax.ShapeDtypeStruct((batch * seq, SWA_Q), BF16),
        compiler_params=_params(("arbitrary",)),
        name="ctx_attention",
    )(sink, q, k, v)


def _lat_attn_kernel(sink_ref, q_ref, ck_ref, cv_ref, kp_ref, kc_ref, kn_ref, vp_ref, vc_ref, vn_ref, o_ref,
                     *, nb, n_ctx):
    i = pl.program_id(1)
    blk = ATTN_BLOCK
    row = lax.broadcasted_iota(jnp.int32, (blk, blk), 0)
    col = lax.broadcasted_iota(jnp.int32, (blk, blk), 1)
    off_prev = jnp.where(i > 0, 0, blk)
    off_next = jnp.where(i < nb - 1, 0, blk)
    bias_prev = jnp.where(col - row >= off_prev, 0.0, -jnp.inf)
    bias_next = jnp.where(row - col >= off_next, 0.0, -jnp.inf)
    first = n_ctx // LANES
    _attend(q_ref, [ck_ref[...], kp_ref[...], kc_ref[...], kn_ref[...]],
            [cv_ref[...], vp_ref[...], vc_ref[...], vn_ref[...]],
            {first: bias_prev, first + 2: bias_next}, sink_ref, o_ref)


def _lat_attention(q, k, v, ctx_k, ctx_v, sink, batch, seq, n_ctx):
    nb = seq // ATTN_BLOCK
    kv = lambda f: pl.BlockSpec((ATTN_BLOCK, SWA_KV), lambda b, i: (b * nb + f(i), 0))
    prev, cur, nxt = kv(lambda i: jnp.maximum(i - 1, 0)), kv(lambda i: i), kv(lambda i: jnp.minimum(i + 1, nb - 1))
    ctx = pl.BlockSpec((n_ctx, SWA_KV), lambda b, i: (b, 0))
    qo = pl.BlockSpec((ATTN_BLOCK, SWA_Q), lambda b, i: (b * nb + i, 0))
    return pl.pallas_call(
        functools.partial(_lat_attn_kernel, nb=nb, n_ctx=n_ctx),
        grid=(batch, nb),
        in_specs=[pl.BlockSpec(memory_space=pltpu.SMEM), qo, ctx, ctx, prev, cur, nxt, prev, cur, nxt],
        out_specs=qo,
        out_shape=jax.ShapeDtypeStruct((batch * seq, SWA_Q), BF16),
        compiler_params=_params(("arbitrary", "arbitrary")),
        name="lat_attention",
    )(sink, q, ctx_k, ctx_v, k, k, k, v, v, v)


def kernel(x_prompt, x_sample, state_l0_gla_fwd, state_l0_gla_bwd, cache_l1_k, cache_l1_v, c, c_ctx, mod_w, mod_b, norm_g, ffn_w1, ffn_w3, ffn_w2, l0_w_in, l0_w_gf, l0_b_gf, l0_w_gb, l0_b_gb, l0_g_head, l0_w_out, l1_w_in, l1_sink, l1_w_out, final_g):
    bp, tp, d = x_prompt.shape
    bs, ts, _ = x_sample.shape
    n_ctx = cache_l1_k.shape[1]
    depth = mod_w.shape[0]
    ctx_row = bs

    w1, w3, w2 = ffn_w1.astype(BF16), ffn_w3.astype(BF16), ffn_w2.astype(BF16)
    gate_lo = 2 * GLA_QK + 2 * GLA_V
    gate_hi = gate_lo + 2 * GLA_GATE_RANK
    l0_wm = jnp.concatenate([l0_w_in[:, :gate_lo], l0_w_in[:, gate_hi:]], axis=1).astype(BF16)
    l0_wg = jnp.pad(l0_w_in[:, gate_lo:gate_hi], ((0, 0), (0, LANES - 2 * GLA_GATE_RANK))).astype(BF16)
    wg2 = jnp.zeros((LANES, 2 * GLA_QK), F32)
    wg2 = wg2.at[:GLA_GATE_RANK, :GLA_QK].set(l0_w_gf)
    wg2 = wg2.at[GLA_GATE_RANK:2 * GLA_GATE_RANK, GLA_QK:].set(l0_w_gb).astype(BF16)
    bg = jnp.concatenate([l0_b_gf, l0_b_gb])[None, :]
    l0_wo, l1_wi, l1_wo = l0_w_out.astype(BF16), l1_w_in.astype(BF16), l1_w_out.astype(BF16)
    g_head = l0_g_head[None, :]
    sink = l1_sink.reshape(-1)
    fin_g = final_g[None, :]

    cond = jnp.zeros((MOD_ROWS, d), F32).at[:bs].set(c).at[ctx_row].set(c_ctx)
    mods = _modulation(cond, mod_w, mod_b)
    mods = mods.reshape(depth, MOD_ROWS, N_MOD, d)
    mods = jnp.pad(mods, ((0, 0), (0, 0), (0, MOD_ROWS - N_MOD), (0, 0)))

    xp = x_prompt.reshape(bp * tp, d)
    xs = x_sample.reshape(bs * ts, d)
    tiles_s = ts // TM
    bfn_p = lambda i: ctx_row
    bfn_s = lambda i: i // tiles_s
    zero_state = jnp.zeros((bp, GLA_QK, GLA_DV), F32)
    s0f = state_l0_gla_fwd.reshape(bs, GLA_QK, GLA_DV)
    s0b = state_l0_gla_bwd.reshape(bs, GLA_QK, GLA_DV)
    rope_tabs = _rope_tables(ts)

    new_state = []
    for layer in range(depth):
        ml = mods[layer]
        ga, gm, gb = (norm_g[layer, r][None, :] for r in range(3))
        fa = (w1, w3, w2, (layer, 0))
        fb = (w1, w3, w2, (layer, 1))
        last = layer == depth - 1
        xp = _half_ffn(xp, ml, bfn_p, ga, *fa, mrow=0)
        xs = _half_ffn(xs, ml, bfn_s, ga, *fa, mrow=0)
        if layer % 2 == 0:
            outs = []
            for x, bfn, nb, t, sf0, sb0 in ((xp, bfn_p, bp, tp, zero_state, zero_state), (xs, bfn_s, bs, ts, s0f, s0b)):
                q, k, v, og, la, u = _l0_in_proj(x, ml, bfn, gm, l0_wm, l0_wg, wg2, bg)
                gla_out, s_f, s_b = _gla(q, k, v, la, og, sf0, sb0, g_head, nb, t)
                fn_out = _fnet(u, nb, t)
                outs.append(((gla_out, fn_out), s_f, s_b))
            (mix_p, s_f, s_b), (mix_s, _, _) = outs
            new_state += [s_f.reshape(bp, GLA_HEADS, GLA_DK, GLA_DV), s_b.reshape(bp, GLA_HEADS, GLA_DK, GLA_DV)]
            w_out = l0_wo
        else:
            qp, kp, vp = _l1_in_proj(xp, ml, bfn_p, gm, l1_wi)
            mix_p = (_ctx_attention(qp, kp, vp, sink, bp, tp),)
            qs, ks, vs = _l1_in_proj(xs, ml, bfn_s, gm, l1_wi, rope_tabs, tiles_s)
            mix_s = (_lat_attention(qs, ks, vs, cache_l1_k.reshape(bs * n_ctx, SWA_KV),
                                    cache_l1_v.reshape(bs * n_ctx, SWA_KV), sink, bs, ts, n_ctx),)
            new_state += [kp.reshape(bp, tp, SWA_KV_HEADS, SWA_HD), vp.reshape(bp, tp, SWA_KV_HEADS, SWA_HD)]
            w_out = l1_wo
        fin = fin_g if last else None
        xp = _half_ffn(xp, ml, bfn_p, gb, *fb, mrow=6, mixes=mix_p, w_out=w_out, final_g=fin)
        xs = _half_ffn(xs, ml, bfn_s, gb, *fb, mrow=6, mixes=mix_s, w_out=w_out, final_g=fin)
    return (xp.reshape(bp, tp, d), xs.reshape(bs, ts, d), *new_state)
```

```python
import functools

import numpy as np
import jax
import jax.numpy as jnp
from jax import lax
from jax.experimental import pallas as pl
from jax.experimental.pallas import tpu as pltpu

F32 = jnp.float32
BF16 = jnp.bfloat16

D_MODEL = 1024
FFN_DIM = 2816
N_MOD = 9
EPS = 1e-6
GRID_W = 64
GLA_HEADS = 4
GLA_DK = 64
GLA_DV = 128
GLA_GATE_RANK = 16
GLA_TAU = 16.0
GLA_CHUNK = 64
GLA_QK = GLA_HEADS * GLA_DK
GLA_V = GLA_HEADS * GLA_DV
FN_GROUPS = 4
FN_CH = 128
FN_W = FN_GROUPS * FN_CH
SWA_HEADS = 16
SWA_KV_HEADS = 4
SWA_HD = 64
SWA_Q = SWA_HEADS * SWA_HD
SWA_KV = SWA_KV_HEADS * SWA_HD
WINDOW = 128
ATTN_BLOCK = 128
ROPE_BASE = 10000.0
LOG2E = float(np.log2(np.e))

LANES = 128
VMEM_LIMIT = 56 * 1024 * 1024

TM = 1024
SUB = 512
SUB_PROJ = 256
TF = 256
GLA_TILE = 256
GLA_GROUP = 2
FN_ROWS = 256
MOD_ROWS = 16
MOD_TN = 1024


def _params(sem):
    return pltpu.CompilerParams(dimension_semantics=sem, vmem_limit_bytes=VMEM_LIMIT)


def _resident(shape):
    nd = len(shape)
    return pl.BlockSpec(shape, lambda *_: (0,) * nd, pipeline_mode=pl.Buffered(1))


def _dot(a, b):
    return jnp.dot(a, b, preferred_element_type=F32)


def _silu(x):
    return x * jax.nn.sigmoid(x)


def _ada_norm(x, g, shift, scale):
    ms = jnp.mean(x * x, axis=-1, keepdims=True)
    return (x * lax.rsqrt(ms + EPS) * g) * (1.0 + scale) + shift


def _mod_kernel(c_ref, w_ref, b_ref, o_ref):
    c = c_ref[...]
    s = _silu(c).astype(BF16)
    o_ref[0] = _dot(s, w_ref[0].astype(BF16)) + b_ref[0]


def _modulation(cond, mod_w, mod_b):
    depth, d, n = mod_w.shape
    return pl.pallas_call(
        _mod_kernel,
        grid=(depth, n // MOD_TN),
        in_specs=[
            pl.BlockSpec((MOD_ROWS, d), lambda l, j: (0, 0)),
            pl.BlockSpec((1, d, MOD_TN), lambda l, j: (l, 0, j)),
            pl.BlockSpec((1, 1, MOD_TN), lambda l, j: (l, 0, j)),
        ],
        out_specs=pl.BlockSpec((1, MOD_ROWS, MOD_TN), lambda l, j: (l, 0, j)),
        out_shape=jax.ShapeDtypeStruct((depth, MOD_ROWS, n), F32),
        compiler_params=_params(("arbitrary", "arbitrary")),
        name="modulation",
    )(cond, mod_w, mod_b.reshape(depth, 1, n))


def _ffn_kernel(*refs, n_mix, mrow, final):
    x_ref = refs[0]
    mix_refs = refs[1:1 + n_mix]
    pos = 1 + n_mix
    if n_mix:
        wo_ref = refs[pos]
        pos += 1
    m_ref, g_ref, w1_ref, w3_ref, w2_ref = refs[pos:pos + 5]
    pos += 5
    if final:
        fg_ref = refs[pos]
        pos += 1
    o_ref, acc_ref, h_ref = refs[pos], refs[pos + 1], refs[pos + 2]

    def prologue(rows):
        x = x_ref[rows, :]
        if n_mix:
            off = 0
            mix = None
            for r in mix_refs:
                w = r.shape[1]
                t = _dot(r[rows, :], wo_ref[off:off + w, :])
                mix = t if mix is None else mix + t
                off += w
            x = x + m_ref[0, 5:6, :] * mix
        o_ref[rows, :] = x
        h = _ada_norm(x, g_ref[...], m_ref[0, mrow:mrow + 1, :], m_ref[0, mrow + 1:mrow + 2, :])
        h_ref[rows, :] = h.astype(BF16)

    def chunk(rows, j):
        h = h_ref[rows, :]
        a = _dot(h, w1_ref[:, j * TF:(j + 1) * TF])
        b = _dot(h, w3_ref[:, j * TF:(j + 1) * TF])
        t = _dot((_silu(a) * b).astype(BF16), w2_ref[j * TF:(j + 1) * TF, :])
        if j == 0:
            acc_ref[rows, :] = t
        else:
            acc_ref[rows, :] += t

    def epilogue(rows):
        y = o_ref[rows, :] + (0.5 * m_ref[0, mrow + 2:mrow + 3, :]) * acc_ref[rows, :]
        if final:
            ms = jnp.mean(y * y, axis=-1, keepdims=True)
            y = y * lax.rsqrt(ms + EPS) * fg_ref[...]
        o_ref[rows, :] = y

    subs = [slice(r0, r0 + SUB) for r0 in range(0, x_ref.shape[0], SUB)]
    for rows in subs:
        prologue(rows)
    for j in range(FFN_DIM // TF):
        for rows in subs:
            chunk(rows, j)
    for rows in subs:
        epilogue(rows)


def _pick(arr, lead):
    tail = arr.shape[len(lead):]
    idx = tuple(lead) + (0,) * len(tail)
    return pl.BlockSpec((None,) * len(lead) + tail, lambda *_: idx, pipeline_mode=pl.Buffered(1))


def _half_ffn(x, mods, bfn, g, w1, w3, w2, wsel, *, mrow, mixes=(), w_out=None, final_g=None):
    m, d = x.shape
    row = lambda i: (i, 0)
    ins = [x]
    specs = [pl.BlockSpec((TM, d), row)]
    for a in mixes:
        ins.append(a)
        specs.append(pl.BlockSpec((TM, a.shape[1]), row))
    if mixes:
        ins.append(w_out)
        specs.append(_resident(w_out.shape))
    ins += [mods, g, w1, w3, w2]
    specs += [pl.BlockSpec((1, MOD_ROWS, d), lambda i: (bfn(i), 0, 0)),
              _resident(g.shape), _pick(w1, wsel), _pick(w3, wsel), _pick(w2, wsel)]
    if final_g is not None:
        ins.append(final_g)
        specs.append(_resident(final_g.shape))
    return pl.pallas_call(
        functools.partial(_ffn_kernel, n_mix=len(mixes), mrow=mrow, final=final_g is not None),
        grid=(m // TM,),
        in_specs=specs,
        out_specs=pl.BlockSpec((TM, d), row),
        out_shape=jax.ShapeDtypeStruct((m, d), F32),
        scratch_shapes=[pltpu.VMEM((TM, d), F32), pltpu.VMEM((TM, d), BF16)],
        compiler_params=_params(("arbitrary",)),
        name="half_ffn",
    )(*ins)


def _l0_in_kernel(x_ref, m_ref, g_ref, wm_ref, wg_ref, wg2_ref, bg_ref,
                  q_ref, k_ref, v_ref, og_ref, la_ref, u_ref):
    for r0 in range(0, x_ref.shape[0], SUB_PROJ):
        rows = slice(r0, r0 + SUB_PROJ)
        h = _ada_norm(x_ref[rows, :], g_ref[...], m_ref[0, 3:4, :], m_ref[0, 4:5, :]).astype(BF16)
        q_ref[rows, :] = _dot(h, wm_ref[:, 0:256]) * (GLA_DK ** -0.5)
        k_ref[rows, :] = _dot(h, wm_ref[:, 256:512])
        v_ref[rows, :] = _dot(h, wm_ref[:, 512:1024]).astype(BF16)
        og_ref[rows, :] = _dot(h, wm_ref[:, 1024:1536])
        u_ref[rows, :] = _dot(h, wm_ref[:, 1536:2048]).astype(BF16)
        lr = _dot(h, wg_ref[...]).astype(BF16)
        z = _dot(lr, wg2_ref[...]) + bg_ref[...]
        log_sig = jnp.minimum(z, 0.0) - jnp.log1p(jnp.exp(-jnp.abs(z)))
        la_ref[rows, :] = log_sig * (1.0 / GLA_TAU)


def _l0_in_proj(x, mods, bfn, g, wm, wg, wg2, bg):
    m, d = x.shape
    row = lambda i: (i, 0)
    outs = [(GLA_QK, F32), (GLA_QK, F32), (GLA_V, BF16), (GLA_V, F32), (2 * GLA_QK, F32), (FN_W, BF16)]
    return pl.pallas_call(
        _l0_in_kernel,
        grid=(m // TM,),
        in_specs=[pl.BlockSpec((TM, d), row),
                  pl.BlockSpec((1, MOD_ROWS, d), lambda i: (bfn(i), 0, 0)),
                  _resident(g.shape), _resident(wm.shape), _resident(wg.shape),
                  _resident(wg2.shape), _resident(bg.shape)],
        out_specs=[pl.BlockSpec((TM, w), row) for w, _ in outs],
        out_shape=[jax.ShapeDtypeStruct((m, w), dt) for w, dt in outs],
        compiler_params=_params(("arbitrary",)),
        name="l0_in_proj",
    )(x, mods, g, wm, wg, wg2, bg)


def _gla_tile(q, k, v, g, s, reverse):
    tt = GLA_TILE
    nch = tt // GLA_CHUNK
    r = lax.broadcasted_iota(jnp.int32, (tt, tt), 0)
    c = lax.broadcasted_iota(jnp.int32, (tt, tt), 1)
    same = (r // GLA_CHUNK) == (c // GLA_CHUNK)
    m_intra = jnp.logical_and(same, (c >= r) if reverse else (c <= r))
    tri = jnp.where(m_intra, 1.0, 0.0).astype(BF16)
    g_hi = g.astype(BF16)
    g_lo = (g - g_hi.astype(F32)).astype(BF16)
    b = _dot(tri, g_hi) + _dot(tri, g_lo)
    qd = q * jnp.exp(b)
    b_t = b.T
    k_t = k.T
    cc = lax.broadcasted_iota(jnp.int32, (GLA_QK, tt), 1) // GLA_CHUNK
    end_col = [ci * GLA_CHUNK + (0 if reverse else GLA_CHUNK - 1) for ci in range(nch)]
    ends = [b_t[:, e:e + 1] for e in end_col]
    bl_t = ends[nch - 1]
    for ci in range(nch - 2, -1, -1):
        bl_t = jnp.where(cc == ci, ends[ci], bl_t)
    ki_t = (k_t * jnp.exp(-b_t)).astype(BF16)
    ke_t = k_t * jnp.exp(bl_t - b_t)

    lane = lax.broadcasted_iota(jnp.int32, (tt, GLA_QK), 1) // GLA_DK
    o_intra = []
    for h in range(GLA_HEADS):
        qh = jnp.where(lane == h, qd, 0.0).astype(BF16)
        a = _dot(qh, ki_t)
        a = jnp.where(m_intra, a, 0.0).astype(BF16)
        o_intra.append(_dot(a, v[:, h * GLA_DV:(h + 1) * GLA_DV]))

    zero_v = jnp.zeros((GLA_CHUNK, GLA_DV), BF16)
    deltas = []
    for h in range(GLA_HEADS):
        v_h = v[:, h * GLA_DV:(h + 1) * GLA_DV]
        v_bd = jnp.concatenate(
            [jnp.concatenate([v_h[ci * GLA_CHUNK:(ci + 1) * GLA_CHUNK] if cj == ci else zero_v
                              for cj in range(nch)], axis=1) for ci in range(nch)], axis=0)
        deltas.append(_dot(ke_t[h * GLA_DK:(h + 1) * GLA_DK, :].astype(BF16), v_bd))

    s_h = [s[h * GLA_DK:(h + 1) * GLA_DK, :] for h in range(GLA_HEADS)]
    zero_s = jnp.zeros((GLA_DK, GLA_DV), BF16)
    o_inter = [None] * nch
    for ci in (range(nch - 1, -1, -1) if reverse else range(nch)):
        s_bd = jnp.concatenate(
            [jnp.concatenate([s_h[h].astype(BF16) if hj == h else zero_s for hj in range(GLA_HEADS)], axis=1)
             for h in range(GLA_HEADS)], axis=0)
        o_inter[ci] = _dot(qd[ci * GLA_CHUNK:(ci + 1) * GLA_CHUNK, :].astype(BF16), s_bd)
        decay = jnp.exp(ends[ci])
        s_h = [decay[h * GLA_DK:(h + 1) * GLA_DK] * s_h[h] + deltas[h][:, ci * GLA_DV:(ci + 1) * GLA_DV]
               for h in range(GLA_HEADS)]
    o = jnp.concatenate(o_intra, axis=1) + jnp.concatenate(o_inter, axis=0)
    return o, jnp.concatenate(s_h, axis=0)


def _gla_kernel(q_ref, k_ref, v_ref, la_ref, og_ref, s0f_ref, s0b_ref, gh_ref,
                o_ref, sf_ref, sb_ref, s_ref, ob_ref, *, nt):
    p = pl.program_id(1)
    i = pl.program_id(2)

    @pl.when(p == 0)
    def _backward():
        @pl.when(i == 0)
        def _():
            s_ref[...] = s0b_ref[...]

        start = pl.multiple_of((nt - 1 - i) * GLA_TILE, GLA_TILE)
        for gi in range(GLA_GROUP):
            o, s = _gla_tile(q_ref[gi], k_ref[gi], v_ref[gi], la_ref[gi], s_ref[gi], True)
            s_ref[gi] = s
            ob_ref[gi, pl.ds(start, GLA_TILE), :] = o

        @pl.when(i == nt - 1)
        def _():
            sb_ref[...] = s_ref[...]

    @pl.when(p == 1)
    def _forward():
        @pl.when(i == 0)
        def _():
            s_ref[...] = s0f_ref[...]

        start = pl.multiple_of(i * GLA_TILE, GLA_TILE)
        for gi in range(GLA_GROUP):
            o, s = _gla_tile(q_ref[gi], k_ref[gi], v_ref[gi], la_ref[gi], s_ref[gi], False)
            s_ref[gi] = s
            o = o + ob_ref[gi, pl.ds(start, GLA_TILE), :]
            for h in range(GLA_HEADS):
                sl = slice(h * GLA_DV, (h + 1) * GLA_DV)
                oh = o[:, sl]
                ms = jnp.mean(oh * oh, axis=-1, keepdims=True)
                y = oh * lax.rsqrt(ms + EPS) * gh_ref[...]
                o_ref[gi, :, sl] = (y * _silu(og_ref[gi, :, sl])).astype(BF16)

        @pl.when(i == nt - 1)
        def _():
            sf_ref[...] = s_ref[...]


def _gla(q, k, v, la, og, s0f, s0b, g_head, batch, seq):
    nt = seq // GLA_TILE
    gg = GLA_GROUP
    tile = lambda b, p, i: (b, p * i + (1 - p) * (nt - 1 - i), 0)
    fwd_tile = lambda b, p, i: (b, p * i, 0)
    st = pl.BlockSpec((gg, GLA_QK, GLA_DV), lambda b, p, i: (b, 0, 0))
    return pl.pallas_call(
        functools.partial(_gla_kernel, nt=nt),
        grid=(batch // gg, 2, nt),
        in_specs=[pl.BlockSpec((gg, GLA_TILE, GLA_QK), tile),
                  pl.BlockSpec((gg, GLA_TILE, GLA_QK), tile),
                  pl.BlockSpec((gg, GLA_TILE, GLA_V), tile),
                  pl.BlockSpec((gg, GLA_TILE, GLA_QK), lambda b, p, i: tile(b, p, i)[:2] + (1 - p,)),
                  pl.BlockSpec((gg, GLA_TILE, GLA_V), fwd_tile),
                  st, st, _resident(g_head.shape)],
        out_specs=[pl.BlockSpec((gg, GLA_TILE, GLA_V), fwd_tile), st, st],
        out_shape=[jax.ShapeDtypeStruct((batch, seq, GLA_V), BF16),
                   jax.ShapeDtypeStruct((batch, GLA_QK, GLA_DV), F32),
                   jax.ShapeDtypeStruct((batch, GLA_QK, GLA_DV), F32)],
        scratch_shapes=[pltpu.VMEM((gg, GLA_QK, GLA_DV), F32), pltpu.VMEM((gg, seq, GLA_V), F32)],
        compiler_params=_params(("arbitrary", "arbitrary", "arbitrary")),
        name="gla",
    )(q, k, v, la, og, s0f, s0b, g_head)


def _fnet_kernel(u_ref, cc_ref, sc_ref, cs_ref, o_ref, ucs_ref, *, seq, scale):
    @pl.when(pl.program_id(1) == 0)
    def _():
        for gi in range(FN_GROUPS):
            sl = slice(gi * FN_CH, (gi + 1) * FN_CH)
            ug = u_ref[:, sl]
            ucs_ref[0:seq, sl] = _dot(ug, cc_ref[...]).astype(BF16)
            ucs_ref[seq:2 * seq, sl] = _dot(ug, sc_ref[...]).astype(BF16)

    o_ref[...] = (_dot(cs_ref[...], ucs_ref[...]) * scale).astype(BF16)


def _dft_tables(seq):
    def cs(n):
        idx = np.arange(n, dtype=np.int64)
        ang = 2.0 * np.pi * ((idx[:, None] * idx[None, :]) % n).astype(np.float64) / n
        return np.cos(ang), np.sin(ang)
    ct, st = cs(seq)
    cc, sc = cs(FN_CH)
    to = lambda a: jnp.asarray(a.astype(np.float32)).astype(BF16)
    return to(np.concatenate([ct, -st], axis=1)), to(cc), to(sc)


def _fnet(u, batch, seq):
    cs, cc, sc = _dft_tables(seq)
    rows = min(FN_ROWS, seq)
    nj = seq // rows
    return pl.pallas_call(
        functools.partial(_fnet_kernel, seq=seq, scale=float((seq * FN_CH) ** -0.5)),
        grid=(batch, nj),
        in_specs=[pl.BlockSpec((seq, FN_W), lambda b, j: (b, 0)),
                  _resident(cc.shape), _resident(sc.shape),
                  pl.BlockSpec((rows, 2 * seq), lambda b, j: (j, 0))],
        out_specs=pl.BlockSpec((rows, FN_W), lambda b, j: (b * nj + j, 0)),
        out_shape=jax.ShapeDtypeStruct((batch * seq, FN_W), BF16),
        scratch_shapes=[pltpu.VMEM((2 * seq, FN_W), BF16)],
        compiler_params=_params(("arbitrary", "arbitrary")),
        name="fnet",
    )(u, cc, sc, cs)


def _rope128(x, cos, sin_signed, odd):
    back = pltpu.roll(x, 16, 1)
    fwd = pltpu.roll(x, 112, 1)
    return x * cos + jnp.where(odd, back, fwd) * sin_signed


def _l1_in_kernel(*refs, rope):
    if rope:
        x_ref, m_ref, g_ref, w_ref, cos_ref, sin_ref, q_ref, k_ref, v_ref = refs
        odd = (lax.broadcasted_iota(jnp.int32, (SUB_PROJ, LANES), 1) // 16) % 2 == 1
    else:
        x_ref, m_ref, g_ref, w_ref, q_ref, k_ref, v_ref = refs

    def project(h, rows, dst, col0, width, scale):
        for j in range(width // 256):
            y = _dot(h, w_ref[:, col0 + 256 * j:col0 + 256 * (j + 1)])
            if scale != 1.0:
                y = y * scale
            for half in range(2):
                yh = y[:, LANES * half:LANES * (half + 1)]
                if rope:
                    yh = _rope128(yh, cos_ref[rows, :], sin_ref[rows, :], odd)
                lo = 256 * j + LANES * half
                dst[rows, lo:lo + LANES] = yh.astype(dst.dtype)

    for r0 in range(0, x_ref.shape[0], SUB_PROJ):
        rows = slice(r0, r0 + SUB_PROJ)
        h = _ada_norm(x_ref[rows, :], g_ref[...], m_ref[0, 3:4, :], m_ref[0, 4:5, :]).astype(BF16)
        project(h, rows, q_ref, 0, SWA_Q, SWA_HD ** -0.5 * LOG2E)
        project(h, rows, k_ref, SWA_Q, SWA_KV, 1.0)
        v_ref[rows, :] = _dot(h, w_ref[:, SWA_Q + SWA_KV:SWA_Q + 2 * SWA_KV])


def _l1_in_proj(x, mods, bfn, g, w, rope_tabs=None, tiles_per_seq=None):
    m, d = x.shape
    row = lambda i: (i, 0)
    ins = [x, mods, g, w]
    specs = [pl.BlockSpec((TM, d), row),
             pl.BlockSpec((1, MOD_ROWS, d), lambda i: (bfn(i), 0, 0)),
             _resident(g.shape), _resident(w.shape)]
    if rope_tabs is not None:
        ins += list(rope_tabs)
        specs += [pl.BlockSpec((TM, LANES), lambda i: (i % tiles_per_seq, 0))] * 2
    return pl.pallas_call(
        functools.partial(_l1_in_kernel, rope=rope_tabs is not None),
        grid=(m // TM,),
        in_specs=specs,
        out_specs=[pl.BlockSpec((TM, SWA_Q), row), pl.BlockSpec((TM, SWA_KV), row), pl.BlockSpec((TM, SWA_KV), row)],
        out_shape=[jax.ShapeDtypeStruct((m, SWA_Q), BF16),
                   jax.ShapeDtypeStruct((m, SWA_KV), F32),
                   jax.ShapeDtypeStruct((m, SWA_KV), F32)],
        compiler_params=_params(("arbitrary",)),
        name="l1_in_proj",
    )(*ins)


def _rope_tables(seq):
    rows = seq // GRID_W
    row = jnp.repeat(jnp.arange(rows), GRID_W).astype(F32)
    col = (jnp.arange(rows * GRID_W) % GRID_W).astype(F32)
    n_freq = SWA_HD // 4
    inv = ROPE_BASE ** (-jnp.arange(n_freq, dtype=F32) / n_freq)
    ar = row[:, None] * inv
    ac = col[:, None] * inv
    ang = jnp.concatenate([ar, ar, ac, ac], axis=-1)
    cos, sin = jnp.cos(ang), jnp.sin(ang)
    sign = jnp.where((jnp.arange(SWA_HD) // n_freq) % 2 == 1, 1.0, -1.0).astype(F32)
    two = lambda t: jnp.concatenate([t, t], axis=-1)
    return two(cos), two(sin * sign)


def _attend(q_ref, ks, vs, biases, sink_ref, o_ref):
    kcat = ks[0] if len(ks) == 1 else jnp.concatenate(ks, axis=0)
    vcat = vs[0] if len(vs) == 1 else jnp.concatenate(vs, axis=0)
    n = kcat.shape[0]
    r = q_ref.shape[0]
    nblk = n // LANES
    low = lax.broadcasted_iota(jnp.int32, (n, LANES), 1) < SWA_HD
    ones_lo = jnp.where(low, 1.0, 0.0)
    ones_hi = jnp.where(low, 0.0, 1.0)
    low_out = lax.broadcasted_iota(jnp.int32, (2 * r, LANES), 1) < SWA_HD
    top = lax.broadcasted_iota(jnp.int32, (2 * r, 1), 0) < r
    biases = {t: jnp.concatenate([b, b], axis=0) for t, b in biases.items()}
    nt_dims = (((1,), (1,)), ((), ()))
    for g in range(SWA_KV_HEADS):
        sl = slice(LANES * (g // 2), LANES * (g // 2 + 1))
        kcol, vcol = kcat[:, sl], vcat[:, sl]
        ksw, vsw = pltpu.roll(kcol, SWA_HD, 1), pltpu.roll(vcol, SWA_HD, 1)
        if g % 2 == 0:
            k_lo, k_hi = jnp.where(low, kcol, 0.0), jnp.where(low, 0.0, ksw)
            v_lo, v_hi = jnp.where(low, vcol, 0.0), jnp.where(low, 0.0, vsw)
        else:
            k_lo, k_hi = jnp.where(low, ksw, 0.0), jnp.where(low, 0.0, kcol)
            v_lo, v_hi = jnp.where(low, vsw, 0.0), jnp.where(low, 0.0, vcol)
        k_bd = jnp.concatenate([k_lo, k_hi], axis=0).astype(BF16)
        v_bd = jnp.concatenate([jnp.concatenate([v_lo, ones_lo], axis=1),
                                jnp.concatenate([v_hi, ones_hi], axis=1)], axis=0).astype(BF16)
        qs = jnp.concatenate([q_ref[:, LANES * (2 * g):LANES * (2 * g + 1)],
                              q_ref[:, LANES * (2 * g + 1):LANES * (2 * g + 2)]], axis=0)
        s = lax.dot_general(qs, k_bd, nt_dims, preferred_element_type=F32)
        probs, sink_terms = [], []
        for e in range(2):
            blocks = [s[:, e * n + LANES * t:e * n + LANES * (t + 1)] for t in range(nblk)]
            for t, b in biases.items():
                blocks[t] = blocks[t] + b
            mx = blocks[0]
            for b in blocks[1:]:
                mx = jnp.maximum(mx, b)
            sk = jnp.where(top, sink_ref[4 * g + e], sink_ref[4 * g + 2 + e]) * LOG2E
            m = jnp.maximum(jnp.max(mx, axis=-1, keepdims=True), sk)
            probs += [jnp.exp2(b - m).astype(BF16) for b in blocks]
            sink_terms.append(jnp.exp2(sk - m))
        res = _dot(jnp.concatenate(probs, axis=1), v_bd)
        denom = res[:, LANES:2 * LANES] + jnp.where(low_out, sink_terms[0], sink_terms[1])
        out = (res[:, 0:LANES] / denom).astype(o_ref.dtype)
        o_ref[:, LANES * (2 * g):LANES * (2 * g + 1)] = out[0:r]
        o_ref[:, LANES * (2 * g + 1):LANES * (2 * g + 2)] = out[r:2 * r]


def _ctx_attn_kernel(sink_ref, q_ref, k_ref, v_ref, o_ref):
    _attend(q_ref, [k_ref[...]], [v_ref[...]], {}, sink_ref, o_ref)


def _ctx_attention(q, k, v, sink, batch, seq):
    blk = lambda w: pl.BlockSpec((seq, w), lambda b: (b, 0))
    return pl.pallas_call(
        _ctx_attn_kernel,
        grid=(batch,),
        in_specs=[pl.BlockSpec(memory_space=pltpu.SMEM), blk(SWA_Q), blk(SWA_KV), blk(SWA_KV)],
        out_specs=blk(SWA_Q),
        out_shape=jax.ShapeDtypeStruct((batch * seq, SWA_Q), BF16),
        compiler_params=_params(("arbitrary",)),
        name="ctx_attention",
    )(sink, q, k, v)


def _lat_attn_kernel(sink_ref, q_ref, ck_ref, cv_ref, kp_ref, kc_ref, kn_ref, vp_ref, vc_ref, vn_ref, o_ref,
                     *, nb, n_ctx):
    i = pl.program_id(1)
    blk = ATTN_BLOCK
    row = lax.broadcasted_iota(jnp.int32, (blk, blk), 0)
    col = lax.broadcasted_iota(jnp.int32, (blk, blk), 1)
    off_prev = jnp.where(i > 0, 0, blk)
    off_next = jnp.where(i < nb - 1, 0, blk)
    bias_prev = jnp.where(col - row >= off_prev, 0.0, -jnp.inf)
    bias_next = jnp.where(row - col >= off_next, 0.0, -jnp.inf)
    first = n_ctx // LANES
    _attend(q_ref, [ck_ref[...], kp_ref[...], kc_ref[...], kn_ref[...]],
            [cv_ref[...], vp_ref[...], vc_ref[...], vn_ref[...]],
            {first: bias_prev, first + 2: bias_next}, sink_ref, o_ref)


def _lat_attention(q, k, v, ctx_k, ctx_v, sink, batch, seq, n_ctx):
    nb = seq // ATTN_BLOCK
    kv = lambda f: pl.BlockSpec((ATTN_BLOCK, SWA_KV), lambda b, i: (b * nb + f(i), 0))
    prev, cur, nxt = kv(lambda i: jnp.maximum(i - 1, 0)), kv(lambda i: i), kv(lambda i: jnp.minimum(i + 1, nb - 1))
    ctx = pl.BlockSpec((n_ctx, SWA_KV), lambda b, i: (b, 0))
    qo = pl.BlockSpec((ATTN_BLOCK, SWA_Q), lambda b, i: (b * nb + i, 0))
    return pl.pallas_call(
        functools.partial(_lat_attn_kernel, nb=nb, n_ctx=n_ctx),
        grid=(batch, nb),
        in_specs=[pl.BlockSpec(memory_space=pltpu.SMEM), qo, ctx, ctx, prev, cur, nxt, prev, cur, nxt],
        out_specs=qo,
        out_shape=jax.ShapeDtypeStruct((batch * seq, SWA_Q), BF16),
        compiler_params=_params(("arbitrary", "arbitrary")),
        name="lat_attention",
    )(sink, q, ctx_k, ctx_v, k, k, k, v, v, v)


def kernel(x_prompt, x_sample, state_l0_gla_fwd, state_l0_gla_bwd, cache_l1_k, cache_l1_v, c, c_ctx, mod_w, mod_b, norm_g, ffn_w1, ffn_w3, ffn_w2, l0_w_in, l0_w_gf, l0_b_gf, l0_w_gb, l0_b_gb, l0_g_head, l0_w_out, l1_w_in, l1_sink, l1_w_out, final_g):
    bp, tp, d = x_prompt.shape
    bs, ts, _ = x_sample.shape
    n_ctx = cache_l1_k.shape[1]
    depth = mod_w.shape[0]
    ctx_row = bs

    w1, w3, w2 = ffn_w1.astype(BF16), ffn_w3.astype(BF16), ffn_w2.astype(BF16)
    gate_lo = 2 * GLA_QK + 2 * GLA_V
    gate_hi = gate_lo + 2 * GLA_GATE_RANK
    l0_wm = jnp.concatenate([l0_w_in[:, :gate_lo], l0_w_in[:, gate_hi:]], axis=1).astype(BF16)
    l0_wg = jnp.pad(l0_w_in[:, gate_lo:gate_hi], ((0, 0), (0, LANES - 2 * GLA_GATE_RANK))).astype(BF16)
    wg2 = jnp.zeros((LANES, 2 * GLA_QK), F32)
    wg2 = wg2.at[:GLA_GATE_RANK, :GLA_QK].set(l0_w_gf)
    wg2 = wg2.at[GLA_GATE_RANK:2 * GLA_GATE_RANK, GLA_QK:].set(l0_w_gb).astype(BF16)
    bg = jnp.concatenate([l0_b_gf, l0_b_gb])[None, :]
    l0_wo, l1_wi, l1_wo = l0_w_out.astype(BF16), l1_w_in.astype(BF16), l1_w_out.astype(BF16)
    g_head = l0_g_head[None, :]
    sink = l1_sink.reshape(-1)
    fin_g = final_g[None, :]

    cond = jnp.zeros((MOD_ROWS, d), F32).at[:bs].set(c).at[ctx_row].set(c_ctx)
    mods = _modulation(cond, mod_w, mod_b)
    mods = mods.reshape(depth, MOD_ROWS, N_MOD, d)
    mods = jnp.pad(mods, ((0, 0), (0, 0), (0, MOD_ROWS - N_MOD), (0, 0)))

    xp = x_prompt.reshape(bp * tp, d)
    xs = x_sample.reshape(bs * ts, d)
    tiles_s = ts // TM
    bfn_p = lambda i: ctx_row
    bfn_s = lambda i: i // tiles_s
    zero_state = jnp.zeros((bp, GLA_QK, GLA_DV), F32)
    s0f = state_l0_gla_fwd.reshape(bs, GLA_QK, GLA_DV)
    s0b = state_l0_gla_bwd.reshape(bs, GLA_QK, GLA_DV)
    rope_tabs = _rope_tables(ts)

    new_state = []
    for layer in range(depth):
        ml = mods[layer]
        ga, gm, gb = (norm_g[layer, r][None, :] for r in range(3))
        fa = (w1, w3, w2, (layer, 0))
        fb = (w1, w3, w2, (layer, 1))
        last = layer == depth - 1
        xp = _half_ffn(xp, ml, bfn_p, ga, *fa, mrow=0)
        xs = _half_ffn(xs, ml, bfn_s, ga, *fa, mrow=0)
        if layer % 2 == 0:
            outs = []
            for x, bfn, nb, t, sf0, sb0 in ((xp, bfn_p, bp, tp, zero_state, zero_state), (xs, bfn_s, bs, ts, s0f, s0b)):
                q, k, v, og, la, u = _l0_in_proj(x, ml, bfn, gm, l0_wm, l0_wg, wg2, bg)
                seq3 = lambda a: a.reshape(nb, t, a.shape[-1])
                gla_out, s_f, s_b = _gla(seq3(q), seq3(k), seq3(v), seq3(la), seq3(og), sf0, sb0, g_head, nb, t)
                gla_out = gla_out.reshape(nb * t, GLA_V)
                fn_out = _fnet(u, nb, t)
                outs.append(((gla_out, fn_out), s_f, s_b))
            (mix_p, s_f, s_b), (mix_s, _, _) = outs
            new_state += [s_f.reshape(bp, GLA_HEADS, GLA_DK, GLA_DV), s_b.reshape(bp, GLA_HEADS, GLA_DK, GLA_DV)]
            w_out = l0_wo
        else:
            qp, kp, vp = _l1_in_proj(xp, ml, bfn_p, gm, l1_wi)
            mix_p = (_ctx_attention(qp, kp, vp, sink, bp, tp),)
            qs, ks, vs = _l1_in_proj(xs, ml, bfn_s, gm, l1_wi, rope_tabs, tiles_s)
            mix_s = (_lat_attention(qs, ks, vs, cache_l1_k.reshape(bs * n_ctx, SWA_KV),
                                    cache_l1_v.reshape(bs * n_ctx, SWA_KV), sink, bs, ts, n_ctx),)
            new_state += [kp.reshape(bp, tp, SWA_KV_HEADS, SWA_HD), vp.reshape(bp, tp, SWA_KV_HEADS, SWA_HD)]
            w_out = l1_wo
        fin = fin_g if last else None
        xp = _half_ffn(xp, ml, bfn_p, gb, *fb, mrow=6, mixes=mix_p, w_out=w_out, final_g=fin)
        xs = _half_ffn(xs, ml, bfn_s, gb, *fb, mrow=6, mixes=mix_s, w_out=w_out, final_g=fin)
    return (xp.reshape(bp, tp, d), xs.reshape(bs, ts, d), *new_state)
```

```python
import functools

import numpy as np
import jax
import jax.numpy as jnp
from jax import lax
from jax.experimental import pallas as pl
from jax.experimental.pallas import tpu as pltpu

F32 = jnp.float32
BF16 = jnp.bfloat16

D_MODEL = 1024
FFN_DIM = 2816
N_MOD = 9
EPS = 1e-6
GRID_W = 64
GLA_HEADS = 4
GLA_DK = 64
GLA_DV = 128
GLA_GATE_RANK = 16
GLA_TAU = 16.0
GLA_CHUNK = 64
GLA_QK = GLA_HEADS * GLA_DK
GLA_V = GLA_HEADS * GLA_DV
FN_GROUPS = 4
FN_CH = 128
FN_W = FN_GROUPS * FN_CH
SWA_HEADS = 16
SWA_KV_HEADS = 4
SWA_HD = 64
SWA_Q = SWA_HEADS * SWA_HD
SWA_KV = SWA_KV_HEADS * SWA_HD
WINDOW = 128
ATTN_BLOCK = 128
ROPE_BASE = 10000.0
LOG2E = float(np.log2(np.e))

LANES = 128
VMEM_LIMIT = 56 * 1024 * 1024

TM = 1024
SUB = 512
SUB_PROJ = 256
TF = 256
GLA_TILE = 256
GLA_GROUP = 4
FN_ROWS = 256
MOD_ROWS = 16
MOD_TN = 1024


def _params(sem):
    return pltpu.CompilerParams(dimension_semantics=sem, vmem_limit_bytes=VMEM_LIMIT)


def _resident(shape):
    nd = len(shape)
    return pl.BlockSpec(shape, lambda *_: (0,) * nd, pipeline_mode=pl.Buffered(1))


def _dot(a, b):
    return jnp.dot(a, b, preferred_element_type=F32)


def _silu(x):
    return x * jax.nn.sigmoid(x)


def _ada_norm(x, g, shift, scale):
    ms = jnp.mean(x * x, axis=-1, keepdims=True)
    return (x * lax.rsqrt(ms + EPS)) * (g * (1.0 + scale)) + shift


def _mod_kernel(c_ref, w_ref, b_ref, o_ref):
    c = c_ref[...]
    s = _silu(c).astype(BF16)
    o_ref[0] = _dot(s, w_ref[0].astype(BF16)) + b_ref[0]


def _modulation(cond, mod_w, mod_b):
    depth, d, n = mod_w.shape
    return pl.pallas_call(
        _mod_kernel,
        grid=(depth, n // MOD_TN),
        in_specs=[
            pl.BlockSpec((MOD_ROWS, d), lambda l, j: (0, 0)),
            pl.BlockSpec((1, d, MOD_TN), lambda l, j: (l, 0, j)),
            pl.BlockSpec((1, 1, MOD_TN), lambda l, j: (l, 0, j)),
        ],
        out_specs=pl.BlockSpec((1, MOD_ROWS, MOD_TN), lambda l, j: (l, 0, j)),
        out_shape=jax.ShapeDtypeStruct((depth, MOD_ROWS, n), F32),
        compiler_params=_params(("arbitrary", "arbitrary")),
        name="modulation",
    )(cond, mod_w, mod_b.reshape(depth, 1, n))


def _ffn_kernel(*refs, n_mix, mrow, final):
    x_ref = refs[0]
    mix_refs = refs[1:1 + n_mix]
    pos = 1 + n_mix
    if n_mix:
        wo_ref = refs[pos]
        pos += 1
    m_ref, g_ref, w1_ref, w3_ref, w2_ref = refs[pos:pos + 5]
    pos += 5
    if final:
        fg_ref = refs[pos]
        pos += 1
    o_ref, acc_ref, h_ref = refs[pos], refs[pos + 1], refs[pos + 2]

    def prologue(rows):
        x = x_ref[rows, :]
        if n_mix:
            off = 0
            mix = None
            for r in mix_refs:
                w = r.shape[1]
                t = _dot(r[rows, :], wo_ref[off:off + w, :])
                mix = t if mix is None else mix + t
                off += w
            x = x + m_ref[0, 5:6, :] * mix
        o_ref[rows, :] = x
        h = _ada_norm(x, g_ref[...], m_ref[0, mrow:mrow + 1, :], m_ref[0, mrow + 1:mrow + 2, :])
        h_ref[rows, :] = h.astype(BF16)

    def chunk(rows, j):
        h = h_ref[rows, :]
        a = _dot(h, w1_ref[:, j * TF:(j + 1) * TF])
        b = _dot(h, w3_ref[:, j * TF:(j + 1) * TF])
        t = _dot((_silu(a) * b).astype(BF16), w2_ref[j * TF:(j + 1) * TF, :])
        if j == 0:
            acc_ref[rows, :] = t
        else:
            acc_ref[rows, :] += t

    def epilogue(rows):
        y = o_ref[rows, :] + (0.5 * m_ref[0, mrow + 2:mrow + 3, :]) * acc_ref[rows, :]
        if final:
            ms = jnp.mean(y * y, axis=-1, keepdims=True)
            y = y * lax.rsqrt(ms + EPS) * fg_ref[...]
        o_ref[rows, :] = y

    subs = [slice(r0, r0 + SUB) for r0 in range(0, x_ref.shape[0], SUB)]
    for rows in subs:
        prologue(rows)
    for j in range(FFN_DIM // TF):
        for rows in subs:
            chunk(rows, j)
    for rows in subs:
        epilogue(rows)


def _pick(arr, lead):
    tail = arr.shape[len(lead):]
    idx = tuple(lead) + (0,) * len(tail)
    return pl.BlockSpec((None,) * len(lead) + tail, lambda *_: idx, pipeline_mode=pl.Buffered(1))


def _half_ffn(x, mods, bfn, g, w1, w3, w2, wsel, *, mrow, mixes=(), w_out=None, final_g=None):
    m, d = x.shape
    row = lambda i: (i, 0)
    ins = [x]
    specs = [pl.BlockSpec((TM, d), row)]
    for a in mixes:
        ins.append(a)
        specs.append(pl.BlockSpec((TM, a.shape[1]), row))
    if mixes:
        ins.append(w_out)
        specs.append(_resident(w_out.shape))
    ins += [mods, g, w1, w3, w2]
    specs += [pl.BlockSpec((1, MOD_ROWS, d), lambda i: (bfn(i), 0, 0)),
              _resident(g.shape), _pick(w1, wsel), _pick(w3, wsel), _pick(w2, wsel)]
    if final_g is not None:
        ins.append(final_g)
        specs.append(_resident(final_g.shape))
    return pl.pallas_call(
        functools.partial(_ffn_kernel, n_mix=len(mixes), mrow=mrow, final=final_g is not None),
        grid=(m // TM,),
        in_specs=specs,
        out_specs=pl.BlockSpec((TM, d), row),
        out_shape=jax.ShapeDtypeStruct((m, d), F32),
        scratch_shapes=[pltpu.VMEM((TM, d), F32), pltpu.VMEM((TM, d), BF16)],
        compiler_params=_params(("arbitrary",)),
        name="half_ffn",
    )(*ins)


def _l0_in_kernel(x_ref, m_ref, g_ref, wm_ref, wg_ref, wg2_ref, bg_ref,
                  q_ref, k_ref, v_ref, og_ref, la_ref, u_ref):
    for r0 in range(0, x_ref.shape[0], SUB_PROJ):
        rows = slice(r0, r0 + SUB_PROJ)
        h = _ada_norm(x_ref[rows, :], g_ref[...], m_ref[0, 3:4, :], m_ref[0, 4:5, :]).astype(BF16)
        q_ref[rows, :] = _dot(h, wm_ref[:, 0:256]) * (GLA_DK ** -0.5)
        k_ref[rows, :] = _dot(h, wm_ref[:, 256:512])
        v_ref[rows, :] = _dot(h, wm_ref[:, 512:1024]).astype(BF16)
        og_ref[rows, :] = _dot(h, wm_ref[:, 1024:1536])
        u_ref[rows, :] = _dot(h, wm_ref[:, 1536:2048]).astype(BF16)
        lr = _dot(h, wg_ref[...]).astype(BF16)
        z = _dot(lr, wg2_ref[...]) + bg_ref[...]
        log_sig = jnp.minimum(z, 0.0) - jnp.log(1.0 + jnp.exp(-jnp.abs(z)))
        la_ref[rows, :] = log_sig * (1.0 / GLA_TAU)


def _l0_in_proj(x, mods, bfn, g, wm, wg, wg2, bg):
    m, d = x.shape
    row = lambda i: (i, 0)
    outs = [(GLA_QK, F32), (GLA_QK, F32), (GLA_V, BF16), (GLA_V, F32), (2 * GLA_QK, F32), (FN_W, BF16)]
    return pl.pallas_call(
        _l0_in_kernel,
        grid=(m // TM,),
        in_specs=[pl.BlockSpec((TM, d), row),
                  pl.BlockSpec((1, MOD_ROWS, d), lambda i: (bfn(i), 0, 0)),
                  _resident(g.shape), _resident(wm.shape), _resident(wg.shape),
                  _resident(wg2.shape), _resident(bg.shape)],
        out_specs=[pl.BlockSpec((TM, w), row) for w, _ in outs],
        out_shape=[jax.ShapeDtypeStruct((m, w), dt) for w, dt in outs],
        compiler_params=_params(("arbitrary",)),
        name="l0_in_proj",
    )(x, mods, g, wm, wg, wg2, bg)


def _gla_tiles(qs, ks, vs, gs, ss, reverse):
    tt = GLA_TILE
    nch = tt // GLA_CHUNK
    seqs = range(len(qs))
    heads = range(GLA_HEADS)
    r = lax.broadcasted_iota(jnp.int32, (tt, tt), 0)
    c = lax.broadcasted_iota(jnp.int32, (tt, tt), 1)
    same = (r // GLA_CHUNK) == (c // GLA_CHUNK)
    m_intra = jnp.logical_and(same, (c >= r) if reverse else (c <= r))
    tri = jnp.where(m_intra, 1.0, 0.0).astype(BF16)
    cc = lax.broadcasted_iota(jnp.int32, (GLA_QK, tt), 1) // GLA_CHUNK
    lane = lax.broadcasted_iota(jnp.int32, (tt, GLA_QK), 1) // GLA_DK
    end_col = [ci * GLA_CHUNK + (0 if reverse else GLA_CHUNK - 1) for ci in range(nch)]
    zero_v = jnp.zeros((GLA_CHUNK, GLA_DV), BF16)
    zero_s = jnp.zeros((GLA_DK, GLA_DV), BF16)

    g_hi = [g.astype(BF16) for g in gs]
    g_lo = [(gs[i] - g_hi[i].astype(F32)).astype(BF16) for i in seqs]
    b = [_dot(tri, g_hi[i]) + _dot(tri, g_lo[i]) for i in seqs]
    qd = [qs[i] * jnp.exp(b[i]) for i in seqs]
    b_t = [x.T for x in b]
    k_t = [x.T for x in ks]
    ends = [[b_t[i][:, e:e + 1] for e in end_col] for i in seqs]
    ki_t, ke_t = [], []
    for i in seqs:
        bl_t = ends[i][nch - 1]
        for ci in range(nch - 2, -1, -1):
            bl_t = jnp.where(cc == ci, ends[i][ci], bl_t)
        ki_t.append((k_t[i] * jnp.exp(-b_t[i])).astype(BF16))
        ke_t.append(k_t[i] * jnp.exp(bl_t - b_t[i]))

    o_intra = [[None] * GLA_HEADS for _ in seqs]
    for h in heads:
        a = [_dot(jnp.where(lane == h, qd[i], 0.0).astype(BF16), ki_t[i]) for i in seqs]
        for i in seqs:
            am = jnp.where(m_intra, a[i], 0.0).astype(BF16)
            o_intra[i][h] = _dot(am, vs[i][:, h * GLA_DV:(h + 1) * GLA_DV])

    deltas = [[None] * GLA_HEADS for _ in seqs]
    for h in heads:
        for i in seqs:
            v_h = vs[i][:, h * GLA_DV:(h + 1) * GLA_DV]
            v_bd = jnp.concatenate(
                [jnp.concatenate([v_h[ci * GLA_CHUNK:(ci + 1) * GLA_CHUNK] if cj == ci else zero_v
                                  for cj in range(nch)], axis=1) for ci in range(nch)], axis=0)
            deltas[i][h] = _dot(ke_t[i][h * GLA_DK:(h + 1) * GLA_DK, :].astype(BF16), v_bd)

    s_h = [[ss[i][h * GLA_DK:(h + 1) * GLA_DK, :] for h in heads] for i in seqs]
    o_inter = [[None] * nch for _ in seqs]
    for ci in (range(nch - 1, -1, -1) if reverse else range(nch)):
        for i in seqs:
            s_bd = jnp.concatenate(
                [jnp.concatenate([s_h[i][h].astype(BF16) if hj == h else zero_s for hj in heads], axis=1)
                 for h in heads], axis=0)
            o_inter[i][ci] = _dot(qd[i][ci * GLA_CHUNK:(ci + 1) * GLA_CHUNK, :].astype(BF16), s_bd)
            decay = jnp.exp(ends[i][ci])
            s_h[i] = [decay[h * GLA_DK:(h + 1) * GLA_DK] * s_h[i][h]
                      + deltas[i][h][:, ci * GLA_DV:(ci + 1) * GLA_DV] for h in heads]
    outs = [jnp.concatenate(o_intra[i], axis=1) + jnp.concatenate(o_inter[i], axis=0) for i in seqs]
    return outs, [jnp.concatenate(s_h[i], axis=0) for i in seqs]


def _gla_kernel(q_ref, k_ref, v_ref, la_ref, og_ref, s0f_ref, s0b_ref, gh_ref,
                o_ref, sf_ref, sb_ref, s_ref, ob_ref, *, nt):
    p = pl.program_id(1)
    i = pl.program_id(2)
    grp = range(GLA_GROUP)

    @pl.when(p == 0)
    def _backward():
        @pl.when(i == 0)
        def _():
            s_ref[...] = s0b_ref[...]

        start = pl.multiple_of((nt - 1 - i) * GLA_TILE, GLA_TILE)
        outs, states = _gla_tiles(*[[ref[gi] for gi in grp] for ref in (q_ref, k_ref, v_ref, la_ref, s_ref)], True)
        for gi in grp:
            s_ref[gi] = states[gi]
            ob_ref[gi, pl.ds(start, GLA_TILE), :] = outs[gi]

        @pl.when(i == nt - 1)
        def _():
            sb_ref[...] = s_ref[...]

    @pl.when(p == 1)
    def _forward():
        @pl.when(i == 0)
        def _():
            s_ref[...] = s0f_ref[...]

        start = pl.multiple_of(i * GLA_TILE, GLA_TILE)
        outs, states = _gla_tiles(*[[ref[gi] for gi in grp] for ref in (q_ref, k_ref, v_ref, la_ref, s_ref)], False)
        for gi in grp:
            s_ref[gi] = states[gi]
            o = outs[gi] + ob_ref[gi, pl.ds(start, GLA_TILE), :]
            for h in range(GLA_HEADS):
                sl = slice(h * GLA_DV, (h + 1) * GLA_DV)
                oh = o[:, sl]
                ms = jnp.mean(oh * oh, axis=-1, keepdims=True)
                y = oh * lax.rsqrt(ms + EPS) * gh_ref[...]
                o_ref[gi, :, sl] = (y * _silu(og_ref[gi, :, sl])).astype(BF16)

        @pl.when(i == nt - 1)
        def _():
            sf_ref[...] = s_ref[...]


def _gla(q, k, v, la, og, s0f, s0b, g_head, batch, seq):
    nt = seq // GLA_TILE
    gg = GLA_GROUP
    tile = lambda b, p, i: (b, p * i + (1 - p) * (nt - 1 - i), 0)
    fwd_tile = lambda b, p, i: (b, p * i, 0)
    st = pl.BlockSpec((gg, GLA_QK, GLA_DV), lambda b, p, i: (b, 0, 0))
    return pl.pallas_call(
        functools.partial(_gla_kernel, nt=nt),
        grid=(batch // gg, 2, nt),
        in_specs=[pl.BlockSpec((gg, GLA_TILE, GLA_QK), tile),
                  pl.BlockSpec((gg, GLA_TILE, GLA_QK), tile),
                  pl.BlockSpec((gg, GLA_TILE, GLA_V), tile),
                  pl.BlockSpec((gg, GLA_TILE, GLA_QK), lambda b, p, i: tile(b, p, i)[:2] + (1 - p,)),
                  pl.BlockSpec((gg, GLA_TILE, GLA_V), fwd_tile),
                  st, st, _resident(g_head.shape)],
        out_specs=[pl.BlockSpec((gg, GLA_TILE, GLA_V), fwd_tile), st, st],
        out_shape=[jax.ShapeDtypeStruct((batch, seq, GLA_V), BF16),
                   jax.ShapeDtypeStruct((batch, GLA_QK, GLA_DV), F32),
                   jax.ShapeDtypeStruct((batch, GLA_QK, GLA_DV), F32)],
        scratch_shapes=[pltpu.VMEM((gg, GLA_QK, GLA_DV), F32), pltpu.VMEM((gg, seq, GLA_V), F32)],
        compiler_params=_params(("arbitrary", "arbitrary", "arbitrary")),
        name="gla",
    )(q, k, v, la, og, s0f, s0b, g_head)


def _fnet_kernel(u_ref, cc_ref, sc_ref, cs_ref, o_ref, ucs_ref, *, seq, scale):
    @pl.when(pl.program_id(1) == 0)
    def _():
        for gi in range(FN_GROUPS):
            sl = slice(gi * FN_CH, (gi + 1) * FN_CH)
            ug = u_ref[:, sl]
            ucs_ref[0:seq, sl] = _dot(ug, cc_ref[...]).astype(BF16)
            ucs_ref[seq:2 * seq, sl] = _dot(ug, sc_ref[...]).astype(BF16)

    o_ref[...] = (_dot(cs_ref[...], ucs_ref[...]) * scale).astype(BF16)


def _dft_tables(seq):
    def cs(n):
        idx = np.arange(n, dtype=np.int64)
        ang = 2.0 * np.pi * ((idx[:, None] * idx[None, :]) % n).astype(np.float64) / n
        return np.cos(ang), np.sin(ang)
    ct, st = cs(seq)
    cc, sc = cs(FN_CH)
    to = lambda a: jnp.asarray(a.astype(np.float32)).astype(BF16)
    return to(np.concatenate([ct, -st], axis=1)), to(cc), to(sc)


def _fnet(u, batch, seq):
    cs, cc, sc = _dft_tables(seq)
    rows = min(FN_ROWS, seq)
    nj = seq // rows
    return pl.pallas_call(
        functools.partial(_fnet_kernel, seq=seq, scale=float((seq * FN_CH) ** -0.5)),
        grid=(batch, nj),
        in_specs=[pl.BlockSpec((seq, FN_W), lambda b, j: (b, 0)),
                  _resident(cc.shape), _resident(sc.shape),
                  pl.BlockSpec((rows, 2 * seq), lambda b, j: (j, 0))],
        out_specs=pl.BlockSpec((rows, FN_W), lambda b, j: (b * nj + j, 0)),
        out_shape=jax.ShapeDtypeStruct((batch * seq, FN_W), BF16),
        scratch_shapes=[pltpu.VMEM((2 * seq, FN_W), BF16)],
        compiler_params=_params(("arbitrary", "arbitrary")),
        name="fnet",
    )(u, cc, sc, cs)


def _rope128(x, cos, sin_signed, odd):
    back = pltpu.roll(x, 16, 1)
    fwd = pltpu.roll(x, 112, 1)
    return x * cos + jnp.where(odd, back, fwd) * sin_signed


def _l1_in_kernel(*refs, rope):
    if rope:
        x_ref, m_ref, g_ref, w_ref, cos_ref, sin_ref, q_ref, k_ref, v_ref = refs
        odd = (lax.broadcasted_iota(jnp.int32, (SUB_PROJ, LANES), 1) // 16) % 2 == 1
    else:
        x_ref, m_ref, g_ref, w_ref, q_ref, k_ref, v_ref = refs

    def project(h, rows, dst, col0, width, scale):
        for j in range(width // 256):
            y = _dot(h, w_ref[:, col0 + 256 * j:col0 + 256 * (j + 1)])
            if scale != 1.0:
                y = y * scale
            for half in range(2):
                yh = y[:, LANES * half:LANES * (half + 1)]
                if rope:
                    yh = _rope128(yh, cos_ref[rows, :], sin_ref[rows, :], odd)
                lo = 256 * j + LANES * half
                dst[rows, lo:lo + LANES] = yh.astype(dst.dtype)

    for r0 in range(0, x_ref.shape[0], SUB_PROJ):
        rows = slice(r0, r0 + SUB_PROJ)
        h = _ada_norm(x_ref[rows, :], g_ref[...], m_ref[0, 3:4, :], m_ref[0, 4:5, :]).astype(BF16)
        project(h, rows, q_ref, 0, SWA_Q, SWA_HD ** -0.5 * LOG2E)
        project(h, rows, k_ref, SWA_Q, SWA_KV, 1.0)
        v_ref[rows, :] = _dot(h, w_ref[:, SWA_Q + SWA_KV:SWA_Q + 2 * SWA_KV])


def _l1_in_proj(x, mods, bfn, g, w, rope_tabs=None, tiles_per_seq=None):
    m, d = x.shape
    row = lambda i: (i, 0)
    ins = [x, mods, g, w]
    specs = [pl.BlockSpec((TM, d), row),
             pl.BlockSpec((1, MOD_ROWS, d), lambda i: (bfn(i), 0, 0)),
             _resident(g.shape), _resident(w.shape)]
    if rope_tabs is not None:
        ins += list(rope_tabs)
        specs += [pl.BlockSpec((TM, LANES), lambda i: (i % tiles_per_seq, 0))] * 2
    return pl.pallas_call(
        functools.partial(_l1_in_kernel, rope=rope_tabs is not None),
        grid=(m // TM,),
        in_specs=specs,
        out_specs=[pl.BlockSpec((TM, SWA_Q), row), pl.BlockSpec((TM, SWA_KV), row), pl.BlockSpec((TM, SWA_KV), row)],
        out_shape=[jax.ShapeDtypeStruct((m, SWA_Q), BF16),
                   jax.ShapeDtypeStruct((m, SWA_KV), F32),
                   jax.ShapeDtypeStruct((m, SWA_KV), F32)],
        compiler_params=_params(("arbitrary",)),
        name="l1_in_proj",
    )(*ins)


def _rope_tables(seq):
    rows = seq // GRID_W
    row = jnp.repeat(jnp.arange(rows), GRID_W).astype(F32)
    col = (jnp.arange(rows * GRID_W) % GRID_W).astype(F32)
    n_freq = SWA_HD // 4
    inv = ROPE_BASE ** (-jnp.arange(n_freq, dtype=F32) / n_freq)
    ar = row[:, None] * inv
    ac = col[:, None] * inv
    ang = jnp.concatenate([ar, ar, ac, ac], axis=-1)
    cos, sin = jnp.cos(ang), jnp.sin(ang)
    sign = jnp.where((jnp.arange(SWA_HD) // n_freq) % 2 == 1, 1.0, -1.0).astype(F32)
    two = lambda t: jnp.concatenate([t, t], axis=-1)
    return two(cos), two(sin * sign)


def _attend(q_ref, ks, vs, biases, sink_ref, o_ref):
    kcat = ks[0] if len(ks) == 1 else jnp.concatenate(ks, axis=0)
    vcat = vs[0] if len(vs) == 1 else jnp.concatenate(vs, axis=0)
    n = kcat.shape[0]
    r = q_ref.shape[0]
    nblk = n // LANES
    low = lax.broadcasted_iota(jnp.int32, (n, LANES), 1) < SWA_HD
    ones_lo = jnp.where(low, 1.0, 0.0)
    ones_hi = jnp.where(low, 0.0, 1.0)
    low_out = lax.broadcasted_iota(jnp.int32, (2 * r, LANES), 1) < SWA_HD
    top = lax.broadcasted_iota(jnp.int32, (2 * r, 1), 0) < r
    biases = {t: jnp.concatenate([b, b], axis=0) for t, b in biases.items()}
    nt_dims = (((1,), (1,)), ((), ()))

    def scores(g):
        sl = slice(LANES * (g // 2), LANES * (g // 2 + 1))
        kcol, vcol = kcat[:, sl], vcat[:, sl]
        ksw, vsw = pltpu.roll(kcol, SWA_HD, 1), pltpu.roll(vcol, SWA_HD, 1)
        if g % 2 == 0:
            k_lo, k_hi = jnp.where(low, kcol, 0.0), jnp.where(low, 0.0, ksw)
            v_lo, v_hi = jnp.where(low, vcol, 0.0), jnp.where(low, 0.0, vsw)
        else:
            k_lo, k_hi = jnp.where(low, ksw, 0.0), jnp.where(low, 0.0, kcol)
            v_lo, v_hi = jnp.where(low, vsw, 0.0), jnp.where(low, 0.0, vcol)
        k_bd = jnp.concatenate([k_lo, k_hi], axis=0).astype(BF16)
        v_bd = jnp.concatenate([jnp.concatenate([v_lo, ones_lo], axis=1),
                                jnp.concatenate([v_hi, ones_hi], axis=1)], axis=0).astype(BF16)
        qs = jnp.concatenate([q_ref[:, LANES * (2 * g):LANES * (2 * g + 1)],
                              q_ref[:, LANES * (2 * g + 1):LANES * (2 * g + 2)]], axis=0)
        return lax.dot_general(qs, k_bd, nt_dims, preferred_element_type=F32), v_bd

    nxt = scores(0)
    for g in range(SWA_KV_HEADS):
        s, v_bd = nxt
        if g + 1 < SWA_KV_HEADS:
            nxt = scores(g + 1)
        probs, sink_terms = [], []
        for e in range(2):
            blocks = [s[:, e * n + LANES * t:e * n + LANES * (t + 1)] for t in range(nblk)]
            for t, b in biases.items():
                blocks[t] = blocks[t] + b
            mx = blocks[0]
            for b in blocks[1:]:
                mx = jnp.maximum(mx, b)
            sk = jnp.where(top, sink_ref[4 * g + e], sink_ref[4 * g + 2 + e]) * LOG2E
            m = jnp.maximum(jnp.max(mx, axis=-1, keepdims=True), sk)
            probs += [jnp.exp2(b - m).astype(BF16) for b in blocks]
            sink_terms.append(jnp.exp2(sk - m))
        res = _dot(jnp.concatenate(probs, axis=1), v_bd)
        denom = res[:, LANES:2 * LANES] + jnp.where(low_out, sink_terms[0], sink_terms[1])
        out = (res[:, 0:LANES] / denom).astype(o_ref.dtype)
        o_ref[:, LANES * (2 * g):LANES * (2 * g + 1)] = out[0:r]
        o_ref[:, LANES * (2 * g + 1):LANES * (2 * g + 2)] = out[r:2 * r]


def _ctx_attn_kernel(sink_ref, q_ref, k_ref, v_ref, o_ref):
    _attend(q_ref, [k_ref[...]], [v_ref[...]], {}, sink_ref, o_ref)


def _ctx_attention(q, k, v, sink, batch, seq):
    blk = lambda w: pl.BlockSpec((seq, w), lambda b: (b, 0))
    return pl.pallas_call(
        _ctx_attn_kernel,
        grid=(batch,),
        in_specs=[pl.BlockSpec(memory_space=pltpu.SMEM), blk(SWA_Q), blk(SWA_KV), blk(SWA_KV)],
        out_specs=blk(SWA_Q),
        out_shape=jax.ShapeDtypeStruct((batch * seq, SWA_Q), BF16),
        compiler_params=_params(("arbitrary",)),
        name="ctx_attention",
    )(sink, q, k, v)


def _lat_attn_kernel(sink_ref, q_ref, ck_ref, cv_ref, kp_ref, kc_ref, kn_ref, vp_ref, vc_ref, vn_ref, o_ref,
                     *, nb, n_ctx):
    i = pl.program_id(1)
    blk = ATTN_BLOCK
    row = lax.broadcasted_iota(jnp.int32, (blk, blk), 0)
    col = lax.broadcasted_iota(jnp.int32, (blk, blk), 1)
    off_prev = jnp.where(i > 0, 0, blk)
    off_next = jnp.where(i < nb - 1, 0, blk)
    bias_prev = jnp.where(col - row >= off_prev, 0.0, -jnp.inf)
    bias_next = jnp.where(row - col >= off_next, 0.0, -jnp.inf)
    first = n_ctx // LANES
    _attend(q_ref, [ck_ref[...], kp_ref[...], kc_ref[...], kn_ref[...]],
            [cv_ref[...], vp_ref[...], vc_ref[...], vn_ref[...]],
            {first: bias_prev, first + 2: bias_next}, sink_ref, o_ref)


def _lat_attention(q, k, v, ctx_k, ctx_v, sink, batch, seq, n_ctx):
    nb = seq // ATTN_BLOCK
    kv = lambda f: pl.BlockSpec((ATTN_BLOCK, SWA_KV), lambda b, i: (b * nb + f(i), 0))
    prev, cur, nxt = kv(lambda i: jnp.maximum(i - 1, 0)), kv(lambda i: i), kv(lambda i: jnp.minimum(i + 1, nb - 1))
    ctx = pl.BlockSpec((n_ctx, SWA_KV), lambda b, i: (b, 0))
    qo = pl.BlockSpec((ATTN_BLOCK, SWA_Q), lambda b, i: (b * nb + i, 0))
    return pl.pallas_call(
        functools.partial(_lat_attn_kernel, nb=nb, n_ctx=n_ctx),
        grid=(batch, nb),
        in_specs=[pl.BlockSpec(memory_space=pltpu.SMEM), qo, ctx, ctx, prev, cur, nxt, prev, cur, nxt],
        out_specs=qo,
        out_shape=jax.ShapeDtypeStruct((batch * seq, SWA_Q), BF16),
        compiler_params=_params(("arbitrary", "arbitrary")),
        name="lat_attention",
    )(sink, q, ctx_k, ctx_v, k, k, k, v, v, v)


def kernel(x_prompt, x_sample, state_l0_gla_fwd, state_l0_gla_bwd, cache_l1_k, cache_l1_v, c, c_ctx, mod_w, mod_b, norm_g, ffn_w1, ffn_w3, ffn_w2, l0_w_in, l0_w_gf, l0_b_gf, l0_w_gb, l0_b_gb, l0_g_head, l0_w_out, l1_w_in, l1_sink, l1_w_out, final_g):
    bp, tp, d = x_prompt.shape
    bs, ts, _ = x_sample.shape
    n_ctx = cache_l1_k.shape[1]
    depth = mod_w.shape[0]
    ctx_row = bs

    w1, w3, w2 = ffn_w1.astype(BF16), ffn_w3.astype(BF16), ffn_w2.astype(BF16)
    gate_lo = 2 * GLA_QK + 2 * GLA_V
    gate_hi = gate_lo + 2 * GLA_GATE_RANK
    l0_wm = jnp.concatenate([l0_w_in[:, :gate_lo], l0_w_in[:, gate_hi:]], axis=1).astype(BF16)
    l0_wg = jnp.pad(l0_w_in[:, gate_lo:gate_hi], ((0, 0), (0, LANES - 2 * GLA_GATE_RANK))).astype(BF16)
    wg2 = jnp.zeros((LANES, 2 * GLA_QK), F32)
    wg2 = wg2.at[:GLA_GATE_RANK, :GLA_QK].set(l0_w_gf)
    wg2 = wg2.at[GLA_GATE_RANK:2 * GLA_GATE_RANK, GLA_QK:].set(l0_w_gb).astype(BF16)
    bg = jnp.concatenate([l0_b_gf, l0_b_gb])[None, :]
    l0_wo, l1_wi, l1_wo = l0_w_out.astype(BF16), l1_w_in.astype(BF16), l1_w_out.astype(BF16)
    g_head = l0_g_head[None, :]
    sink = l1_sink.reshape(-1)
    fin_g = final_g[None, :]

    cond = jnp.zeros((MOD_ROWS, d), F32).at[:bs].set(c).at[ctx_row].set(c_ctx)
    mods = _modulation(cond, mod_w, mod_b)
    mods = mods.reshape(depth, MOD_ROWS, N_MOD, d)
    mods = jnp.pad(mods, ((0, 0), (0, 0), (0, MOD_ROWS - N_MOD), (0, 0)))

    xp = x_prompt.reshape(bp * tp, d)
    xs = x_sample.reshape(bs * ts, d)
    tiles_s = ts // TM
    bfn_p = lambda i: ctx_row
    bfn_s = lambda i: i // tiles_s
    zero_state = jnp.zeros((bp, GLA_QK, GLA_DV), F32)
    s0f = state_l0_gla_fwd.reshape(bs, GLA_QK, GLA_DV)
    s0b = state_l0_gla_bwd.reshape(bs, GLA_QK, GLA_DV)
    rope_tabs = _rope_tables(ts)

    new_state = []
    for layer in range(depth):
        ml = mods[layer]
        ga, gm, gb = (norm_g[layer, r][None, :] for r in range(3))
        fa = (w1, w3, w2, (layer, 0))
        fb = (w1, w3, w2, (layer, 1))
        last = layer == depth - 1
        xp = _half_ffn(xp, ml, bfn_p, ga, *fa, mrow=0)
        xs = _half_ffn(xs, ml, bfn_s, ga, *fa, mrow=0)
        if layer % 2 == 0:
            outs = []
            for x, bfn, nb, t, sf0, sb0 in ((xp, bfn_p, bp, tp, zero_state, zero_state), (xs, bfn_s, bs, ts, s0f, s0b)):
                q, k, v, og, la, u = _l0_in_proj(x, ml, bfn, gm, l0_wm, l0_wg, wg2, bg)
                seq3 = lambda a: a.reshape(nb, t, a.shape[-1])
                gla_out, s_f, s_b = _gla(seq3(q), seq3(k), seq3(v), seq3(la), seq3(og), sf0, sb0, g_head, nb, t)
                gla_out = gla_out.reshape(nb * t, GLA_V)
                fn_out = _fnet(u, nb, t)
                outs.append(((gla_out, fn_out), s_f, s_b))
            (mix_p, s_f, s_b), (mix_s, _, _) = outs
            new_state += [s_f.reshape(bp, GLA_HEADS, GLA_DK, GLA_DV), s_b.reshape(bp, GLA_HEADS, GLA_DK, GLA_DV)]
            w_out = l0_wo
        else:
            qp, kp, vp = _l1_in_proj(xp, ml, bfn_p, gm, l1_wi)
            mix_p = (_ctx_attention(qp, kp, vp, sink, bp, tp),)
            qs, ks, vs = _l1_in_proj(xs, ml, bfn_s, gm, l1_wi, rope_tabs, tiles_s)
            mix_s = (_lat_attention(qs, ks, vs, cache_l1_k.reshape(bs * n_ctx, SWA_KV),
                                    cache_l1_v.reshape(bs * n_ctx, SWA_KV), sink, bs, ts, n_ctx),)
            new_state += [kp.reshape(bp, tp, SWA_KV_HEADS, SWA_HD), vp.reshape(bp, tp, SWA_KV_HEADS, SWA_HD)]
            w_out = l1_wo
        fin = fin_g if last else None
        xp = _half_ffn(xp, ml, bfn_p, gb, *fb, mrow=6, mixes=mix_p, w_out=w_out, final_g=fin)
        xs = _half_ffn(xs, ml, bfn_s, gb, *fb, mrow=6, mixes=mix_s, w_out=w_out, final_g=fin)
    return (xp.reshape(bp, tp, d), xs.reshape(bs, ts, d), *new_state)
```

```python
import functools

import numpy as np
import jax
import jax.numpy as jnp
from jax import lax
from jax.experimental import pallas as pl
from jax.experimental.pallas import tpu as pltpu

F32 = jnp.float32
BF16 = jnp.bfloat16

D_MODEL = 1024
FFN_DIM = 2816
N_MOD = 9
EPS = 1e-6
GRID_W = 64
GLA_HEADS = 4
GLA_DK = 64
GLA_DV = 128
GLA_GATE_RANK = 16
GLA_TAU = 16.0
GLA_CHUNK = 64
GLA_QK = GLA_HEADS * GLA_DK
GLA_V = GLA_HEADS * GLA_DV
FN_GROUPS = 4
FN_CH = 128
FN_W = FN_GROUPS * FN_CH
SWA_HEADS = 16
SWA_KV_HEADS = 4
SWA_HD = 64
SWA_Q = SWA_HEADS * SWA_HD
SWA_KV = SWA_KV_HEADS * SWA_HD
WINDOW = 128
ATTN_BLOCK = 128
ROPE_BASE = 10000.0
LOG2E = float(np.log2(np.e))

LANES = 128
VMEM_LIMIT = 56 * 1024 * 1024

TM = 1024
SUB = 512
SUB_PROJ = 256
TF = 256
GLA_TILE = 256
GLA_GROUP = 4
FN_ROWS = 256
FN_TOKENS = 2048
MOD_ROWS = 16
MOD_TN = 1024


def _params(sem):
    return pltpu.CompilerParams(dimension_semantics=sem, vmem_limit_bytes=VMEM_LIMIT)


def _resident(shape):
    nd = len(shape)
    return pl.BlockSpec(shape, lambda *_: (0,) * nd, pipeline_mode=pl.Buffered(1))


def _dot(a, b):
    return jnp.dot(a, b, preferred_element_type=F32)


def _silu(x):
    return x * jax.nn.sigmoid(x)


def _ada_norm(x, g, shift, scale):
    ms = jnp.mean(x * x, axis=-1, keepdims=True)
    return (x * lax.rsqrt(ms + EPS)) * (g * (1.0 + scale)) + shift


def _mod_kernel(c_ref, w_ref, b_ref, o_ref):
    c = c_ref[...]
    s = _silu(c).astype(BF16)
    o_ref[0] = _dot(s, w_ref[0].astype(BF16)) + b_ref[0]


def _modulation(cond, mod_w, mod_b):
    depth, d, n = mod_w.shape
    return pl.pallas_call(
        _mod_kernel,
        grid=(depth, n // MOD_TN),
        in_specs=[
            pl.BlockSpec((MOD_ROWS, d), lambda l, j: (0, 0)),
            pl.BlockSpec((1, d, MOD_TN), lambda l, j: (l, 0, j)),
            pl.BlockSpec((1, 1, MOD_TN), lambda l, j: (l, 0, j)),
        ],
        out_specs=pl.BlockSpec((1, MOD_ROWS, MOD_TN), lambda l, j: (l, 0, j)),
        out_shape=jax.ShapeDtypeStruct((depth, MOD_ROWS, n), F32),
        compiler_params=_params(("arbitrary", "arbitrary")),
        name="modulation",
    )(cond, mod_w, mod_b.reshape(depth, 1, n))


def _ffn_kernel(*refs, n_mix, mrow, final):
    x_ref = refs[0]
    mix_refs = refs[1:1 + n_mix]
    pos = 1 + n_mix
    if n_mix:
        wo_ref = refs[pos]
        pos += 1
    m_ref, g_ref, w1_ref, w3_ref, w2_ref = refs[pos:pos + 5]
    pos += 5
    if final:
        fg_ref = refs[pos]
        pos += 1
    o_ref, acc_ref, h_ref = refs[pos], refs[pos + 1], refs[pos + 2]

    def prologue(rows):
        x = x_ref[rows, :]
        if n_mix:
            off = 0
            mix = None
            for r in mix_refs:
                w = r.shape[1]
                t = _dot(r[rows, :], wo_ref[off:off + w, :])
                mix = t if mix is None else mix + t
                off += w
            x = x + m_ref[0, 5:6, :] * mix
        o_ref[rows, :] = x
        h = _ada_norm(x, g_ref[...], m_ref[0, mrow:mrow + 1, :], m_ref[0, mrow + 1:mrow + 2, :])
        h_ref[rows, :] = h.astype(BF16)

    def chunk(rows, j):
        h = h_ref[rows, :]
        a = _dot(h, w1_ref[:, j * TF:(j + 1) * TF])
        b = _dot(h, w3_ref[:, j * TF:(j + 1) * TF])
        t = _dot((_silu(a) * b).astype(BF16), w2_ref[j * TF:(j + 1) * TF, :])
        if j == 0:
            acc_ref[rows, :] = t
        else:
            acc_ref[rows, :] += t

    def epilogue(rows):
        y = o_ref[rows, :] + (0.5 * m_ref[0, mrow + 2:mrow + 3, :]) * acc_ref[rows, :]
        if final:
            ms = jnp.mean(y * y, axis=-1, keepdims=True)
            y = y * lax.rsqrt(ms + EPS) * fg_ref[...]
        o_ref[rows, :] = y

    subs = [slice(r0, r0 + SUB) for r0 in range(0, x_ref.shape[0], SUB)]
    for rows in subs:
        prologue(rows)
    for j in range(FFN_DIM // TF):
        for rows in subs:
            chunk(rows, j)
    for rows in subs:
        epilogue(rows)


def _pick(arr, lead):
    tail = arr.shape[len(lead):]
    idx = tuple(lead) + (0,) * len(tail)
    return pl.BlockSpec((None,) * len(lead) + tail, lambda *_: idx, pipeline_mode=pl.Buffered(1))


def _half_ffn(x, mods, bfn, g, w1, w3, w2, wsel, *, mrow, mixes=(), w_out=None, final_g=None):
    m, d = x.shape
    row = lambda i: (i, 0)
    ins = [x]
    specs = [pl.BlockSpec((TM, d), row)]
    for a in mixes:
        ins.append(a)
        specs.append(pl.BlockSpec((TM, a.shape[1]), row))
    if mixes:
        ins.append(w_out)
        specs.append(_resident(w_out.shape))
    ins += [mods, g, w1, w3, w2]
    specs += [pl.BlockSpec((1, MOD_ROWS, d), lambda i: (bfn(i), 0, 0)),
              _resident(g.shape), _pick(w1, wsel), _pick(w3, wsel), _pick(w2, wsel)]
    if final_g is not None:
        ins.append(final_g)
        specs.append(_resident(final_g.shape))
    return pl.pallas_call(
        functools.partial(_ffn_kernel, n_mix=len(mixes), mrow=mrow, final=final_g is not None),
        grid=(m // TM,),
        in_specs=specs,
        out_specs=pl.BlockSpec((TM, d), row),
        out_shape=jax.ShapeDtypeStruct((m, d), F32),
        scratch_shapes=[pltpu.VMEM((TM, d), F32), pltpu.VMEM((TM, d), BF16)],
        compiler_params=_params(("arbitrary",)),
        name="half_ffn",
    )(*ins)


def _l0_in_kernel(x_ref, m_ref, g_ref, wm_ref, wg_ref, wg2_ref, bg_ref,
                  q_ref, k_ref, v_ref, og_ref, la_ref, u_ref):
    for r0 in range(0, x_ref.shape[0], SUB_PROJ):
        rows = slice(r0, r0 + SUB_PROJ)
        h = _ada_norm(x_ref[rows, :], g_ref[...], m_ref[0, 3:4, :], m_ref[0, 4:5, :]).astype(BF16)
        q_ref[rows, :] = _dot(h, wm_ref[:, 0:256]) * (GLA_DK ** -0.5)
        k_ref[rows, :] = _dot(h, wm_ref[:, 256:512])
        v_ref[rows, :] = _dot(h, wm_ref[:, 512:1024]).astype(BF16)
        og_ref[rows, :] = _dot(h, wm_ref[:, 1024:1536])
        u_ref[rows, :] = _dot(h, wm_ref[:, 1536:2048]).astype(BF16)
        lr = _dot(h, wg_ref[...]).astype(BF16)
        z = _dot(lr, wg2_ref[...]) + bg_ref[...]
        log_sig = jnp.minimum(z, 0.0) - jnp.log(1.0 + jnp.exp(-jnp.abs(z)))
        la_ref[rows, :] = log_sig * (1.0 / GLA_TAU)


def _l0_in_proj(x, mods, bfn, g, wm, wg, wg2, bg):
    m, d = x.shape
    row = lambda i: (i, 0)
    outs = [(GLA_QK, F32), (GLA_QK, F32), (GLA_V, BF16), (GLA_V, F32), (2 * GLA_QK, F32), (FN_W, BF16)]
    return pl.pallas_call(
        _l0_in_kernel,
        grid=(m // TM,),
        in_specs=[pl.BlockSpec((TM, d), row),
                  pl.BlockSpec((1, MOD_ROWS, d), lambda i: (bfn(i), 0, 0)),
                  _resident(g.shape), _resident(wm.shape), _resident(wg.shape),
                  _resident(wg2.shape), _resident(bg.shape)],
        out_specs=[pl.BlockSpec((TM, w), row) for w, _ in outs],
        out_shape=[jax.ShapeDtypeStruct((m, w), dt) for w, dt in outs],
        compiler_params=_params(("arbitrary",)),
        name="l0_in_proj",
    )(x, mods, g, wm, wg, wg2, bg)


def _gla_tiles(qs, ks, vs, gs, ss, reverse):
    tt = GLA_TILE
    nch = tt // GLA_CHUNK
    seqs = range(len(qs))
    heads = range(GLA_HEADS)
    r = lax.broadcasted_iota(jnp.int32, (tt, tt), 0)
    c = lax.broadcasted_iota(jnp.int32, (tt, tt), 1)
    same = (r // GLA_CHUNK) == (c // GLA_CHUNK)
    m_intra = jnp.logical_and(same, (c >= r) if reverse else (c <= r))
    tri = jnp.where(m_intra, 1.0, 0.0).astype(BF16)
    cc = lax.broadcasted_iota(jnp.int32, (GLA_QK, tt), 1) // GLA_CHUNK
    lane = lax.broadcasted_iota(jnp.int32, (tt, GLA_QK), 1) // GLA_DK
    end_col = [ci * GLA_CHUNK + (0 if reverse else GLA_CHUNK - 1) for ci in range(nch)]
    zero_v = jnp.zeros((GLA_CHUNK, GLA_DV), BF16)
    zero_s = jnp.zeros((GLA_DK, GLA_DV), BF16)

    g_hi = [g.astype(BF16) for g in gs]
    g_lo = [(gs[i] - g_hi[i].astype(F32)).astype(BF16) for i in seqs]
    b = [_dot(tri, g_hi[i]) + _dot(tri, g_lo[i]) for i in seqs]
    qd = [qs[i] * jnp.exp(b[i]) for i in seqs]
    b_t = [x.T for x in b]
    k_t = [x.T for x in ks]
    ends = [[b_t[i][:, e:e + 1] for e in end_col] for i in seqs]
    ki_t, ke_t = [], []
    for i in seqs:
        bl_t = ends[i][nch - 1]
        for ci in range(nch - 2, -1, -1):
            bl_t = jnp.where(cc == ci, ends[i][ci], bl_t)
        ki_t.append((k_t[i] * jnp.exp(-b_t[i])).astype(BF16))
        ke_t.append(k_t[i] * jnp.exp(bl_t - b_t[i]))

    o_intra = [[None] * GLA_HEADS for _ in seqs]
    for h in heads:
        a = [_dot(jnp.where(lane == h, qd[i], 0.0).astype(BF16), ki_t[i]) for i in seqs]
        for i in seqs:
            am = jnp.where(m_intra, a[i], 0.0).astype(BF16)
            o_intra[i][h] = _dot(am, vs[i][:, h * GLA_DV:(h + 1) * GLA_DV])

    deltas = [[None] * GLA_HEADS for _ in seqs]
    for h in heads:
        for i in seqs:
            v_h = vs[i][:, h * GLA_DV:(h + 1) * GLA_DV]
            v_bd = jnp.concatenate(
                [jnp.concatenate([v_h[ci * GLA_CHUNK:(ci + 1) * GLA_CHUNK] if cj == ci else zero_v
                                  for cj in range(nch)], axis=1) for ci in range(nch)], axis=0)
            deltas[i][h] = _dot(ke_t[i][h * GLA_DK:(h + 1) * GLA_DK, :].astype(BF16), v_bd)

    s_h = [[ss[i][h * GLA_DK:(h + 1) * GLA_DK, :] for h in heads] for i in seqs]
    o_inter = [[None] * nch for _ in seqs]
    for ci in (range(nch - 1, -1, -1) if reverse else range(nch)):
        for i in seqs:
            s_bd = jnp.concatenate(
                [jnp.concatenate([s_h[i][h].astype(BF16) if hj == h else zero_s for hj in heads], axis=1)
                 for h in heads], axis=0)
            o_inter[i][ci] = _dot(qd[i][ci * GLA_CHUNK:(ci + 1) * GLA_CHUNK, :].astype(BF16), s_bd)
            decay = jnp.exp(ends[i][ci])
            s_h[i] = [decay[h * GLA_DK:(h + 1) * GLA_DK] * s_h[i][h]
                      + deltas[i][h][:, ci * GLA_DV:(ci + 1) * GLA_DV] for h in heads]
    outs = [jnp.concatenate(o_intra[i], axis=1) + jnp.concatenate(o_inter[i], axis=0) for i in seqs]
    return outs, [jnp.concatenate(s_h[i], axis=0) for i in seqs]


def _gla_kernel(q_ref, k_ref, v_ref, la_ref, og_ref, s0f_ref, s0b_ref, gh_ref,
                o_ref, sf_ref, sb_ref, s_ref, ob_ref, *, nt):
    p = pl.program_id(1)
    i = pl.program_id(2)
    grp = range(GLA_GROUP)

    @pl.when(p == 0)
    def _backward():
        @pl.when(i == 0)
        def _():
            s_ref[...] = s0b_ref[...]

        start = pl.multiple_of((nt - 1 - i) * GLA_TILE, GLA_TILE)
        outs, states = _gla_tiles(*[[ref[gi] for gi in grp] for ref in (q_ref, k_ref, v_ref, la_ref, s_ref)], True)
        for gi in grp:
            s_ref[gi] = states[gi]
            ob_ref[gi, pl.ds(start, GLA_TILE), :] = outs[gi]

        @pl.when(i == nt - 1)
        def _():
            sb_ref[...] = s_ref[...]

    @pl.when(p == 1)
    def _forward():
        @pl.when(i == 0)
        def _():
            s_ref[...] = s0f_ref[...]

        start = pl.multiple_of(i * GLA_TILE, GLA_TILE)
        outs, states = _gla_tiles(*[[ref[gi] for gi in grp] for ref in (q_ref, k_ref, v_ref, la_ref, s_ref)], False)
        for gi in grp:
            s_ref[gi] = states[gi]
            o = outs[gi] + ob_ref[gi, pl.ds(start, GLA_TILE), :]
            for h in range(GLA_HEADS):
                sl = slice(h * GLA_DV, (h + 1) * GLA_DV)
                oh = o[:, sl]
                ms = jnp.mean(oh * oh, axis=-1, keepdims=True)
                y = oh * lax.rsqrt(ms + EPS) * gh_ref[...]
                o_ref[gi, :, sl] = (y * _silu(og_ref[gi, :, sl])).astype(BF16)

        @pl.when(i == nt - 1)
        def _():
            sf_ref[...] = s_ref[...]


def _gla(q, k, v, la, og, s0f, s0b, g_head, batch, seq):
    nt = seq // GLA_TILE
    gg = GLA_GROUP
    tile = lambda b, p, i: (b, p * i + (1 - p) * (nt - 1 - i), 0)
    fwd_tile = lambda b, p, i: (b, p * i, 0)
    st = pl.BlockSpec((gg, GLA_QK, GLA_DV), lambda b, p, i: (b, 0, 0))
    return pl.pallas_call(
        functools.partial(_gla_kernel, nt=nt),
        grid=(batch // gg, 2, nt),
        in_specs=[pl.BlockSpec((gg, GLA_TILE, GLA_QK), tile),
                  pl.BlockSpec((gg, GLA_TILE, GLA_QK), tile),
                  pl.BlockSpec((gg, GLA_TILE, GLA_V), tile),
                  pl.BlockSpec((gg, GLA_TILE, GLA_QK), lambda b, p, i: tile(b, p, i)[:2] + (1 - p,)),
                  pl.BlockSpec((gg, GLA_TILE, GLA_V), fwd_tile),
                  st, st, _resident(g_head.shape)],
        out_specs=[pl.BlockSpec((gg, GLA_TILE, GLA_V), fwd_tile), st, st],
        out_shape=[jax.ShapeDtypeStruct((batch, seq, GLA_V), BF16),
                   jax.ShapeDtypeStruct((batch, GLA_QK, GLA_DV), F32),
                   jax.ShapeDtypeStruct((batch, GLA_QK, GLA_DV), F32)],
        scratch_shapes=[pltpu.VMEM((gg, GLA_QK, GLA_DV), F32), pltpu.VMEM((gg, seq, GLA_V), F32)],
        compiler_params=_params(("arbitrary", "arbitrary", "arbitrary")),
        name="gla",
    )(q, k, v, la, og, s0f, s0b, g_head)


def _fnet_kernel(u_ref, ccs_ref, cs_ref, o_ref, ucs_ref, *, seq, scale):
    nseq = u_ref.shape[0]
    rows = min(FN_ROWS, seq)
    for s in range(nseq):
        for gi in range(FN_GROUPS):
            sl = slice(gi * FN_CH, (gi + 1) * FN_CH)
            t = _dot(u_ref[s, :, sl], ccs_ref[...])
            ucs_ref[s, 0:seq, sl] = t[:, :FN_CH].astype(BF16)
            ucs_ref[s, seq:2 * seq, sl] = t[:, FN_CH:].astype(BF16)
    for s in range(nseq):
        for j in range(seq // rows):
            rs = slice(j * rows, (j + 1) * rows)
            o_ref[s, rs, :] = (_dot(cs_ref[rs, :], ucs_ref[s]) * scale).astype(BF16)


def _dft_tables(seq):
    def cs(n):
        idx = np.arange(n, dtype=np.int64)
        ang = 2.0 * np.pi * ((idx[:, None] * idx[None, :]) % n).astype(np.float64) / n
        return np.cos(ang), np.sin(ang)
    ct, st = cs(seq)
    cc, sc = cs(FN_CH)
    to = lambda a: jnp.asarray(a.astype(np.float32)).astype(BF16)
    return to(np.concatenate([ct, -st], axis=1)), to(np.concatenate([cc, sc], axis=1))


def _fnet(u, batch, seq):
    cs, ccs = _dft_tables(seq)
    nseq = max(1, FN_TOKENS // seq)
    blk = pl.BlockSpec((nseq, seq, FN_W), lambda b: (b, 0, 0))
    return pl.pallas_call(
        functools.partial(_fnet_kernel, seq=seq, scale=float((seq * FN_CH) ** -0.5)),
        grid=(batch // nseq,),
        in_specs=[blk, _resident(ccs.shape), _resident(cs.shape)],
        out_specs=blk,
        out_shape=jax.ShapeDtypeStruct((batch, seq, FN_W), BF16),
        scratch_shapes=[pltpu.VMEM((nseq, 2 * seq, FN_W), BF16)],
        compiler_params=_params(("arbitrary",)),
        name="fnet",
    )(u, ccs, cs)


def _rope128(x, cos, sin_signed, odd):
    back = pltpu.roll(x, 16, 1)
    fwd = pltpu.roll(x, 112, 1)
    return x * cos + jnp.where(odd, back, fwd) * sin_signed


def _l1_in_kernel(*refs, rope):
    if rope:
        x_ref, m_ref, g_ref, w_ref, cos_ref, sin_ref, q_ref, k_ref, v_ref = refs
        odd = (lax.broadcasted_iota(jnp.int32, (SUB_PROJ, LANES), 1) // 16) % 2 == 1
    else:
        x_ref, m_ref, g_ref, w_ref, q_ref, k_ref, v_ref = refs

    def project(h, rows, dst, col0, width, scale):
        for j in range(width // 256):
            y = _dot(h, w_ref[:, col0 + 256 * j:col0 + 256 * (j + 1)])
            if scale != 1.0:
                y = y * scale
            for half in range(2):
                yh = y[:, LANES * half:LANES * (half + 1)]
                if rope:
                    yh = _rope128(yh, cos_ref[rows, :], sin_ref[rows, :], odd)
                lo = 256 * j + LANES * half
                dst[rows, lo:lo + LANES] = yh.astype(dst.dtype)

    for r0 in range(0, x_ref.shape[0], SUB_PROJ):
        rows = slice(r0, r0 + SUB_PROJ)
        h = _ada_norm(x_ref[rows, :], g_ref[...], m_ref[0, 3:4, :], m_ref[0, 4:5, :]).astype(BF16)
        project(h, rows, q_ref, 0, SWA_Q, SWA_HD ** -0.5 * LOG2E)
        project(h, rows, k_ref, SWA_Q, SWA_KV, 1.0)
        v_ref[rows, :] = _dot(h, w_ref[:, SWA_Q + SWA_KV:SWA_Q + 2 * SWA_KV])


def _l1_in_proj(x, mods, bfn, g, w, rope_tabs=None, tiles_per_seq=None):
    m, d = x.shape
    row = lambda i: (i, 0)
    ins = [x, mods, g, w]
    specs = [pl.BlockSpec((TM, d), row),
             pl.BlockSpec((1, MOD_ROWS, d), lambda i: (bfn(i), 0, 0)),
             _resident(g.shape), _resident(w.shape)]
    if rope_tabs is not None:
        ins += list(rope_tabs)
        specs += [pl.BlockSpec((TM, LANES), lambda i: (i % tiles_per_seq, 0))] * 2
    return pl.pallas_call(
        functools.partial(_l1_in_kernel, rope=rope_tabs is not None),
        grid=(m // TM,),
        in_specs=specs,
        out_specs=[pl.BlockSpec((TM, SWA_Q), row), pl.BlockSpec((TM, SWA_KV), row), pl.BlockSpec((TM, SWA_KV), row)],
        out_shape=[jax.ShapeDtypeStruct((m, SWA_Q), BF16),
                   jax.ShapeDtypeStruct((m, SWA_KV), F32),
                   jax.ShapeDtypeStruct((m, SWA_KV), F32)],
        compiler_params=_params(("arbitrary",)),
        name="l1_in_proj",
    )(*ins)


def _rope_tables(seq):
    rows = seq // GRID_W
    row = np.repeat(np.arange(rows), GRID_W).astype(np.float64)
    col = (np.arange(rows * GRID_W) % GRID_W).astype(np.float64)
    n_freq = SWA_HD // 4
    inv = ROPE_BASE ** (-np.arange(n_freq, dtype=np.float64) / n_freq)
    ar = row[:, None] * inv
    ac = col[:, None] * inv
    ang = np.concatenate([ar, ar, ac, ac], axis=-1)
    sign = np.where((np.arange(SWA_HD) // n_freq) % 2 == 1, 1.0, -1.0)
    two = lambda t: jnp.asarray(np.concatenate([t, t], axis=-1).astype(np.float32))
    return two(np.cos(ang)), two(np.sin(ang) * sign)


def _attend(q_ref, ks, vs, biases, sink_ref, o_ref):
    kcat = ks[0] if len(ks) == 1 else jnp.concatenate(ks, axis=0)
    vcat = vs[0] if len(vs) == 1 else jnp.concatenate(vs, axis=0)
    n = kcat.shape[0]
    r = q_ref.shape[0]
    nblk = n // LANES
    low = lax.broadcasted_iota(jnp.int32, (n, LANES), 1) < SWA_HD
    ones_lo = jnp.where(low, 1.0, 0.0)
    ones_hi = jnp.where(low, 0.0, 1.0)
    low_out = lax.broadcasted_iota(jnp.int32, (2 * r, LANES), 1) < SWA_HD
    top = lax.broadcasted_iota(jnp.int32, (2 * r, 1), 0) < r
    biases = {t: jnp.concatenate([b, b], axis=0) for t, b in biases.items()}
    nt_dims = (((1,), (1,)), ((), ()))

    def scores(g):
        sl = slice(LANES * (g // 2), LANES * (g // 2 + 1))
        kcol, vcol = kcat[:, sl], vcat[:, sl]
        ksw, vsw = pltpu.roll(kcol, SWA_HD, 1), pltpu.roll(vcol, SWA_HD, 1)
        if g % 2 == 0:
            k_lo, k_hi = jnp.where(low, kcol, 0.0), jnp.where(low, 0.0, ksw)
            v_lo, v_hi = jnp.where(low, vcol, 0.0), jnp.where(low, 0.0, vsw)
        else:
            k_lo, k_hi = jnp.where(low, ksw, 0.0), jnp.where(low, 0.0, kcol)
            v_lo, v_hi = jnp.where(low, vsw, 0.0), jnp.where(low, 0.0, vcol)
        k_bd = jnp.concatenate([k_lo, k_hi], axis=0).astype(BF16)
        v_bd = jnp.concatenate([jnp.concatenate([v_lo, ones_lo], axis=1),
                                jnp.concatenate([v_hi, ones_hi], axis=1)], axis=0).astype(BF16)
        qs = jnp.concatenate([q_ref[:, LANES * (2 * g):LANES * (2 * g + 1)],
                              q_ref[:, LANES * (2 * g + 1):LANES * (2 * g + 2)]], axis=0)
        return lax.dot_general(qs, k_bd, nt_dims, preferred_element_type=F32), v_bd

    nxt = scores(0)
    for g in range(SWA_KV_HEADS):
        s, v_bd = nxt
        if g + 1 < SWA_KV_HEADS:
            nxt = scores(g + 1)
        probs, sink_terms = [], []
        for e in range(2):
            blocks = [s[:, e * n + LANES * t:e * n + LANES * (t + 1)] for t in range(nblk)]
            for t, b in biases.items():
                blocks[t] = blocks[t] + b
            mx = blocks[0]
            for b in blocks[1:]:
                mx = jnp.maximum(mx, b)
            sk = jnp.where(top, sink_ref[4 * g + e], sink_ref[4 * g + 2 + e]) * LOG2E
            m = jnp.maximum(jnp.max(mx, axis=-1, keepdims=True), sk)
            probs += [jnp.exp2(b - m).astype(BF16) for b in blocks]
            sink_terms.append(jnp.exp2(sk - m))
        res = _dot(jnp.concatenate(probs, axis=1), v_bd)
        denom = res[:, LANES:2 * LANES] + jnp.where(low_out, sink_terms[0], sink_terms[1])
        out = (res[:, 0:LANES] / denom).astype(o_ref.dtype)
        o_ref[:, LANES * (2 * g):LANES * (2 * g + 1)] = out[0:r]
        o_ref[:, LANES * (2 * g + 1):LANES * (2 * g + 2)] = out[r:2 * r]


def _ctx_attn_kernel(sink_ref, q_ref, k_ref, v_ref, o_ref):
    _attend(q_ref, [k_ref[...]], [v_ref[...]], {}, sink_ref, o_ref)


def _ctx_attention(q, k, v, sink, batch, seq):
    blk = lambda w: pl.BlockSpec((seq, w), lambda b: (b, 0))
    return pl.pallas_call(
        _ctx_attn_kernel,
        grid=(batch,),
        in_specs=[pl.BlockSpec(memory_space=pltpu.SMEM), blk(SWA_Q), blk(SWA_KV), blk(SWA_KV)],
        out_specs=blk(SWA_Q),
        out_shape=jax.ShapeDtypeStruct((batch * seq, SWA_Q), BF16),
        compiler_params=_params(("arbitrary",)),
        name="ctx_attention",
    )(sink, q, k, v)


def _lat_attn_kernel(sink_ref, q_ref, ck_ref, cv_ref, kp_ref, kc_ref, kn_ref, vp_ref, vc_ref, vn_ref, o_ref,
                     *, nb, n_ctx):
    i = pl.program_id(1)
    blk = ATTN_BLOCK
    row = lax.broadcasted_iota(jnp.int32, (blk, blk), 0)
    col = lax.broadcasted_iota(jnp.int32, (blk, blk), 1)
    off_prev = jnp.where(i > 0, 0, blk)
    off_next = jnp.where(i < nb - 1, 0, blk)
    bias_prev = jnp.where(col - row >= off_prev, 0.0, -jnp.inf)
    bias_next = jnp.where(row - col >= off_next, 0.0, -jnp.inf)
    first = n_ctx // LANES
    _attend(q_ref, [ck_ref[...], kp_ref[...], kc_ref[...], kn_ref[...]],
            [cv_ref[...], vp_ref[...], vc_ref[...], vn_ref[...]],
            {first: bias_prev, first + 2: bias_next}, sink_ref, o_ref)


def _lat_attention(q, k, v, ctx_k, ctx_v, sink, batch, seq, n_ctx):
    nb = seq // ATTN_BLOCK
    kv = lambda f: pl.BlockSpec((ATTN_BLOCK, SWA_KV), lambda b, i: (b * nb + f(i), 0))
    prev, cur, nxt = kv(lambda i: jnp.maximum(i - 1, 0)), kv(lambda i: i), kv(lambda i: jnp.minimum(i + 1, nb - 1))
    ctx = pl.BlockSpec((n_ctx, SWA_KV), lambda b, i: (b, 0))
    qo = pl.BlockSpec((ATTN_BLOCK, SWA_Q), lambda b, i: (b * nb + i, 0))
    return pl.pallas_call(
        functools.partial(_lat_attn_kernel, nb=nb, n_ctx=n_ctx),
        grid=(batch, nb),
        in_specs=[pl.BlockSpec(memory_space=pltpu.SMEM), qo, ctx, ctx, prev, cur, nxt, prev, cur, nxt],
        out_specs=qo,
        out_shape=jax.ShapeDtypeStruct((batch * seq, SWA_Q), BF16),
        compiler_params=_params(("arbitrary", "arbitrary")),
        name="lat_attention",
    )(sink, q, ctx_k, ctx_v, k, k, k, v, v, v)


def kernel(x_prompt, x_sample, state_l0_gla_fwd, state_l0_gla_bwd, cache_l1_k, cache_l1_v, c, c_ctx, mod_w, mod_b, norm_g, ffn_w1, ffn_w3, ffn_w2, l0_w_in, l0_w_gf, l0_b_gf, l0_w_gb, l0_b_gb, l0_g_head, l0_w_out, l1_w_in, l1_sink, l1_w_out, final_g):
    bp, tp, d = x_prompt.shape
    bs, ts, _ = x_sample.shape
    n_ctx = cache_l1_k.shape[1]
    depth = mod_w.shape[0]
    ctx_row = bs

    w1, w3, w2 = ffn_w1.astype(BF16), ffn_w3.astype(BF16), ffn_w2.astype(BF16)
    gate_lo = 2 * GLA_QK + 2 * GLA_V
    gate_hi = gate_lo + 2 * GLA_GATE_RANK
    l0_wm = jnp.concatenate([l0_w_in[:, :gate_lo], l0_w_in[:, gate_hi:]], axis=1).astype(BF16)
    l0_wg = jnp.pad(l0_w_in[:, gate_lo:gate_hi], ((0, 0), (0, LANES - 2 * GLA_GATE_RANK))).astype(BF16)
    wg2 = jnp.zeros((LANES, 2 * GLA_QK), F32)
    wg2 = wg2.at[:GLA_GATE_RANK, :GLA_QK].set(l0_w_gf)
    wg2 = wg2.at[GLA_GATE_RANK:2 * GLA_GATE_RANK, GLA_QK:].set(l0_w_gb).astype(BF16)
    bg = jnp.concatenate([l0_b_gf, l0_b_gb])[None, :]
    l0_wo, l1_wi, l1_wo = l0_w_out.astype(BF16), l1_w_in.astype(BF16), l1_w_out.astype(BF16)
    g_head = l0_g_head[None, :]
    sink = l1_sink.reshape(-1)
    fin_g = final_g[None, :]

    cond = jnp.zeros((MOD_ROWS, d), F32).at[:bs].set(c).at[ctx_row].set(c_ctx)
    mods = _modulation(cond, mod_w, mod_b)
    mods = mods.reshape(depth, MOD_ROWS, N_MOD, d)
    mods = jnp.pad(mods, ((0, 0), (0, 0), (0, MOD_ROWS - N_MOD), (0, 0)))

    xp = x_prompt.reshape(bp * tp, d)
    xs = x_sample.reshape(bs * ts, d)
    tiles_s = ts // TM
    bfn_p = lambda i: ctx_row
    bfn_s = lambda i: i // tiles_s
    zero_state = jnp.zeros((bp, GLA_QK, GLA_DV), F32)
    s0f = state_l0_gla_fwd.reshape(bs, GLA_QK, GLA_DV)
    s0b = state_l0_gla_bwd.reshape(bs, GLA_QK, GLA_DV)
    rope_tabs = _rope_tables(ts)

    new_state = []
    for layer in range(depth):
        ml = mods[layer]
        ga, gm, gb = (norm_g[layer, r][None, :] for r in range(3))
        fa = (w1, w3, w2, (layer, 0))
        fb = (w1, w3, w2, (layer, 1))
        last = layer == depth - 1
        xp = _half_ffn(xp, ml, bfn_p, ga, *fa, mrow=0)
        xs = _half_ffn(xs, ml, bfn_s, ga, *fa, mrow=0)
        if layer % 2 == 0:
            outs = []
            for x, bfn, nb, t, sf0, sb0 in ((xp, bfn_p, bp, tp, zero_state, zero_state), (xs, bfn_s, bs, ts, s0f, s0b)):
                q, k, v, og, la, u = _l0_in_proj(x, ml, bfn, gm, l0_wm, l0_wg, wg2, bg)
                seq3 = lambda a: a.reshape(nb, t, a.shape[-1])
                gla_out, s_f, s_b = _gla(seq3(q), seq3(k), seq3(v), seq3(la), seq3(og), sf0, sb0, g_head, nb, t)
                gla_out = gla_out.reshape(nb * t, GLA_V)
                fn_out = _fnet(seq3(u), nb, t).reshape(nb * t, FN_W)
                outs.append(((gla_out, fn_out), s_f, s_b))
            (mix_p, s_f, s_b), (mix_s, _, _) = outs
            new_state += [s_f.reshape(bp, GLA_HEADS, GLA_DK, GLA_DV), s_b.reshape(bp, GLA_HEADS, GLA_DK, GLA_DV)]
            w_out = l0_wo
        else:
            qp, kp, vp = _l1_in_proj(xp, ml, bfn_p, gm, l1_wi)
            mix_p = (_ctx_attention(qp, kp, vp, sink, bp, tp),)
            qs, ks, vs = _l1_in_proj(xs, ml, bfn_s, gm, l1_wi, rope_tabs, tiles_s)
            mix_s = (_lat_attention(qs, ks, vs, cache_l1_k.reshape(bs * n_ctx, SWA_KV),
                                    cache_l1_v.reshape(bs * n_ctx, SWA_KV), sink, bs, ts, n_ctx),)
            new_state += [kp.reshape(bp, tp, SWA_KV_HEADS, SWA_HD), vp.reshape(bp, tp, SWA_KV_HEADS, SWA_HD)]
            w_out = l1_wo
        fin = fin_g if last else None
        xp = _half_ffn(xp, ml, bfn_p, gb, *fb, mrow=6, mixes=mix_p, w_out=w_out, final_g=fin)
        xs = _half_ffn(xs, ml, bfn_s, gb, *fb, mrow=6, mixes=mix_s, w_out=w_out, final_g=fin)
    return (xp.reshape(bp, tp, d), xs.reshape(bs, ts, d), *new_state)
```

```python
import functools

import numpy as np
import jax
import jax.numpy as jnp
from jax import lax
from jax.experimental import pallas as pl
from jax.experimental.pallas import tpu as pltpu

F32 = jnp.float32
BF16 = jnp.bfloat16

D_MODEL = 1024
FFN_DIM = 2816
N_MOD = 9
EPS = 1e-6
GRID_W = 64
GLA_HEADS = 4
GLA_DK = 64
GLA_DV = 128
GLA_GATE_RANK = 16
GLA_TAU = 16.0
GLA_CHUNK = 64
GLA_QK = GLA_HEADS * GLA_DK
GLA_V = GLA_HEADS * GLA_DV
FN_GROUPS = 4
FN_CH = 128
FN_W = FN_GROUPS * FN_CH
SWA_HEADS = 16
SWA_KV_HEADS = 4
SWA_HD = 64
SWA_Q = SWA_HEADS * SWA_HD
SWA_KV = SWA_KV_HEADS * SWA_HD
WINDOW = 128
ATTN_BLOCK = 128
ROPE_BASE = 10000.0
LOG2E = float(np.log2(np.e))

LANES = 128
VMEM_LIMIT = 56 * 1024 * 1024

TM = 1024
SUB = 512
SUB_PROJ = 256
TF = 256
GLA_TILE = 256
GLA_GROUP = 4
FN_ROWS = 256
FN_TOKENS = 2048
MOD_ROWS = 16
MOD_TN = 1024


def _params(sem):
    return pltpu.CompilerParams(dimension_semantics=sem, vmem_limit_bytes=VMEM_LIMIT)


def _resident(shape):
    nd = len(shape)
    return pl.BlockSpec(shape, lambda *_: (0,) * nd, pipeline_mode=pl.Buffered(1))


def _dot(a, b):
    return jnp.dot(a, b, preferred_element_type=F32)


def _silu(x):
    return x * jax.nn.sigmoid(x)


def _ada_norm(x, g, shift, scale):
    ms = jnp.mean(x * x, axis=-1, keepdims=True)
    return (x * lax.rsqrt(ms + EPS)) * (g * (1.0 + scale)) + shift


def _mod_kernel(c_ref, w_ref, b_ref, o_ref):
    c = c_ref[...]
    s = _silu(c).astype(BF16)
    o_ref[0] = _dot(s, w_ref[0].astype(BF16)) + b_ref[0]


def _modulation(cond, mod_w, mod_b):
    depth, d, n = mod_w.shape
    return pl.pallas_call(
        _mod_kernel,
        grid=(depth, n // MOD_TN),
        in_specs=[
            pl.BlockSpec((MOD_ROWS, d), lambda l, j: (0, 0)),
            pl.BlockSpec((1, d, MOD_TN), lambda l, j: (l, 0, j)),
            pl.BlockSpec((1, 1, MOD_TN), lambda l, j: (l, 0, j)),
        ],
        out_specs=pl.BlockSpec((1, MOD_ROWS, MOD_TN), lambda l, j: (l, 0, j)),
        out_shape=jax.ShapeDtypeStruct((depth, MOD_ROWS, n), F32),
        compiler_params=_params(("arbitrary", "arbitrary")),
        name="modulation",
    )(cond, mod_w, mod_b.reshape(depth, 1, n))


def _ffn_kernel(*refs, n_mix, mrow, final, n_cast):
    x_ref = refs[0]
    mix_refs = refs[1:1 + n_mix]
    pos = 1 + n_mix
    if n_mix:
        wo_ref = refs[pos]
        pos += 1
    m_ref, g_ref, w1_ref, w3_ref, w2_ref = refs[pos:pos + 5]
    pos += 5
    if final:
        fg_ref = refs[pos]
        pos += 1
    cast_in = refs[pos:pos + n_cast]
    pos += n_cast
    o_ref = refs[pos]
    cast_out = refs[pos + 1:pos + 1 + n_cast]
    pos += 1 + n_cast
    acc_ref, h_ref = refs[pos], refs[pos + 1]
    for src, dst in zip(cast_in, cast_out):
        dst[...] = src[...].astype(BF16)

    def prologue(rows):
        x = x_ref[rows, :]
        if n_mix:
            off = 0
            mix = None
            for r in mix_refs:
                w = r.shape[1]
                t = _dot(r[rows, :], wo_ref[off:off + w, :])
                mix = t if mix is None else mix + t
                off += w
            x = x + m_ref[0, 5:6, :] * mix
        o_ref[rows, :] = x
        h = _ada_norm(x, g_ref[...], m_ref[0, mrow:mrow + 1, :], m_ref[0, mrow + 1:mrow + 2, :])
        h_ref[rows, :] = h.astype(BF16)

    def chunk(rows, j):
        h = h_ref[rows, :]
        a = _dot(h, w1_ref[:, j * TF:(j + 1) * TF])
        b = _dot(h, w3_ref[:, j * TF:(j + 1) * TF])
        t = _dot((_silu(a) * b).astype(BF16), w2_ref[j * TF:(j + 1) * TF, :])
        if j == 0:
            acc_ref[rows, :] = t
        else:
            acc_ref[rows, :] += t

    def epilogue(rows):
        y = o_ref[rows, :] + (0.5 * m_ref[0, mrow + 2:mrow + 3, :]) * acc_ref[rows, :]
        if final:
            ms = jnp.mean(y * y, axis=-1, keepdims=True)
            y = y * lax.rsqrt(ms + EPS) * fg_ref[...]
        o_ref[rows, :] = y

    subs = [slice(r0, r0 + SUB) for r0 in range(0, x_ref.shape[0], SUB)]
    for rows in subs:
        prologue(rows)
    for j in range(FFN_DIM // TF):
        for rows in subs:
            chunk(rows, j)
    for rows in subs:
        epilogue(rows)


def _half_ffn(x, mods, bfn, g, w1, w3, w2, *, mrow, mixes=(), w_out=None, final_g=None, cast=None):
    m, d = x.shape
    steps = m // TM
    row = lambda i: (i, 0)
    ins = [x]
    specs = [pl.BlockSpec((TM, d), row)]
    for a in mixes:
        ins.append(a)
        specs.append(pl.BlockSpec((TM, a.shape[1]), row))
    if mixes:
        ins.append(w_out)
        specs.append(_resident(w_out.shape))
    ins += [mods, g, w1, w3, w2]
    specs += [pl.BlockSpec((1, MOD_ROWS, d), lambda i: (bfn(i), 0, 0)),
              _resident(g.shape), _resident(w1.shape), _resident(w3.shape), _resident(w2.shape)]
    if final_g is not None:
        ins.append(final_g)
        specs.append(_resident(final_g.shape))
    out_specs = [pl.BlockSpec((TM, d), row)]
    out_shape = [jax.ShapeDtypeStruct((m, d), F32)]
    stacks, (cl, ch) = cast if cast is not None else ((), (0, 0))
    for s in stacks:
        rows, cols = s.shape[2] // steps, s.shape[3]
        ins.append(s)
        specs.append(pl.BlockSpec((None, None, rows, cols), lambda i: (cl, ch, i, 0)))
        out_specs.append(pl.BlockSpec((rows, cols), row))
        out_shape.append(jax.ShapeDtypeStruct(s.shape[2:], BF16))
    outs = pl.pallas_call(
        functools.partial(_ffn_kernel, n_mix=len(mixes), mrow=mrow, final=final_g is not None,
                          n_cast=len(stacks)),
        grid=(steps,),
        in_specs=specs,
        out_specs=out_specs,
        out_shape=out_shape,
        scratch_shapes=[pltpu.VMEM((TM, d), F32), pltpu.VMEM((TM, d), BF16)],
        compiler_params=_params(("arbitrary",)),
        name="half_ffn",
    )(*ins)
    return outs[0], tuple(outs[1:])


def _l0_in_kernel(x_ref, m_ref, g_ref, wm_ref, wg_ref, wg2_ref, bg_ref,
                  q_ref, k_ref, v_ref, og_ref, la_ref, u_ref):
    for r0 in range(0, x_ref.shape[0], SUB_PROJ):
        rows = slice(r0, r0 + SUB_PROJ)
        h = _ada_norm(x_ref[rows, :], g_ref[...], m_ref[0, 3:4, :], m_ref[0, 4:5, :]).astype(BF16)
        q_ref[rows, :] = _dot(h, wm_ref[:, 0:256]) * (GLA_DK ** -0.5)
        lr = _dot(h, wg_ref[...]).astype(BF16)
        k_ref[rows, :] = _dot(h, wm_ref[:, 256:512])
        z = _dot(lr, wg2_ref[...]) + bg_ref[...]
        log_sig = jnp.minimum(z, 0.0) - jnp.log(1.0 + jnp.exp(-jnp.abs(z)))
        la_ref[rows, :] = log_sig * (1.0 / GLA_TAU)
        v_ref[rows, :] = _dot(h, wm_ref[:, 512:1024]).astype(BF16)
        og_ref[rows, :] = _dot(h, wm_ref[:, 1024:1536])
        u_ref[rows, :] = _dot(h, wm_ref[:, 1536:2048]).astype(BF16)


def _l0_in_proj(x, mods, bfn, g, wm, wg, wg2, bg):
    m, d = x.shape
    row = lambda i: (i, 0)
    outs = [(GLA_QK, F32), (GLA_QK, F32), (GLA_V, BF16), (GLA_V, F32), (2 * GLA_QK, F32), (FN_W, BF16)]
    return pl.pallas_call(
        _l0_in_kernel,
        grid=(m // TM,),
        in_specs=[pl.BlockSpec((TM, d), row),
                  pl.BlockSpec((1, MOD_ROWS, d), lambda i: (bfn(i), 0, 0)),
                  _resident(g.shape), _resident(wm.shape), _resident(wg.shape),
                  _resident(wg2.shape), _resident(bg.shape)],
        out_specs=[pl.BlockSpec((TM, w), row) for w, _ in outs],
        out_shape=[jax.ShapeDtypeStruct((m, w), dt) for w, dt in outs],
        compiler_params=_params(("arbitrary",)),
        name="l0_in_proj",
    )(x, mods, g, wm, wg, wg2, bg)


def _gla_tiles(qs, ks, vs, gs, ss, reverse):
    tt = GLA_TILE
    nch = tt // GLA_CHUNK
    seqs = range(len(qs))
    heads = range(GLA_HEADS)
    r = lax.broadcasted_iota(jnp.int32, (tt, tt), 0)
    c = lax.broadcasted_iota(jnp.int32, (tt, tt), 1)
    same = (r // GLA_CHUNK) == (c // GLA_CHUNK)
    m_intra = jnp.logical_and(same, (c >= r) if reverse else (c <= r))
    tri = jnp.where(m_intra, 1.0, 0.0).astype(BF16)
    cc = lax.broadcasted_iota(jnp.int32, (GLA_QK, tt), 1) // GLA_CHUNK
    lane = lax.broadcasted_iota(jnp.int32, (tt, GLA_QK), 1) // GLA_DK
    end_col = [ci * GLA_CHUNK + (0 if reverse else GLA_CHUNK - 1) for ci in range(nch)]
    zero_v = jnp.zeros((GLA_CHUNK, GLA_DV), BF16)
    zero_s = jnp.zeros((GLA_DK, GLA_DV), BF16)

    g_hi = [g.astype(BF16) for g in gs]
    g_lo = [(gs[i] - g_hi[i].astype(F32)).astype(BF16) for i in seqs]
    b = [_dot(tri, g_hi[i]) + _dot(tri, g_lo[i]) for i in seqs]
    qd = [qs[i] * jnp.exp(b[i]) for i in seqs]
    b_t = [x.T for x in b]
    k_t = [x.T for x in ks]
    ends = [[b_t[i][:, e:e + 1] for e in end_col] for i in seqs]
    ki_t, ke_t = [], []
    for i in seqs:
        bl_t = ends[i][nch - 1]
        for ci in range(nch - 2, -1, -1):
            bl_t = jnp.where(cc == ci, ends[i][ci], bl_t)
        ki_t.append((k_t[i] * jnp.exp(-b_t[i])).astype(BF16))
        ke_t.append(k_t[i] * jnp.exp(bl_t - b_t[i]))

    o_intra = [[None] * GLA_HEADS for _ in seqs]
    for h in heads:
        a = [_dot(jnp.where(lane == h, qd[i], 0.0).astype(BF16), ki_t[i]) for i in seqs]
        for i in seqs:
            am = jnp.where(m_intra, a[i], 0.0).astype(BF16)
            o_intra[i][h] = _dot(am, vs[i][:, h * GLA_DV:(h + 1) * GLA_DV])

    deltas = [[None] * GLA_HEADS for _ in seqs]
    for h in heads:
        for i in seqs:
            v_h = vs[i][:, h * GLA_DV:(h + 1) * GLA_DV]
            v_bd = jnp.concatenate(
                [jnp.concatenate([v_h[ci * GLA_CHUNK:(ci + 1) * GLA_CHUNK] if cj == ci else zero_v
                                  for cj in range(nch)], axis=1) for ci in range(nch)], axis=0)
            deltas[i][h] = _dot(ke_t[i][h * GLA_DK:(h + 1) * GLA_DK, :].astype(BF16), v_bd)

    s_h = [[ss[i][h * GLA_DK:(h + 1) * GLA_DK, :] for h in heads] for i in seqs]
    o_inter = [[None] * nch for _ in seqs]
    for ci in (range(nch - 1, -1, -1) if reverse else range(nch)):
        for i in seqs:
            s_bd = jnp.concatenate(
                [jnp.concatenate([s_h[i][h].astype(BF16) if hj == h else zero_s for hj in heads], axis=1)
                 for h in heads], axis=0)
            o_inter[i][ci] = _dot(qd[i][ci * GLA_CHUNK:(ci + 1) * GLA_CHUNK, :].astype(BF16), s_bd)
            decay = jnp.exp(ends[i][ci])
            s_h[i] = [decay[h * GLA_DK:(h + 1) * GLA_DK] * s_h[i][h]
                      + deltas[i][h][:, ci * GLA_DV:(ci + 1) * GLA_DV] for h in heads]
    outs = [jnp.concatenate(o_intra[i], axis=1) + jnp.concatenate(o_inter[i], axis=0) for i in seqs]
    return outs, [jnp.concatenate(s_h[i], axis=0) for i in seqs]


def _gla_kernel(q_ref, k_ref, v_ref, la_ref, og_ref, s0f_ref, s0b_ref, gh_ref,
                o_ref, sf_ref, sb_ref, s_ref, ob_ref, *, nt):
    p = pl.program_id(1)
    i = pl.program_id(2)
    grp = range(GLA_GROUP)

    @pl.when(p == 0)
    def _backward():
        @pl.when(i == 0)
        def _():
            s_ref[...] = s0b_ref[...]

        start = pl.multiple_of((nt - 1 - i) * GLA_TILE, GLA_TILE)
        outs, states = _gla_tiles(*[[ref[gi] for gi in grp] for ref in (q_ref, k_ref, v_ref, la_ref, s_ref)], True)
        for gi in grp:
            s_ref[gi] = states[gi]
            ob_ref[gi, pl.ds(start, GLA_TILE), :] = outs[gi]

        @pl.when(i == nt - 1)
        def _():
            sb_ref[...] = s_ref[...]

    @pl.when(p == 1)
    def _forward():
        @pl.when(i == 0)
        def _():
            s_ref[...] = s0f_ref[...]

        start = pl.multiple_of(i * GLA_TILE, GLA_TILE)
        outs, states = _gla_tiles(*[[ref[gi] for gi in grp] for ref in (q_ref, k_ref, v_ref, la_ref, s_ref)], False)
        for gi in grp:
            s_ref[gi] = states[gi]
            o = outs[gi] + ob_ref[gi, pl.ds(start, GLA_TILE), :]
            for h in range(GLA_HEADS):
                sl = slice(h * GLA_DV, (h + 1) * GLA_DV)
                oh = o[:, sl]
                ms = jnp.mean(oh * oh, axis=-1, keepdims=True)
                y = oh * lax.rsqrt(ms + EPS) * gh_ref[...]
                o_ref[gi, :, sl] = (y * _silu(og_ref[gi, :, sl])).astype(BF16)

        @pl.when(i == nt - 1)
        def _():
            sf_ref[...] = s_ref[...]


def _gla(q, k, v, la, og, s0f, s0b, g_head, batch, seq):
    nt = seq // GLA_TILE
    gg = GLA_GROUP
    tile = lambda b, p, i: (b, p * i + (1 - p) * (nt - 1 - i), 0)
    fwd_tile = lambda b, p, i: (b, p * i, 0)
    st = pl.BlockSpec((gg, GLA_QK, GLA_DV), lambda b, p, i: (b, 0, 0))
    return pl.pallas_call(
        functools.partial(_gla_kernel, nt=nt),
        grid=(batch // gg, 2, nt),
        in_specs=[pl.BlockSpec((gg, GLA_TILE, GLA_QK), tile),
                  pl.BlockSpec((gg, GLA_TILE, GLA_QK), tile),
                  pl.BlockSpec((gg, GLA_TILE, GLA_V), tile),
                  pl.BlockSpec((gg, GLA_TILE, GLA_QK), lambda b, p, i: tile(b, p, i)[:2] + (1 - p,)),
                  pl.BlockSpec((gg, GLA_TILE, GLA_V), fwd_tile),
                  st, st, _resident(g_head.shape)],
        out_specs=[pl.BlockSpec((gg, GLA_TILE, GLA_V), fwd_tile), st, st],
        out_shape=[jax.ShapeDtypeStruct((batch, seq, GLA_V), BF16),
                   jax.ShapeDtypeStruct((batch, GLA_QK, GLA_DV), F32),
                   jax.ShapeDtypeStruct((batch, GLA_QK, GLA_DV), F32)],
        scratch_shapes=[pltpu.VMEM((gg, GLA_QK, GLA_DV), F32), pltpu.VMEM((gg, seq, GLA_V), F32)],
        compiler_params=_params(("arbitrary", "arbitrary", "arbitrary")),
        name="gla",
    )(q, k, v, la, og, s0f, s0b, g_head)


def _fnet_kernel(u_ref, ccs_ref, cs_ref, o_ref, ucs_ref, *, seq, scale):
    nseq = u_ref.shape[0]
    rows = min(FN_ROWS, seq)
    for s in range(nseq):
        for gi in range(FN_GROUPS):
            sl = slice(gi * FN_CH, (gi + 1) * FN_CH)
            t = _dot(u_ref[s, :, sl], ccs_ref[...])
            ucs_ref[s, 0:seq, sl] = t[:, :FN_CH].astype(BF16)
            ucs_ref[s, seq:2 * seq, sl] = t[:, FN_CH:].astype(BF16)
    for s in range(nseq):
        for j in range(seq // rows):
            rs = slice(j * rows, (j + 1) * rows)
            o_ref[s, rs, :] = (_dot(cs_ref[rs, :], ucs_ref[s]) * scale).astype(BF16)


def _dft_tables(seq):
    def cs(n):
        idx = np.arange(n, dtype=np.int64)
        ang = 2.0 * np.pi * ((idx[:, None] * idx[None, :]) % n).astype(np.float64) / n
        return np.cos(ang), np.sin(ang)
    ct, st = cs(seq)
    cc, sc = cs(FN_CH)
    to = lambda a: jnp.asarray(a.astype(np.float32)).astype(BF16)
    return to(np.concatenate([ct, -st], axis=1)), to(np.concatenate([cc, sc], axis=1))


def _fnet(u, batch, seq):
    cs, ccs = _dft_tables(seq)
    nseq = max(1, FN_TOKENS // seq)
    blk = pl.BlockSpec((nseq, seq, FN_W), lambda b: (b, 0, 0))
    return pl.pallas_call(
        functools.partial(_fnet_kernel, seq=seq, scale=float((seq * FN_CH) ** -0.5)),
        grid=(batch // nseq,),
        in_specs=[blk, _resident(ccs.shape), _resident(cs.shape)],
        out_specs=blk,
        out_shape=jax.ShapeDtypeStruct((batch, seq, FN_W), BF16),
        scratch_shapes=[pltpu.VMEM((nseq, 2 * seq, FN_W), BF16)],
        compiler_params=_params(("arbitrary",)),
        name="fnet",
    )(u, ccs, cs)


def _rope128(x, cos, sin_signed, odd):
    back = pltpu.roll(x, 16, 1)
    fwd = pltpu.roll(x, 112, 1)
    return x * cos + jnp.where(odd, back, fwd) * sin_signed


def _l1_in_kernel(*refs, rope):
    if rope:
        x_ref, m_ref, g_ref, w_ref, cos_ref, sin_ref, q_ref, k_ref, v_ref = refs
        odd = (lax.broadcasted_iota(jnp.int32, (SUB_PROJ, LANES), 1) // 16) % 2 == 1
    else:
        x_ref, m_ref, g_ref, w_ref, q_ref, k_ref, v_ref = refs

    def project(h, rows, dst, col0, width, scale):
        for j in range(width // 256):
            y = _dot(h, w_ref[:, col0 + 256 * j:col0 + 256 * (j + 1)])
            if scale != 1.0:
                y = y * scale
            for half in range(2):
                yh = y[:, LANES * half:LANES * (half + 1)]
                if rope:
                    yh = _rope128(yh, cos_ref[rows, :], sin_ref[rows, :], odd)
                lo = 256 * j + LANES * half
                dst[rows, lo:lo + LANES] = yh.astype(dst.dtype)

    for r0 in range(0, x_ref.shape[0], SUB_PROJ):
        rows = slice(r0, r0 + SUB_PROJ)
        h = _ada_norm(x_ref[rows, :], g_ref[...], m_ref[0, 3:4, :], m_ref[0, 4:5, :]).astype(BF16)
        project(h, rows, q_ref, 0, SWA_Q, SWA_HD ** -0.5 * LOG2E)
        project(h, rows, k_ref, SWA_Q, SWA_KV, 1.0)
        v_ref[rows, :] = _dot(h, w_ref[:, SWA_Q + SWA_KV:SWA_Q + 2 * SWA_KV])


def _l1_in_proj(x, mods, bfn, g, w, rope_tabs=None, tiles_per_seq=None):
    m, d = x.shape
    row = lambda i: (i, 0)
    ins = [x, mods, g, w]
    specs = [pl.BlockSpec((TM, d), row),
             pl.BlockSpec((1, MOD_ROWS, d), lambda i: (bfn(i), 0, 0)),
             _resident(g.shape), _resident(w.shape)]
    if rope_tabs is not None:
        ins += list(rope_tabs)
        specs += [pl.BlockSpec((TM, LANES), lambda i: (i % tiles_per_seq, 0))] * 2
    return pl.pallas_call(
        functools.partial(_l1_in_kernel, rope=rope_tabs is not None),
        grid=(m // TM,),
        in_specs=specs,
        out_specs=[pl.BlockSpec((TM, SWA_Q), row), pl.BlockSpec((TM, SWA_KV), row), pl.BlockSpec((TM, SWA_KV), row)],
        out_shape=[jax.ShapeDtypeStruct((m, SWA_Q), BF16),
                   jax.ShapeDtypeStruct((m, SWA_KV), F32),
                   jax.ShapeDtypeStruct((m, SWA_KV), F32)],
        compiler_params=_params(("arbitrary",)),
        name="l1_in_proj",
    )(*ins)


def _rope_tables(seq):
    rows = seq // GRID_W
    row = np.repeat(np.arange(rows), GRID_W).astype(np.float64)
    col = (np.arange(rows * GRID_W) % GRID_W).astype(np.float64)
    n_freq = SWA_HD // 4
    inv = ROPE_BASE ** (-np.arange(n_freq, dtype=np.float64) / n_freq)
    ar = row[:, None] * inv
    ac = col[:, None] * inv
    ang = np.concatenate([ar, ar, ac, ac], axis=-1)
    sign = np.where((np.arange(SWA_HD) // n_freq) % 2 == 1, 1.0, -1.0)
    two = lambda t: jnp.asarray(np.concatenate([t, t], axis=-1).astype(np.float32))
    return two(np.cos(ang)), two(np.sin(ang) * sign)


def _attend(q_ref, ks, vs, biases, sink_ref, o_ref):
    kcat = ks[0] if len(ks) == 1 else jnp.concatenate(ks, axis=0)
    vcat = vs[0] if len(vs) == 1 else jnp.concatenate(vs, axis=0)
    n = kcat.shape[0]
    r = q_ref.shape[0]
    nblk = n // LANES
    low = lax.broadcasted_iota(jnp.int32, (n, LANES), 1) < SWA_HD
    ones_lo = jnp.where(low, 1.0, 0.0)
    ones_hi = jnp.where(low, 0.0, 1.0)
    low_out = lax.broadcasted_iota(jnp.int32, (2 * r, LANES), 1) < SWA_HD
    top = lax.broadcasted_iota(jnp.int32, (2 * r, 1), 0) < r
    biases = {t: jnp.concatenate([b, b], axis=0) for t, b in biases.items()}
    nt_dims = (((1,), (1,)), ((), ()))

    def scores(g):
        sl = slice(LANES * (g // 2), LANES * (g // 2 + 1))
        kcol, vcol = kcat[:, sl], vcat[:, sl]
        ksw, vsw = pltpu.roll(kcol, SWA_HD, 1), pltpu.roll(vcol, SWA_HD, 1)
        if g % 2 == 0:
            k_lo, k_hi = jnp.where(low, kcol, 0.0), jnp.where(low, 0.0, ksw)
            v_lo, v_hi = jnp.where(low, vcol, 0.0), jnp.where(low, 0.0, vsw)
        else:
            k_lo, k_hi = jnp.where(low, ksw, 0.0), jnp.where(low, 0.0, kcol)
            v_lo, v_hi = jnp.where(low, vsw, 0.0), jnp.where(low, 0.0, vcol)
        k_bd = jnp.concatenate([k_lo, k_hi], axis=0).astype(BF16)
        v_bd = jnp.concatenate([jnp.concatenate([v_lo, ones_lo], axis=1),
                                jnp.concatenate([v_hi, ones_hi], axis=1)], axis=0).astype(BF16)
        qs = jnp.concatenate([q_ref[:, LANES * (2 * g):LANES * (2 * g + 1)],
                              q_ref[:, LANES * (2 * g + 1):LANES * (2 * g + 2)]], axis=0)
        return lax.dot_general(qs, k_bd, nt_dims, preferred_element_type=F32), v_bd

    nxt = scores(0)
    for g in range(SWA_KV_HEADS):
        s, v_bd = nxt
        if g + 1 < SWA_KV_HEADS:
            nxt = scores(g + 1)
        probs, sink_terms = [], []
        for e in range(2):
            blocks = [s[:, e * n + LANES * t:e * n + LANES * (t + 1)] for t in range(nblk)]
            for t, b in biases.items():
                blocks[t] = blocks[t] + b
            mx = blocks[0]
            for b in blocks[1:]:
                mx = jnp.maximum(mx, b)
            sk = jnp.where(top, sink_ref[4 * g + e], sink_ref[4 * g + 2 + e]) * LOG2E
            m = jnp.maximum(jnp.max(mx, axis=-1, keepdims=True), sk)
            probs += [jnp.exp2(b - m).astype(BF16) for b in blocks]
            sink_terms.append(jnp.exp2(sk - m))
        res = _dot(jnp.concatenate(probs, axis=1), v_bd)
        denom = res[:, LANES:2 * LANES] + jnp.where(low_out, sink_terms[0], sink_terms[1])
        out = (res[:, 0:LANES] / denom).astype(o_ref.dtype)
        o_ref[:, LANES * (2 * g):LANES * (2 * g + 1)] = out[0:r]
        o_ref[:, LANES * (2 * g + 1):LANES * (2 * g + 2)] = out[r:2 * r]


def _ctx_attn_kernel(sink_ref, q_ref, k_ref, v_ref, o_ref):
    _attend(q_ref, [k_ref[...]], [v_ref[...]], {}, sink_ref, o_ref)


def _ctx_attention(q, k, v, sink, batch, seq):
    blk = lambda w: pl.BlockSpec((seq, w), lambda b: (b, 0))
    return pl.pallas_call(
        _ctx_attn_kernel,
        grid=(batch,),
        in_specs=[pl.BlockSpec(memory_space=pltpu.SMEM), blk(SWA_Q), blk(SWA_KV), blk(SWA_KV)],
        out_specs=blk(SWA_Q),
        out_shape=jax.ShapeDtypeStruct((batch * seq, SWA_Q), BF16),
        compiler_params=_params(("arbitrary",)),
        name="ctx_attention",
    )(sink, q, k, v)


def _lat_attn_kernel(sink_ref, q_ref, ck_ref, cv_ref, kp_ref, kc_ref, kn_ref, vp_ref, vc_ref, vn_ref, o_ref,
                     *, nb, n_ctx):
    i = pl.program_id(1)
    blk = ATTN_BLOCK
    row = lax.broadcasted_iota(jnp.int32, (blk, blk), 0)
    col = lax.broadcasted_iota(jnp.int32, (blk, blk), 1)
    off_prev = jnp.where(i > 0, 0, blk)
    off_next = jnp.where(i < nb - 1, 0, blk)
    bias_prev = jnp.where(col - row >= off_prev, 0.0, -jnp.inf)
    bias_next = jnp.where(row - col >= off_next, 0.0, -jnp.inf)
    first = n_ctx // LANES
    _attend(q_ref, [ck_ref[...], kp_ref[...], kc_ref[...], kn_ref[...]],
            [cv_ref[...], vp_ref[...], vc_ref[...], vn_ref[...]],
            {first: bias_prev, first + 2: bias_next}, sink_ref, o_ref)


def _lat_attention(q, k, v, ctx_k, ctx_v, sink, batch, seq, n_ctx):
    nb = seq // ATTN_BLOCK
    kv = lambda f: pl.BlockSpec((ATTN_BLOCK, SWA_KV), lambda b, i: (b * nb + f(i), 0))
    prev, cur, nxt = kv(lambda i: jnp.maximum(i - 1, 0)), kv(lambda i: i), kv(lambda i: jnp.minimum(i + 1, nb - 1))
    ctx = pl.BlockSpec((n_ctx, SWA_KV), lambda b, i: (b, 0))
    qo = pl.BlockSpec((ATTN_BLOCK, SWA_Q), lambda b, i: (b * nb + i, 0))
    return pl.pallas_call(
        functools.partial(_lat_attn_kernel, nb=nb, n_ctx=n_ctx),
        grid=(batch, nb),
        in_specs=[pl.BlockSpec(memory_space=pltpu.SMEM), qo, ctx, ctx, prev, cur, nxt, prev, cur, nxt],
        out_specs=qo,
        out_shape=jax.ShapeDtypeStruct((batch * seq, SWA_Q), BF16),
        compiler_params=_params(("arbitrary", "arbitrary")),
        name="lat_attention",
    )(sink, q, ctx_k, ctx_v, k, k, k, v, v, v)


def kernel(x_prompt, x_sample, state_l0_gla_fwd, state_l0_gla_bwd, cache_l1_k, cache_l1_v, c, c_ctx, mod_w, mod_b, norm_g, ffn_w1, ffn_w3, ffn_w2, l0_w_in, l0_w_gf, l0_b_gf, l0_w_gb, l0_b_gb, l0_g_head, l0_w_out, l1_w_in, l1_sink, l1_w_out, final_g):
    bp, tp, d = x_prompt.shape
    bs, ts, _ = x_sample.shape
    n_ctx = cache_l1_k.shape[1]
    depth = mod_w.shape[0]
    ctx_row = bs

    ffn_stacks = (ffn_w1, ffn_w3, ffn_w2)
    ffn_w = tuple(s[0, 0].astype(BF16) for s in ffn_stacks)
    gate_lo = 2 * GLA_QK + 2 * GLA_V
    gate_hi = gate_lo + 2 * GLA_GATE_RANK
    l0_wm = jnp.concatenate([l0_w_in[:, :gate_lo], l0_w_in[:, gate_hi:]], axis=1).astype(BF16)
    l0_wg = jnp.pad(l0_w_in[:, gate_lo:gate_hi], ((0, 0), (0, LANES - 2 * GLA_GATE_RANK))).astype(BF16)
    wg2 = jnp.zeros((LANES, 2 * GLA_QK), F32)
    wg2 = wg2.at[:GLA_GATE_RANK, :GLA_QK].set(l0_w_gf)
    wg2 = wg2.at[GLA_GATE_RANK:2 * GLA_GATE_RANK, GLA_QK:].set(l0_w_gb).astype(BF16)
    bg = jnp.concatenate([l0_b_gf, l0_b_gb])[None, :]
    l0_wo, l1_wi, l1_wo = l0_w_out.astype(BF16), l1_w_in.astype(BF16), l1_w_out.astype(BF16)
    g_head = l0_g_head[None, :]
    sink = l1_sink.reshape(-1)
    fin_g = final_g[None, :]

    cond = jnp.zeros((MOD_ROWS, d), F32).at[:bs].set(c).at[ctx_row].set(c_ctx)
    mods = _modulation(cond, mod_w, mod_b)
    mods = mods.reshape(depth, MOD_ROWS, N_MOD, d)
    mods = jnp.pad(mods, ((0, 0), (0, 0), (0, MOD_ROWS - N_MOD), (0, 0)))

    xp = x_prompt.reshape(bp * tp, d)
    xs = x_sample.reshape(bs * ts, d)
    tiles_s = ts // TM
    bfn_p = lambda i: ctx_row
    bfn_s = lambda i: i // tiles_s
    zero_state = jnp.zeros((bp, GLA_QK, GLA_DV), F32)
    s0f = state_l0_gla_fwd.reshape(bs, GLA_QK, GLA_DV)
    s0b = state_l0_gla_bwd.reshape(bs, GLA_QK, GLA_DV)
    rope_tabs = _rope_tables(ts)

    new_state = []
    for layer in range(depth):
        ml = mods[layer]
        ga, gm, gb = (norm_g[layer, r][None, :] for r in range(3))
        last = layer == depth - 1
        xp, _ = _half_ffn(xp, ml, bfn_p, ga, *ffn_w, mrow=0)
        xs, ffn_w_b = _half_ffn(xs, ml, bfn_s, ga, *ffn_w, mrow=0, cast=(ffn_stacks, (layer, 1)))
        if layer % 2 == 0:
            outs = []
            for x, bfn, nb, t, sf0, sb0 in ((xp, bfn_p, bp, tp, zero_state, zero_state), (xs, bfn_s, bs, ts, s0f, s0b)):
                q, k, v, og, la, u = _l0_in_proj(x, ml, bfn, gm, l0_wm, l0_wg, wg2, bg)
                seq3 = lambda a: a.reshape(nb, t, a.shape[-1])
                gla_out, s_f, s_b = _gla(seq3(q), seq3(k), seq3(v), seq3(la), seq3(og), sf0, sb0, g_head, nb, t)
                gla_out = gla_out.reshape(nb * t, GLA_V)
                fn_out = _fnet(seq3(u), nb, t).reshape(nb * t, FN_W)
                outs.append(((gla_out, fn_out), s_f, s_b))
            (mix_p, s_f, s_b), (mix_s, _, _) = outs
            new_state += [s_f.reshape(bp, GLA_HEADS, GLA_DK, GLA_DV), s_b.reshape(bp, GLA_HEADS, GLA_DK, GLA_DV)]
            w_out = l0_wo
        else:
            qp, kp, vp = _l1_in_proj(xp, ml, bfn_p, gm, l1_wi)
            mix_p = (_ctx_attention(qp, kp, vp, sink, bp, tp),)
            qs, ks, vs = _l1_in_proj(xs, ml, bfn_s, gm, l1_wi, rope_tabs, tiles_s)
            mix_s = (_lat_attention(qs, ks, vs, cache_l1_k.reshape(bs * n_ctx, SWA_KV),
                                    cache_l1_v.reshape(bs * n_ctx, SWA_KV), sink, bs, ts, n_ctx),)
            new_state += [kp.reshape(bp, tp, SWA_KV_HEADS, SWA_HD), vp.reshape(bp, tp, SWA_KV_HEADS, SWA_HD)]
            w_out = l1_wo
        fin = fin_g if last else None
        xp, _ = _half_ffn(xp, ml, bfn_p, gb, *ffn_w_b, mrow=6, mixes=mix_p, w_out=w_out, final_g=fin)
        xs, ffn_w = _half_ffn(xs, ml, bfn_s, gb, *ffn_w_b, mrow=6, mixes=mix_s, w_out=w_out, final_g=fin,
                              cast=None if last else (ffn_stacks, (layer + 1, 0)))
    return (xp.reshape(bp, tp, d), xs.reshape(bs, ts, d), *new_state)
```

```python
import functools

import numpy as np
import jax
import jax.numpy as jnp
from jax import lax
from jax.experimental import pallas as pl
from jax.experimental.pallas import tpu as pltpu

F32 = jnp.float32
BF16 = jnp.bfloat16

D_MODEL = 1024
FFN_DIM = 2816
N_MOD = 9
EPS = 1e-6
GRID_W = 64
GLA_HEADS = 4
GLA_DK = 64
GLA_DV = 128
GLA_GATE_RANK = 16
GLA_TAU = 16.0
GLA_CHUNK = 64
GLA_QK = GLA_HEADS * GLA_DK
GLA_V = GLA_HEADS * GLA_DV
FN_GROUPS = 4
FN_CH = 128
FN_W = FN_GROUPS * FN_CH
SWA_HEADS = 16
SWA_KV_HEADS = 4
SWA_HD = 64
SWA_Q = SWA_HEADS * SWA_HD
SWA_KV = SWA_KV_HEADS * SWA_HD
WINDOW = 128
ATTN_BLOCK = 128
ROPE_BASE = 10000.0
LOG2E = float(np.log2(np.e))

LANES = 128
VMEM_LIMIT = 56 * 1024 * 1024

TM = 1024
SUB = 512
SUB_PROJ = 256
TF = 256
GLA_TILE = 256
GLA_GROUP = 4
FN_ROWS = 256
FN_TOKENS = 2048
FN_PAD = 16
MOD_ROWS = 16
MOD_TN = 1024


def _params(sem):
    return pltpu.CompilerParams(dimension_semantics=sem, vmem_limit_bytes=VMEM_LIMIT)


def _resident(shape):
    nd = len(shape)
    return pl.BlockSpec(shape, lambda *_: (0,) * nd, pipeline_mode=pl.Buffered(1))


def _dot(a, b):
    return jnp.dot(a, b, preferred_element_type=F32)


def _silu(x):
    return x * jax.nn.sigmoid(x)


def _ada_norm(x, g, shift, scale):
    ms = jnp.mean(x * x, axis=-1, keepdims=True)
    return (x * lax.rsqrt(ms + EPS)) * (g * (1.0 + scale)) + shift


def _mod_kernel(c_ref, w_ref, b_ref, o_ref):
    c = c_ref[...]
    s = _silu(c).astype(BF16)
    o_ref[0] = _dot(s, w_ref[0].astype(BF16)) + b_ref[0]


def _modulation(cond, mod_w, mod_b):
    depth, d, n = mod_w.shape
    return pl.pallas_call(
        _mod_kernel,
        grid=(depth, n // MOD_TN),
        in_specs=[
            pl.BlockSpec((MOD_ROWS, d), lambda l, j: (0, 0)),
            pl.BlockSpec((1, d, MOD_TN), lambda l, j: (l, 0, j)),
            pl.BlockSpec((1, 1, MOD_TN), lambda l, j: (l, 0, j)),
        ],
        out_specs=pl.BlockSpec((1, MOD_ROWS, MOD_TN), lambda l, j: (l, 0, j)),
        out_shape=jax.ShapeDtypeStruct((depth, MOD_ROWS, n), F32),
        compiler_params=_params(("arbitrary", "arbitrary")),
        name="modulation",
    )(cond, mod_w, mod_b.reshape(depth, 1, n))


def _ffn_kernel(*refs, n_mix, mrow, final, n_cast):
    x_ref = refs[0]
    mix_refs = refs[1:1 + n_mix]
    pos = 1 + n_mix
    if n_mix:
        wo_ref = refs[pos]
        pos += 1
    m_ref, g_ref, w1_ref, w3_ref, w2_ref = refs[pos:pos + 5]
    pos += 5
    if final:
        fg_ref = refs[pos]
        pos += 1
    cast_in = refs[pos:pos + n_cast]
    pos += n_cast
    o_ref = refs[pos]
    cast_out = refs[pos + 1:pos + 1 + n_cast]
    pos += 1 + n_cast
    acc_ref, h_ref = refs[pos], refs[pos + 1]
    for src, dst in zip(cast_in, cast_out):
        dst[...] = src[...].astype(BF16)

    def prologue(rows):
        x = x_ref[rows, :]
        if n_mix:
            off = 0
            mix = None
            for r in mix_refs:
                w = r.shape[1]
                t = _dot(r[rows, :], wo_ref[off:off + w, :])
                mix = t if mix is None else mix + t
                off += w
            x = x + m_ref[0, 5:6, :] * mix
        o_ref[rows, :] = x
        h = _ada_norm(x, g_ref[...], m_ref[0, mrow:mrow + 1, :], m_ref[0, mrow + 1:mrow + 2, :])
        h_ref[rows, :] = h.astype(BF16)

    def chunk(rows, j):
        h = h_ref[rows, :]
        a = _dot(h, w1_ref[:, j * TF:(j + 1) * TF])
        b = _dot(h, w3_ref[:, j * TF:(j + 1) * TF])
        t = _dot((_silu(a) * b).astype(BF16), w2_ref[j * TF:(j + 1) * TF, :])
        if j == 0:
            acc_ref[rows, :] = t
        else:
            acc_ref[rows, :] += t

    def epilogue(rows):
        y = o_ref[rows, :] + (0.5 * m_ref[0, mrow + 2:mrow + 3, :]) * acc_ref[rows, :]
        if final:
            ms = jnp.mean(y * y, axis=-1, keepdims=True)
            y = y * lax.rsqrt(ms + EPS) * fg_ref[...]
        o_ref[rows, :] = y

    subs = [slice(r0, r0 + SUB) for r0 in range(0, x_ref.shape[0], SUB)]
    for rows in subs:
        prologue(rows)
    for j in range(FFN_DIM // TF):
        for rows in subs:
            chunk(rows, j)
    for rows in subs:
        epilogue(rows)


def _half_ffn(x, mods, bfn, g, w1, w3, w2, *, mrow, mixes=(), w_out=None, final_g=None, cast=None):
    m, d = x.shape
    steps = m // TM
    row = lambda i: (i, 0)
    ins = [x]
    specs = [pl.BlockSpec((TM, d), row)]
    for a in mixes:
        ins.append(a)
        specs.append(pl.BlockSpec((TM, a.shape[1]), row))
    if mixes:
        ins.append(w_out)
        specs.append(_resident(w_out.shape))
    ins += [mods, g, w1, w3, w2]
    specs += [pl.BlockSpec((1, MOD_ROWS, d), lambda i: (bfn(i), 0, 0)),
              _resident(g.shape), _resident(w1.shape), _resident(w3.shape), _resident(w2.shape)]
    if final_g is not None:
        ins.append(final_g)
        specs.append(_resident(final_g.shape))
    out_specs = [pl.BlockSpec((TM, d), row)]
    out_shape = [jax.ShapeDtypeStruct((m, d), F32)]
    cast = tuple(cast or ())
    for s, lead in cast:
        rows, cols = s.shape[-2] // steps, s.shape[-1]
        ins.append(s)
        specs.append(pl.BlockSpec((None,) * len(lead) + (rows, cols), lambda i, lead=tuple(lead): lead + (i, 0)))
        out_specs.append(pl.BlockSpec((rows, cols), row))
        out_shape.append(jax.ShapeDtypeStruct(s.shape[-2:], BF16))
    outs = pl.pallas_call(
        functools.partial(_ffn_kernel, n_mix=len(mixes), mrow=mrow, final=final_g is not None,
                          n_cast=len(cast)),
        grid=(steps,),
        in_specs=specs,
        out_specs=out_specs,
        out_shape=out_shape,
        scratch_shapes=[pltpu.VMEM((TM, d), F32), pltpu.VMEM((TM, d), BF16)],
        compiler_params=_params(("arbitrary",)),
        name="half_ffn",
    )(*ins)
    return outs[0], tuple(outs[1:])


def _l0_in_kernel(x_ref, m_ref, g_ref, wm_ref, wg_ref, wg2_ref, bg_ref,
                  q_ref, k_ref, v_ref, og_ref, la_ref, u_ref):
    for r0 in range(0, x_ref.shape[0], SUB_PROJ):
        rows = slice(r0, r0 + SUB_PROJ)
        h = _ada_norm(x_ref[rows, :], g_ref[...], m_ref[0, 3:4, :], m_ref[0, 4:5, :]).astype(BF16)
        q_ref[rows, :] = _dot(h, wm_ref[:, 0:256]) * (GLA_DK ** -0.5)
        lr = _dot(h, wg_ref[...]).astype(BF16)
        k_ref[rows, :] = _dot(h, wm_ref[:, 256:512])
        z = _dot(lr, wg2_ref[...]) + bg_ref[...]
        log_sig = jnp.minimum(z, 0.0) - jnp.log(1.0 + jnp.exp(-jnp.abs(z)))
        la_ref[rows, :] = log_sig * (1.0 / GLA_TAU)
        v_ref[rows, :] = _dot(h, wm_ref[:, 512:1024]).astype(BF16)
        og_ref[rows, :] = _dot(h, wm_ref[:, 1024:1536])
        u_ref[rows, :] = _dot(h, wm_ref[:, 1536:2048]).astype(BF16)


def _l0_in_proj(x, mods, bfn, g, wm, wg, wg2, bg):
    m, d = x.shape
    row = lambda i: (i, 0)
    outs = [(GLA_QK, F32), (GLA_QK, F32), (GLA_V, BF16), (GLA_V, F32), (2 * GLA_QK, F32), (FN_W, BF16)]
    return pl.pallas_call(
        _l0_in_kernel,
        grid=(m // TM,),
        in_specs=[pl.BlockSpec((TM, d), row),
                  pl.BlockSpec((1, MOD_ROWS, d), lambda i: (bfn(i), 0, 0)),
                  _resident(g.shape), _resident(wm.shape), _resident(wg.shape),
                  _resident(wg2.shape), _resident(bg.shape)],
        out_specs=[pl.BlockSpec((TM, w), row) for w, _ in outs],
        out_shape=[jax.ShapeDtypeStruct((m, w), dt) for w, dt in outs],
        compiler_params=_params(("arbitrary",)),
        name="l0_in_proj",
    )(x, mods, g, wm, wg, wg2, bg)


def _gla_tiles(qs, ks, vs, gs, ss, reverse):
    tt = GLA_TILE
    nch = tt // GLA_CHUNK
    seqs = range(len(qs))
    heads = range(GLA_HEADS)
    r = lax.broadcasted_iota(jnp.int32, (tt, tt), 0)
    c = lax.broadcasted_iota(jnp.int32, (tt, tt), 1)
    same = (r // GLA_CHUNK) == (c // GLA_CHUNK)
    m_intra = jnp.logical_and(same, (c >= r) if reverse else (c <= r))
    tri = jnp.where(m_intra, 1.0, 0.0).astype(BF16)
    cc = lax.broadcasted_iota(jnp.int32, (GLA_QK, tt), 1) // GLA_CHUNK
    lane = lax.broadcasted_iota(jnp.int32, (tt, GLA_QK), 1) // GLA_DK
    end_col = [ci * GLA_CHUNK + (0 if reverse else GLA_CHUNK - 1) for ci in range(nch)]
    zero_v = jnp.zeros((GLA_CHUNK, GLA_DV), BF16)
    zero_s = jnp.zeros((GLA_DK, GLA_DV), BF16)

    g_hi = [g.astype(BF16) for g in gs]
    g_lo = [(gs[i] - g_hi[i].astype(F32)).astype(BF16) for i in seqs]
    b = [_dot(tri, g_hi[i]) + _dot(tri, g_lo[i]) for i in seqs]
    qd = [qs[i] * jnp.exp(b[i]) for i in seqs]
    b_t = [x.T for x in b]
    k_t = [x.T for x in ks]
    ends = [[b_t[i][:, e:e + 1] for e in end_col] for i in seqs]
    ki_t, ke_t = [], []
    for i in seqs:
        bl_t = ends[i][nch - 1]
        for ci in range(nch - 2, -1, -1):
            bl_t = jnp.where(cc == ci, ends[i][ci], bl_t)
        ki_t.append((k_t[i] * jnp.exp(-b_t[i])).astype(BF16))
        ke_t.append(k_t[i] * jnp.exp(bl_t - b_t[i]))

    o_intra = [[None] * GLA_HEADS for _ in seqs]
    for h in heads:
        a = [_dot(jnp.where(lane == h, qd[i], 0.0).astype(BF16), ki_t[i]) for i in seqs]
        for i in seqs:
            am = jnp.where(m_intra, a[i], 0.0).astype(BF16)
            o_intra[i][h] = _dot(am, vs[i][:, h * GLA_DV:(h + 1) * GLA_DV])

    deltas = [[None] * GLA_HEADS for _ in seqs]
    for h in heads:
        for i in seqs:
            v_h = vs[i][:, h * GLA_DV:(h + 1) * GLA_DV]
            v_bd = jnp.concatenate(
                [jnp.concatenate([v_h[ci * GLA_CHUNK:(ci + 1) * GLA_CHUNK] if cj == ci else zero_v
                                  for cj in range(nch)], axis=1) for ci in range(nch)], axis=0)
            deltas[i][h] = _dot(ke_t[i][h * GLA_DK:(h + 1) * GLA_DK, :].astype(BF16), v_bd)

    s_h = [[ss[i][h * GLA_DK:(h + 1) * GLA_DK, :] for h in heads] for i in seqs]
    o_inter = [[None] * nch for _ in seqs]
    for ci in (range(nch - 1, -1, -1) if reverse else range(nch)):
        for i in seqs:
            s_bd = jnp.concatenate(
                [jnp.concatenate([s_h[i][h].astype(BF16) if hj == h else zero_s for hj in heads], axis=1)
                 for h in heads], axis=0)
            o_inter[i][ci] = _dot(qd[i][ci * GLA_CHUNK:(ci + 1) * GLA_CHUNK, :].astype(BF16), s_bd)
            decay = jnp.exp(ends[i][ci])
            s_h[i] = [decay[h * GLA_DK:(h + 1) * GLA_DK] * s_h[i][h]
                      + deltas[i][h][:, ci * GLA_DV:(ci + 1) * GLA_DV] for h in heads]
    outs = [jnp.concatenate(o_intra[i], axis=1) + jnp.concatenate(o_inter[i], axis=0) for i in seqs]
    return outs, [jnp.concatenate(s_h[i], axis=0) for i in seqs]


def _gla_kernel(q_ref, k_ref, v_ref, la_ref, og_ref, s0f_ref, s0b_ref, gh_ref,
                o_ref, sf_ref, sb_ref, s_ref, ob_ref, *, nt):
    p = pl.program_id(1)
    i = pl.program_id(2)
    grp = range(q_ref.shape[0])

    @pl.when(p == 0)
    def _backward():
        @pl.when(i == 0)
        def _():
            s_ref[...] = s0b_ref[...]

        start = pl.multiple_of((nt - 1 - i) * GLA_TILE, GLA_TILE)
        outs, states = _gla_tiles(*[[ref[gi] for gi in grp] for ref in (q_ref, k_ref, v_ref, la_ref, s_ref)], True)
        for gi in grp:
            s_ref[gi] = states[gi]
            ob_ref[gi, pl.ds(start, GLA_TILE), :] = outs[gi]

        @pl.when(i == nt - 1)
        def _():
            sb_ref[...] = s_ref[...]

    @pl.when(p == 1)
    def _forward():
        @pl.when(i == 0)
        def _():
            s_ref[...] = s0f_ref[...]

        start = pl.multiple_of(i * GLA_TILE, GLA_TILE)
        outs, states = _gla_tiles(*[[ref[gi] for gi in grp] for ref in (q_ref, k_ref, v_ref, la_ref, s_ref)], False)
        for gi in grp:
            s_ref[gi] = states[gi]
            o = outs[gi] + ob_ref[gi, pl.ds(start, GLA_TILE), :]
            for h in range(GLA_HEADS):
                sl = slice(h * GLA_DV, (h + 1) * GLA_DV)
                oh = o[:, sl]
                ms = jnp.mean(oh * oh, axis=-1, keepdims=True)
                y = oh * lax.rsqrt(ms + EPS) * gh_ref[...]
                o_ref[gi, :, sl] = (y * _silu(og_ref[gi, :, sl])).astype(BF16)

        @pl.when(i == nt - 1)
        def _():
            sf_ref[...] = s_ref[...]


def _gla(q, k, v, la, og, s0f, s0b, g_head, batch, seq):
    nt = seq // GLA_TILE
    gg = GLA_GROUP
    tile = lambda b, p, i: (b, p * i + (1 - p) * (nt - 1 - i), 0)
    fwd_tile = lambda b, p, i: (b, p * i, 0)
    st = pl.BlockSpec((gg, GLA_QK, GLA_DV), lambda b, p, i: (b, 0, 0))
    return pl.pallas_call(
        functools.partial(_gla_kernel, nt=nt),
        grid=(batch // gg, 2, nt),
        in_specs=[pl.BlockSpec((gg, GLA_TILE, GLA_QK), tile),
                  pl.BlockSpec((gg, GLA_TILE, GLA_QK), tile),
                  pl.BlockSpec((gg, GLA_TILE, GLA_V), tile),
                  pl.BlockSpec((gg, GLA_TILE, GLA_QK), lambda b, p, i: tile(b, p, i)[:2] + (1 - p,)),
                  pl.BlockSpec((gg, GLA_TILE, GLA_V), fwd_tile),
                  st, st, _resident(g_head.shape)],
        out_specs=[pl.BlockSpec((gg, GLA_TILE, GLA_V), fwd_tile), st, st],
        out_shape=[jax.ShapeDtypeStruct((batch, seq, GLA_V), BF16),
                   jax.ShapeDtypeStruct((batch, GLA_QK, GLA_DV), F32),
                   jax.ShapeDtypeStruct((batch, GLA_QK, GLA_DV), F32)],
        scratch_shapes=[pltpu.VMEM((gg, GLA_QK, GLA_DV), F32), pltpu.VMEM((gg, seq, GLA_V), F32)],
        compiler_params=_params(("arbitrary", "arbitrary", "arbitrary")),
        name="gla",
    )(q, k, v, la, og, s0f, s0b, g_head)


def _fnet_kernel(u_ref, ccs_ref, cs_ref, o_ref, ucs_ref, eo_ref, *, seq, scale):
    nseq = u_ref.shape[0]
    half = seq // 2
    blk = min(FN_ROWS, half)
    rows = min(FN_ROWS, seq)
    rr = lax.broadcasted_iota(jnp.int32, (blk, blk + FN_PAD), 0)
    cc = lax.broadcasted_iota(jnp.int32, (blk, blk + FN_PAD), 1)
    mirror = jnp.where(cc == blk - rr, 1.0, 0.0).astype(BF16)
    sign = 1.0 - 2.0 * (lax.broadcasted_iota(jnp.int32, (rows, 1), 0) % 2).astype(F32)
    for s in range(nseq):
        for part in range(2):
            ucs_ref[s, part, seq:seq + FN_PAD, :] = jnp.zeros((FN_PAD, FN_W), BF16)
        for gi in range(FN_GROUPS):
            sl = slice(gi * FN_CH, (gi + 1) * FN_CH)
            t = _dot(u_ref[s, :, sl], ccs_ref[...])
            ucs_ref[s, 0, 0:seq, sl] = t[:, :FN_CH].astype(BF16)
            ucs_ref[s, 1, 0:seq, sl] = t[:, FN_CH:].astype(BF16)
    for s in range(nseq):
        for j in range(half // blk):
            w0 = seq - (j + 1) * blk
            for part, sgn in ((0, 1.0), (1, -1.0)):
                own = ucs_ref[s, part, j * blk:(j + 1) * blk, :].astype(F32)
                partner = _dot(mirror, ucs_ref[s, part, w0:w0 + blk + FN_PAD, :])
                eo_ref[s, part * half + j * blk:part * half + (j + 1) * blk, :] = (own + sgn * partner).astype(BF16)
    for s in range(nseq):
        mid = ucs_ref[s, 0, half:half + 1, :].astype(F32)
        for j in range(seq // rows):
            rs = slice(j * rows, (j + 1) * rows)
            o_ref[s, rs, :] = ((_dot(cs_ref[rs, :], eo_ref[s]) + sign * mid) * scale).astype(BF16)


def _dft_tables(seq):
    def cs(n, cols):
        k = np.arange(n, dtype=np.int64)
        t = np.arange(cols, dtype=np.int64)
        ang = 2.0 * np.pi * ((k[:, None] * t[None, :]) % n).astype(np.float64) / n
        return np.cos(ang), np.sin(ang)
    ct, st = cs(seq, seq // 2)
    cc, sc = cs(FN_CH, FN_CH)
    to = lambda a: jnp.asarray(a.astype(np.float32)).astype(BF16)
    return to(np.concatenate([ct, -st], axis=1)), to(np.concatenate([cc, sc], axis=1))


def _fnet(u, batch, seq):
    cs, ccs = _dft_tables(seq)
    nseq = max(1, FN_TOKENS // seq)
    blk = pl.BlockSpec((nseq, seq, FN_W), lambda b: (b, 0, 0))
    return pl.pallas_call(
        functools.partial(_fnet_kernel, seq=seq, scale=float((seq * FN_CH) ** -0.5)),
        grid=(batch // nseq,),
        in_specs=[blk, _resident(ccs.shape), _resident(cs.shape)],
        out_specs=blk,
        out_shape=jax.ShapeDtypeStruct((batch, seq, FN_W), BF16),
        scratch_shapes=[pltpu.VMEM((nseq, 2, seq + FN_PAD, FN_W), BF16),
                        pltpu.VMEM((nseq, seq, FN_W), BF16)],
        compiler_params=_params(("arbitrary",)),
        name="fnet",
    )(u, ccs, cs)


def _rope128(x, cos, sin_signed, odd):
    back = pltpu.roll(x, 16, 1)
    fwd = pltpu.roll(x, 112, 1)
    return x * cos + jnp.where(odd, back, fwd) * sin_signed


def _l1_in_kernel(*refs, rope):
    if rope:
        x_ref, m_ref, g_ref, w_ref, cos_ref, sin_ref, q_ref, k_ref, v_ref = refs
        odd = (lax.broadcasted_iota(jnp.int32, (SUB_PROJ, LANES), 1) // 16) % 2 == 1
    else:
        x_ref, m_ref, g_ref, w_ref, q_ref, k_ref, v_ref = refs

    def project(h, rows, dst, col0, width, scale):
        for j in range(width // 256):
            y = _dot(h, w_ref[:, col0 + 256 * j:col0 + 256 * (j + 1)])
            if scale != 1.0:
                y = y * scale
            for half in range(2):
                yh = y[:, LANES * half:LANES * (half + 1)]
                if rope:
                    yh = _rope128(yh, cos_ref[rows, :], sin_ref[rows, :], odd)
                lo = 256 * j + LANES * half
                dst[rows, lo:lo + LANES] = yh.astype(dst.dtype)

    for r0 in range(0, x_ref.shape[0], SUB_PROJ):
        rows = slice(r0, r0 + SUB_PROJ)
        h = _ada_norm(x_ref[rows, :], g_ref[...], m_ref[0, 3:4, :], m_ref[0, 4:5, :]).astype(BF16)
        project(h, rows, q_ref, 0, SWA_Q, SWA_HD ** -0.5 * LOG2E)
        project(h, rows, k_ref, SWA_Q, SWA_KV, 1.0)
        v_ref[rows, :] = _dot(h, w_ref[:, SWA_Q + SWA_KV:SWA_Q + 2 * SWA_KV])


def _l1_in_proj(x, mods, bfn, g, w, rope_tabs=None, tiles_per_seq=None):
    m, d = x.shape
    row = lambda i: (i, 0)
    ins = [x, mods, g, w]
    specs = [pl.BlockSpec((TM, d), row),
             pl.BlockSpec((1, MOD_ROWS, d), lambda i: (bfn(i), 0, 0)),
             _resident(g.shape), _resident(w.shape)]
    if rope_tabs is not None:
        ins += list(rope_tabs)
        specs += [pl.BlockSpec((TM, LANES), lambda i: (i % tiles_per_seq, 0))] * 2
    return pl.pallas_call(
        functools.partial(_l1_in_kernel, rope=rope_tabs is not None),
        grid=(m // TM,),
        in_specs=specs,
        out_specs=[pl.BlockSpec((TM, SWA_Q), row), pl.BlockSpec((TM, SWA_KV), row), pl.BlockSpec((TM, SWA_KV), row)],
        out_shape=[jax.ShapeDtypeStruct((m, SWA_Q), BF16),
                   jax.ShapeDtypeStruct((m, SWA_KV), F32),
                   jax.ShapeDtypeStruct((m, SWA_KV), F32)],
        compiler_params=_params(("arbitrary",)),
        name="l1_in_proj",
    )(*ins)


def _rope_tables(seq):
    rows = seq // GRID_W
    row = np.repeat(np.arange(rows), GRID_W).astype(np.float64)
    col = (np.arange(rows * GRID_W) % GRID_W).astype(np.float64)
    n_freq = SWA_HD // 4
    inv = ROPE_BASE ** (-np.arange(n_freq, dtype=np.float64) / n_freq)
    ar = row[:, None] * inv
    ac = col[:, None] * inv
    ang = np.concatenate([ar, ar, ac, ac], axis=-1)
    sign = np.where((np.arange(SWA_HD) // n_freq) % 2 == 1, 1.0, -1.0)
    two = lambda t: jnp.asarray(np.concatenate([t, t], axis=-1).astype(np.float32))
    return two(np.cos(ang)), two(np.sin(ang) * sign)


def _attend(q_ref, ks, vs, biases, sink_ref, o_ref):
    kcat = ks[0] if len(ks) == 1 else jnp.concatenate(ks, axis=0)
    vcat = vs[0] if len(vs) == 1 else jnp.concatenate(vs, axis=0)
    n = kcat.shape[0]
    r = q_ref.shape[0]
    nblk = n // LANES
    low = lax.broadcasted_iota(jnp.int32, (n, LANES), 1) < SWA_HD
    ones_lo = jnp.where(low, 1.0, 0.0)
    ones_hi = jnp.where(low, 0.0, 1.0)
    low_out = lax.broadcasted_iota(jnp.int32, (2 * r, LANES), 1) < SWA_HD
    top = lax.broadcasted_iota(jnp.int32, (2 * r, 1), 0) < r
    biases = {t: jnp.concatenate([b, b], axis=0) for t, b in biases.items()}
    nt_dims = (((1,), (1,)), ((), ()))

    def scores(g):
        sl = slice(LANES * (g // 2), LANES * (g // 2 + 1))
        kcol, vcol = kcat[:, sl], vcat[:, sl]
        ksw, vsw = pltpu.roll(kcol, SWA_HD, 1), pltpu.roll(vcol, SWA_HD, 1)
        if g % 2 == 0:
            k_lo, k_hi = jnp.where(low, kcol, 0.0), jnp.where(low, 0.0, ksw)
            v_lo, v_hi = jnp.where(low, vcol, 0.0), jnp.where(low, 0.0, vsw)
        else:
            k_lo, k_hi = jnp.where(low, ksw, 0.0), jnp.where(low, 0.0, kcol)
            v_lo, v_hi = jnp.where(low, vsw, 0.0), jnp.where(low, 0.0, vcol)
        k_bd = jnp.concatenate([k_lo, k_hi], axis=0).astype(BF16)
        v_bd = jnp.concatenate([jnp.concatenate([v_lo, ones_lo], axis=1),
                                jnp.concatenate([v_hi, ones_hi], axis=1)], axis=0).astype(BF16)
        qs = jnp.concatenate([q_ref[:, LANES * (2 * g):LANES * (2 * g + 1)],
                              q_ref[:, LANES * (2 * g + 1):LANES * (2 * g + 2)]], axis=0)
        return lax.dot_general(qs, k_bd, nt_dims, preferred_element_type=F32), v_bd

    nxt = scores(0)
    for g in range(SWA_KV_HEADS):
        s, v_bd = nxt
        if g + 1 < SWA_KV_HEADS:
            nxt = scores(g + 1)
        probs, sink_terms = [], []
        for e in range(2):
            blocks = [s[:, e * n + LANES * t:e * n + LANES * (t + 1)] for t in range(nblk)]
            for t, b in biases.items():
                blocks[t] = blocks[t] + b
            mx = blocks[0]
            for b in blocks[1:]:
                mx = jnp.maximum(mx, b)
            sk = jnp.where(top, sink_ref[4 * g + e], sink_ref[4 * g + 2 + e]) * LOG2E
            m = jnp.maximum(jnp.max(mx, axis=-1, keepdims=True), sk)
            probs += [jnp.exp2(b - m).astype(BF16) for b in blocks]
            sink_terms.append(jnp.exp2(sk - m))
        res = _dot(jnp.concatenate(probs, axis=1), v_bd)
        denom = res[:, LANES:2 * LANES] + jnp.where(low_out, sink_terms[0], sink_terms[1])
        out = (res[:, 0:LANES] / denom).astype(o_ref.dtype)
        o_ref[:, LANES * (2 * g):LANES * (2 * g + 1)] = out[0:r]
        o_ref[:, LANES * (2 * g + 1):LANES * (2 * g + 2)] = out[r:2 * r]


def _ctx_attn_kernel(sink_ref, q_ref, k_ref, v_ref, o_ref):
    _attend(q_ref, [k_ref[...]], [v_ref[...]], {}, sink_ref, o_ref)


def _ctx_attention(q, k, v, sink, batch, seq):
    blk = lambda w: pl.BlockSpec((seq, w), lambda b: (b, 0))
    return pl.pallas_call(
        _ctx_attn_kernel,
        grid=(batch,),
        in_specs=[pl.BlockSpec(memory_space=pltpu.SMEM), blk(SWA_Q), blk(SWA_KV), blk(SWA_KV)],
        out_specs=blk(SWA_Q),
        out_shape=jax.ShapeDtypeStruct((batch * seq, SWA_Q), BF16),
        compiler_params=_params(("arbitrary",)),
        name="ctx_attention",
    )(sink, q, k, v)


def _lat_attn_kernel(sink_ref, q_ref, ck_ref, cv_ref, kp_ref, kc_ref, kn_ref, vp_ref, vc_ref, vn_ref, o_ref,
                     *, nb, n_ctx):
    i = pl.program_id(1)
    blk = ATTN_BLOCK
    row = lax.broadcasted_iota(jnp.int32, (blk, blk), 0)
    col = lax.broadcasted_iota(jnp.int32, (blk, blk), 1)
    off_prev = jnp.where(i > 0, 0, blk)
    off_next = jnp.where(i < nb - 1, 0, blk)
    bias_prev = jnp.where(col - row >= off_prev, 0.0, -jnp.inf)
    bias_next = jnp.where(row - col >= off_next, 0.0, -jnp.inf)
    first = n_ctx // LANES
    _attend(q_ref, [ck_ref[...], kp_ref[...], kc_ref[...], kn_ref[...]],
            [cv_ref[...], vp_ref[...], vc_ref[...], vn_ref[...]],
            {first: bias_prev, first + 2: bias_next}, sink_ref, o_ref)


def _lat_attention(q, k, v, ctx_k, ctx_v, sink, batch, seq, n_ctx):
    nb = seq // ATTN_BLOCK
    kv = lambda f: pl.BlockSpec((ATTN_BLOCK, SWA_KV), lambda b, i: (b * nb + f(i), 0))
    prev, cur, nxt = kv(lambda i: jnp.maximum(i - 1, 0)), kv(lambda i: i), kv(lambda i: jnp.minimum(i + 1, nb - 1))
    ctx = pl.BlockSpec((n_ctx, SWA_KV), lambda b, i: (b, 0))
    qo = pl.BlockSpec((ATTN_BLOCK, SWA_Q), lambda b, i: (b * nb + i, 0))
    return pl.pallas_call(
        functools.partial(_lat_attn_kernel, nb=nb, n_ctx=n_ctx),
        grid=(batch, nb),
        in_specs=[pl.BlockSpec(memory_space=pltpu.SMEM), qo, ctx, ctx, prev, cur, nxt, prev, cur, nxt],
        out_specs=qo,
        out_shape=jax.ShapeDtypeStruct((batch * seq, SWA_Q), BF16),
        compiler_params=_params(("arbitrary", "arbitrary")),
        name="lat_attention",
    )(sink, q, ctx_k, ctx_v, k, k, k, v, v, v)


def kernel(x_prompt, x_sample, state_l0_gla_fwd, state_l0_gla_bwd, cache_l1_k, cache_l1_v, c, c_ctx, mod_w, mod_b, norm_g, ffn_w1, ffn_w3, ffn_w2, l0_w_in, l0_w_gf, l0_b_gf, l0_w_gb, l0_b_gb, l0_g_head, l0_w_out, l1_w_in, l1_sink, l1_w_out, final_g):
    bp, tp, d = x_prompt.shape
    bs, ts, _ = x_sample.shape
    n_ctx = cache_l1_k.shape[1]
    depth = mod_w.shape[0]
    ctx_row = bs

    ffn_stacks = (ffn_w1, ffn_w3, ffn_w2)
    ffn_w = tuple(s[0, 0].astype(BF16) for s in ffn_stacks)
    gate_lo = 2 * GLA_QK + 2 * GLA_V
    gate_hi = gate_lo + 2 * GLA_GATE_RANK
    l0_wm = jnp.concatenate([l0_w_in[:, :gate_lo], l0_w_in[:, gate_hi:]], axis=1).astype(BF16)
    l0_wg = jnp.pad(l0_w_in[:, gate_lo:gate_hi], ((0, 0), (0, LANES - 2 * GLA_GATE_RANK))).astype(BF16)
    wg2 = jnp.zeros((LANES, 2 * GLA_QK), F32)
    wg2 = wg2.at[:GLA_GATE_RANK, :GLA_QK].set(l0_w_gf)
    wg2 = wg2.at[GLA_GATE_RANK:2 * GLA_GATE_RANK, GLA_QK:].set(l0_w_gb).astype(BF16)
    bg = jnp.concatenate([l0_b_gf, l0_b_gb])[None, :]
    g_head = l0_g_head[None, :]
    sink = l1_sink.reshape(-1)
    fin_g = final_g[None, :]

    cond = jnp.zeros((MOD_ROWS, d), F32).at[:bs].set(c).at[ctx_row].set(c_ctx)
    mods = _modulation(cond, mod_w, mod_b)
    mods = mods.reshape(depth, MOD_ROWS, N_MOD, d)
    mods = jnp.pad(mods, ((0, 0), (0, 0), (0, MOD_ROWS - N_MOD), (0, 0)))

    xp = x_prompt.reshape(bp * tp, d)
    xs = x_sample.reshape(bs * ts, d)
    tiles_s = ts // TM
    bfn_p = lambda i: ctx_row
    bfn_s = lambda i: i // tiles_s
    zero_state = jnp.zeros((bp, GLA_QK, GLA_DV), F32)
    s0f = state_l0_gla_fwd.reshape(bs, GLA_QK, GLA_DV)
    s0b = state_l0_gla_bwd.reshape(bs, GLA_QK, GLA_DV)
    rope_tabs = _rope_tables(ts)

    new_state = []
    for layer in range(depth):
        ml = mods[layer]
        ga, gm, gb = (norm_g[layer, r][None, :] for r in range(3))
        last = layer == depth - 1
        xp, proj_w = _half_ffn(xp, ml, bfn_p, ga, *ffn_w, mrow=0,
                               cast=[(w, ()) for w in (l0_w_out, l1_w_in, l1_w_out)] if layer == 0 else None)
        if layer == 0:
            l0_wo, l1_wi, l1_wo = proj_w
        xs, ffn_w_b = _half_ffn(xs, ml, bfn_s, ga, *ffn_w, mrow=0, cast=[(s, (layer, 1)) for s in ffn_stacks])
        if layer % 2 == 0:
            outs = []
            for x, bfn, nb, t, sf0, sb0 in ((xp, bfn_p, bp, tp, zero_state, zero_state), (xs, bfn_s, bs, ts, s0f, s0b)):
                q, k, v, og, la, u = _l0_in_proj(x, ml, bfn, gm, l0_wm, l0_wg, wg2, bg)
                seq3 = lambda a: a.reshape(nb, t, a.shape[-1])
                gla_out, s_f, s_b = _gla(seq3(q), seq3(k), seq3(v), seq3(la), seq3(og), sf0, sb0, g_head, nb, t)
                gla_out = gla_out.reshape(nb * t, GLA_V)
                fn_out = _fnet(seq3(u), nb, t).reshape(nb * t, FN_W)
                outs.append(((gla_out, fn_out), s_f, s_b))
            (mix_p, s_f, s_b), (mix_s, _, _) = outs
            new_state += [s_f.reshape(bp, GLA_HEADS, GLA_DK, GLA_DV), s_b.reshape(bp, GLA_HEADS, GLA_DK, GLA_DV)]
            w_out = l0_wo
        else:
            qp, kp, vp = _l1_in_proj(xp, ml, bfn_p, gm, l1_wi)
            mix_p = (_ctx_attention(qp, kp, vp, sink, bp, tp),)
            qs, ks, vs = _l1_in_proj(xs, ml, bfn_s, gm, l1_wi, rope_tabs, tiles_s)
            mix_s = (_lat_attention(qs, ks, vs, cache_l1_k.reshape(bs * n_ctx, SWA_KV),
                                    cache_l1_v.reshape(bs * n_ctx, SWA_KV), sink, bs, ts, n_ctx),)
            new_state += [kp.reshape(bp, tp, SWA_KV_HEADS, SWA_HD), vp.reshape(bp, tp, SWA_KV_HEADS, SWA_HD)]
            w_out = l1_wo
        fin = fin_g if last else None
        xp, _ = _half_ffn(xp, ml, bfn_p, gb, *ffn_w_b, mrow=6, mixes=mix_p, w_out=w_out, final_g=fin)
        xs, ffn_w = _half_ffn(xs, ml, bfn_s, gb, *ffn_w_b, mrow=6, mixes=mix_s, w_out=w_out, final_g=fin,
                              cast=None if last else [(s, (layer + 1, 0)) for s in ffn_stacks])
    return (xp.reshape(bp, tp, d), xs.reshape(bs, ts, d), *new_state)
```

```python
import functools

import numpy as np
import jax
import jax.numpy as jnp
from jax import lax
from jax.experimental import pallas as pl
from jax.experimental.pallas import tpu as pltpu

F32 = jnp.float32
BF16 = jnp.bfloat16

D_MODEL = 1024
FFN_DIM = 2816
N_MOD = 9
EPS = 1e-6
GRID_W = 64
GLA_HEADS = 4
GLA_DK = 64
GLA_DV = 128
GLA_GATE_RANK = 16
GLA_TAU = 16.0
GLA_CHUNK = 64
GLA_QK = GLA_HEADS * GLA_DK
GLA_V = GLA_HEADS * GLA_DV
FN_GROUPS = 4
FN_CH = 128
FN_W = FN_GROUPS * FN_CH
SWA_HEADS = 16
SWA_KV_HEADS = 4
SWA_HD = 64
SWA_Q = SWA_HEADS * SWA_HD
SWA_KV = SWA_KV_HEADS * SWA_HD
WINDOW = 128
ATTN_BLOCK = 128
ROPE_BASE = 10000.0
LOG2E = float(np.log2(np.e))

LANES = 128
VMEM_LIMIT = 56 * 1024 * 1024

TM = 1024
SUB = 512
SUB_PROJ = 256
TF = 256
GLA_TILE = 256
GLA_GROUP = 4
FN_ROWS = 256
FN_TOKENS = 2048
FN_PAD = 16
FN_FOLD = 128
MOD_ROWS = 16
MOD_TN = 1024


def _params(sem):
    return pltpu.CompilerParams(dimension_semantics=sem, vmem_limit_bytes=VMEM_LIMIT)


def _resident(shape):
    nd = len(shape)
    return pl.BlockSpec(shape, lambda *_: (0,) * nd, pipeline_mode=pl.Buffered(1))


def _dot(a, b):
    return jnp.dot(a, b, preferred_element_type=F32)


def _silu(x):
    return x * jax.nn.sigmoid(x)


def _ada_norm(x, g, shift, scale):
    ms = jnp.mean(x * x, axis=-1, keepdims=True)
    return (x * lax.rsqrt(ms + EPS)) * (g * (1.0 + scale)) + shift


def _mod_kernel(c_ref, w_ref, b_ref, o_ref):
    c = c_ref[...]
    s = _silu(c).astype(BF16)
    o_ref[0] = _dot(s, w_ref[0].astype(BF16)) + b_ref[0]


def _modulation(cond, mod_w, mod_b):
    depth, d, n = mod_w.shape
    return pl.pallas_call(
        _mod_kernel,
        grid=(depth, n // MOD_TN),
        in_specs=[
            pl.BlockSpec((MOD_ROWS, d), lambda l, j: (0, 0)),
            pl.BlockSpec((1, d, MOD_TN), lambda l, j: (l, 0, j)),
            pl.BlockSpec((1, 1, MOD_TN), lambda l, j: (l, 0, j)),
        ],
        out_specs=pl.BlockSpec((1, MOD_ROWS, MOD_TN), lambda l, j: (l, 0, j)),
        out_shape=jax.ShapeDtypeStruct((depth, MOD_ROWS, n), F32),
        compiler_params=_params(("arbitrary", "arbitrary")),
        name="modulation",
    )(cond, mod_w, mod_b.reshape(depth, 1, n))


def _ffn_kernel(*refs, n_mix, mrow, final, n_cast):
    x_ref = refs[0]
    mix_refs = refs[1:1 + n_mix]
    pos = 1 + n_mix
    if n_mix:
        wo_ref = refs[pos]
        pos += 1
    m_ref, g_ref, w1_ref, w3_ref, w2_ref = refs[pos:pos + 5]
    pos += 5
    if final:
        fg_ref = refs[pos]
        pos += 1
    cast_in = refs[pos:pos + n_cast]
    pos += n_cast
    o_ref = refs[pos]
    cast_out = refs[pos + 1:pos + 1 + n_cast]
    pos += 1 + n_cast
    acc_ref, h_ref = refs[pos], refs[pos + 1]
    for src, dst in zip(cast_in, cast_out):
        dst[...] = src[...].astype(BF16)

    def prologue(rows):
        x = x_ref[rows, :]
        if n_mix:
            off = 0
            mix = None
            for r in mix_refs:
                w = r.shape[1]
                t = _dot(r[rows, :], wo_ref[off:off + w, :])
                mix = t if mix is None else mix + t
                off += w
            x = x + m_ref[0, 5:6, :] * mix
        o_ref[rows, :] = x
        h = _ada_norm(x, g_ref[...], m_ref[0, mrow:mrow + 1, :], m_ref[0, mrow + 1:mrow + 2, :])
        h_ref[rows, :] = h.astype(BF16)

    def chunk(rows, j):
        h = h_ref[rows, :]
        a = _dot(h, w1_ref[:, j * TF:(j + 1) * TF])
        b = _dot(h, w3_ref[:, j * TF:(j + 1) * TF])
        t = _dot((_silu(a) * b).astype(BF16), w2_ref[j * TF:(j + 1) * TF, :])
        if j == 0:
            acc_ref[rows, :] = t
        else:
            acc_ref[rows, :] += t

    def epilogue(rows):
        y = o_ref[rows, :] + (0.5 * m_ref[0, mrow + 2:mrow + 3, :]) * acc_ref[rows, :]
        if final:
            ms = jnp.mean(y * y, axis=-1, keepdims=True)
            y = y * lax.rsqrt(ms + EPS) * fg_ref[...]
        o_ref[rows, :] = y

    subs = [slice(r0, r0 + SUB) for r0 in range(0, x_ref.shape[0], SUB)]
    for rows in subs:
        prologue(rows)
    for j in range(FFN_DIM // TF):
        for rows in subs:
            chunk(rows, j)
    for rows in subs:
        epilogue(rows)


def _half_ffn(x, mods, bfn, g, w1, w3, w2, *, mrow, mixes=(), w_out=None, final_g=None, cast=None):
    m, d = x.shape
    steps = m // TM
    row = lambda i: (i, 0)
    ins = [x]
    specs = [pl.BlockSpec((TM, d), row)]
    for a in mixes:
        ins.append(a)
        specs.append(pl.BlockSpec((TM, a.shape[1]), row))
    if mixes:
        ins.append(w_out)
        specs.append(_resident(w_out.shape))
    ins += [mods, g, w1, w3, w2]
    specs += [pl.BlockSpec((1, MOD_ROWS, d), lambda i: (bfn(i), 0, 0)),
              _resident(g.shape), _resident(w1.shape), _resident(w3.shape), _resident(w2.shape)]
    if final_g is not None:
        ins.append(final_g)
        specs.append(_resident(final_g.shape))
    out_specs = [pl.BlockSpec((TM, d), row)]
    out_shape = [jax.ShapeDtypeStruct((m, d), F32)]
    cast = tuple(cast or ())
    for s, lead in cast:
        rows, cols = s.shape[-2] // steps, s.shape[-1]
        ins.append(s)
        specs.append(pl.BlockSpec((None,) * len(lead) + (rows, cols), lambda i, lead=tuple(lead): lead + (i, 0)))
        out_specs.append(pl.BlockSpec((rows, cols), row))
        out_shape.append(jax.ShapeDtypeStruct(s.shape[-2:], BF16))
    outs = pl.pallas_call(
        functools.partial(_ffn_kernel, n_mix=len(mixes), mrow=mrow, final=final_g is not None,
                          n_cast=len(cast)),
        grid=(steps,),
        in_specs=specs,
        out_specs=out_specs,
        out_shape=out_shape,
        scratch_shapes=[pltpu.VMEM((TM, d), F32), pltpu.VMEM((TM, d), BF16)],
        compiler_params=_params(("arbitrary",)),
        name="half_ffn",
    )(*ins)
    return outs[0], tuple(outs[1:])


def _l0_in_kernel(x_ref, m_ref, g_ref, wm_ref, wg_ref, wg2_ref, bg_ref,
                  q_ref, k_ref, v_ref, og_ref, la_ref, u_ref):
    for r0 in range(0, x_ref.shape[0], SUB_PROJ):
        rows = slice(r0, r0 + SUB_PROJ)
        h = _ada_norm(x_ref[rows, :], g_ref[...], m_ref[0, 3:4, :], m_ref[0, 4:5, :]).astype(BF16)
        q_ref[rows, :] = _dot(h, wm_ref[:, 0:256]) * (GLA_DK ** -0.5)
        lr = _dot(h, wg_ref[...]).astype(BF16)
        k_ref[rows, :] = _dot(h, wm_ref[:, 256:512])
        z = _dot(lr, wg2_ref[...]) + bg_ref[...]
        log_sig = jnp.minimum(z, 0.0) - jnp.log(1.0 + jnp.exp(-jnp.abs(z)))
        la_ref[rows, :] = log_sig * (1.0 / GLA_TAU)
        v_ref[rows, :] = _dot(h, wm_ref[:, 512:1024]).astype(BF16)
        og_ref[rows, :] = _dot(h, wm_ref[:, 1024:1536])
        u_ref[rows, :] = _dot(h, wm_ref[:, 1536:2048]).astype(BF16)


def _l0_in_proj(x, mods, bfn, g, wm, wg, wg2, bg):
    m, d = x.shape
    row = lambda i: (i, 0)
    outs = [(GLA_QK, F32), (GLA_QK, F32), (GLA_V, BF16), (GLA_V, F32), (2 * GLA_QK, F32), (FN_W, BF16)]
    return pl.pallas_call(
        _l0_in_kernel,
        grid=(m // TM,),
        in_specs=[pl.BlockSpec((TM, d), row),
                  pl.BlockSpec((1, MOD_ROWS, d), lambda i: (bfn(i), 0, 0)),
                  _resident(g.shape), _resident(wm.shape), _resident(wg.shape),
                  _resident(wg2.shape), _resident(bg.shape)],
        out_specs=[pl.BlockSpec((TM, w), row) for w, _ in outs],
        out_shape=[jax.ShapeDtypeStruct((m, w), dt) for w, dt in outs],
        compiler_params=_params(("arbitrary",)),
        name="l0_in_proj",
    )(x, mods, g, wm, wg, wg2, bg)


def _gla_tiles(qs, ks, vs, gs, ss, reverse):
    tt = GLA_TILE
    nch = tt // GLA_CHUNK
    seqs = range(len(qs))
    heads = range(GLA_HEADS)
    r = lax.broadcasted_iota(jnp.int32, (tt, tt), 0)
    c = lax.broadcasted_iota(jnp.int32, (tt, tt), 1)
    same = (r // GLA_CHUNK) == (c // GLA_CHUNK)
    m_intra = jnp.logical_and(same, (c >= r) if reverse else (c <= r))
    tri = jnp.where(m_intra, 1.0, 0.0).astype(BF16)
    cc = lax.broadcasted_iota(jnp.int32, (GLA_QK, tt), 1) // GLA_CHUNK
    lane = lax.broadcasted_iota(jnp.int32, (tt, GLA_QK), 1) // GLA_DK
    end_col = [ci * GLA_CHUNK + (0 if reverse else GLA_CHUNK - 1) for ci in range(nch)]
    zero_v = jnp.zeros((GLA_CHUNK, GLA_DV), BF16)
    zero_s = jnp.zeros((GLA_DK, GLA_DV), BF16)

    g_hi = [g.astype(BF16) for g in gs]
    g_lo = [(gs[i] - g_hi[i].astype(F32)).astype(BF16) for i in seqs]
    b = [_dot(tri, g_hi[i]) + _dot(tri, g_lo[i]) for i in seqs]
    qd = [qs[i] * jnp.exp(b[i]) for i in seqs]
    b_t = [x.T for x in b]
    k_t = [x.T for x in ks]
    ends = [[b_t[i][:, e:e + 1] for e in end_col] for i in seqs]
    ki_t, ke_t = [], []
    for i in seqs:
        bl_t = ends[i][nch - 1]
        for ci in range(nch - 2, -1, -1):
            bl_t = jnp.where(cc == ci, ends[i][ci], bl_t)
        ki_t.append((k_t[i] * jnp.exp(-b_t[i])).astype(BF16))
        ke_t.append(k_t[i] * jnp.exp(bl_t - b_t[i]))

    o_intra = [[None] * GLA_HEADS for _ in seqs]
    for h in heads:
        a = [_dot(jnp.where(lane == h, qd[i], 0.0).astype(BF16), ki_t[i]) for i in seqs]
        for i in seqs:
            am = jnp.where(m_intra, a[i], 0.0).astype(BF16)
            o_intra[i][h] = _dot(am, vs[i][:, h * GLA_DV:(h + 1) * GLA_DV])

    deltas = [[None] * GLA_HEADS for _ in seqs]
    for h in heads:
        for i in seqs:
            v_h = vs[i][:, h * GLA_DV:(h + 1) * GLA_DV]
            v_bd = jnp.concatenate(
                [jnp.concatenate([v_h[ci * GLA_CHUNK:(ci + 1) * GLA_CHUNK] if cj == ci else zero_v
                                  for cj in range(nch)], axis=1) for ci in range(nch)], axis=0)
            deltas[i][h] = _dot(ke_t[i][h * GLA_DK:(h + 1) * GLA_DK, :].astype(BF16), v_bd)

    s_h = [[ss[i][h * GLA_DK:(h + 1) * GLA_DK, :] for h in heads] for i in seqs]
    o_inter = [[None] * nch for _ in seqs]
    for ci in (range(nch - 1, -1, -1) if reverse else range(nch)):
        for i in seqs:
            s_bd = jnp.concatenate(
                [jnp.concatenate([s_h[i][h].astype(BF16) if hj == h else zero_s for hj in heads], axis=1)
                 for h in heads], axis=0)
            o_inter[i][ci] = _dot(qd[i][ci * GLA_CHUNK:(ci + 1) * GLA_CHUNK, :].astype(BF16), s_bd)
            decay = jnp.exp(ends[i][ci])
            s_h[i] = [decay[h * GLA_DK:(h + 1) * GLA_DK] * s_h[i][h]
                      + deltas[i][h][:, ci * GLA_DV:(ci + 1) * GLA_DV] for h in heads]
    outs = [jnp.concatenate(o_intra[i], axis=1) + jnp.concatenate(o_inter[i], axis=0) for i in seqs]
    return outs, [jnp.concatenate(s_h[i], axis=0) for i in seqs]


def _gla_kernel(q_ref, k_ref, v_ref, la_ref, og_ref, s0f_ref, s0b_ref, gh_ref,
                o_ref, sf_ref, sb_ref, s_ref, ob_ref, *, nt):
    p = pl.program_id(1)
    i = pl.program_id(2)
    grp = range(q_ref.shape[0])

    @pl.when(p == 0)
    def _backward():
        @pl.when(i == 0)
        def _():
            s_ref[...] = s0b_ref[...]

        start = pl.multiple_of((nt - 1 - i) * GLA_TILE, GLA_TILE)
        outs, states = _gla_tiles(*[[ref[gi] for gi in grp] for ref in (q_ref, k_ref, v_ref, la_ref, s_ref)], True)
        for gi in grp:
            s_ref[gi] = states[gi]
            ob_ref[gi, pl.ds(start, GLA_TILE), :] = outs[gi]

        @pl.when(i == nt - 1)
        def _():
            sb_ref[...] = s_ref[...]

    @pl.when(p == 1)
    def _forward():
        @pl.when(i == 0)
        def _():
            s_ref[...] = s0f_ref[...]

        start = pl.multiple_of(i * GLA_TILE, GLA_TILE)
        outs, states = _gla_tiles(*[[ref[gi] for gi in grp] for ref in (q_ref, k_ref, v_ref, la_ref, s_ref)], False)
        for gi in grp:
            s_ref[gi] = states[gi]
            o = outs[gi] + ob_ref[gi, pl.ds(start, GLA_TILE), :]
            for h in range(GLA_HEADS):
                sl = slice(h * GLA_DV, (h + 1) * GLA_DV)
                oh = o[:, sl]
                ms = jnp.mean(oh * oh, axis=-1, keepdims=True)
                y = oh * lax.rsqrt(ms + EPS) * gh_ref[...]
                o_ref[gi, :, sl] = (y * _silu(og_ref[gi, :, sl])).astype(BF16)

        @pl.when(i == nt - 1)
        def _():
            sf_ref[...] = s_ref[...]


def _gla(q, k, v, la, og, s0f, s0b, g_head, batch, seq):
    nt = seq // GLA_TILE
    gg = GLA_GROUP
    tile = lambda b, p, i: (b, p * i + (1 - p) * (nt - 1 - i), 0)
    fwd_tile = lambda b, p, i: (b, p * i, 0)
    st = pl.BlockSpec((gg, GLA_QK, GLA_DV), lambda b, p, i: (b, 0, 0))
    return pl.pallas_call(
        functools.partial(_gla_kernel, nt=nt),
        grid=(batch // gg, 2, nt),
        in_specs=[pl.BlockSpec((gg, GLA_TILE, GLA_QK), tile),
                  pl.BlockSpec((gg, GLA_TILE, GLA_QK), tile),
                  pl.BlockSpec((gg, GLA_TILE, GLA_V), tile),
                  pl.BlockSpec((gg, GLA_TILE, GLA_QK), lambda b, p, i: tile(b, p, i)[:2] + (1 - p,)),
                  pl.BlockSpec((gg, GLA_TILE, GLA_V), fwd_tile),
                  st, st, _resident(g_head.shape)],
        out_specs=[pl.BlockSpec((gg, GLA_TILE, GLA_V), fwd_tile), st, st],
        out_shape=[jax.ShapeDtypeStruct((batch, seq, GLA_V), BF16),
                   jax.ShapeDtypeStruct((batch, GLA_QK, GLA_DV), F32),
                   jax.ShapeDtypeStruct((batch, GLA_QK, GLA_DV), F32)],
        scratch_shapes=[pltpu.VMEM((gg, GLA_QK, GLA_DV), F32), pltpu.VMEM((gg, seq, GLA_V), F32)],
        compiler_params=_params(("arbitrary", "arbitrary", "arbitrary")),
        name="gla",
    )(q, k, v, la, og, s0f, s0b, g_head)


def _fnet_kernel(u_ref, ccs_ref, cs_ref, o_ref, ucs_ref, eo_ref, *, seq, scale):
    nseq = u_ref.shape[0]
    half = seq // 2
    blk = min(FN_FOLD, half)
    rows = min(FN_ROWS, seq)
    rr = lax.broadcasted_iota(jnp.int32, (blk, blk + FN_PAD), 0)
    cc = lax.broadcasted_iota(jnp.int32, (blk, blk + FN_PAD), 1)
    mirror = jnp.where(cc == blk - rr, 1.0, 0.0).astype(BF16)
    sign = 1.0 - 2.0 * (lax.broadcasted_iota(jnp.int32, (rows, 1), 0) % 2).astype(F32)
    for s in range(nseq):
        for part in range(2):
            ucs_ref[s, part, seq:seq + FN_PAD, :] = jnp.zeros((FN_PAD, FN_W), BF16)
        for gi in range(FN_GROUPS):
            sl = slice(gi * FN_CH, (gi + 1) * FN_CH)
            t = _dot(u_ref[s, :, sl], ccs_ref[...])
            ucs_ref[s, 0, 0:seq, sl] = t[:, :FN_CH].astype(BF16)
            ucs_ref[s, 1, 0:seq, sl] = t[:, FN_CH:].astype(BF16)
    for s in range(nseq):
        for j in range(half // blk):
            w0 = seq - (j + 1) * blk
            for part, sgn in ((0, 1.0), (1, -1.0)):
                own = ucs_ref[s, part, j * blk:(j + 1) * blk, :].astype(F32)
                partner = _dot(mirror, ucs_ref[s, part, w0:w0 + blk + FN_PAD, :])
                eo_ref[s, part * half + j * blk:part * half + (j + 1) * blk, :] = (own + sgn * partner).astype(BF16)
    for s in range(nseq):
        mid = ucs_ref[s, 0, half:half + 1, :].astype(F32)
        for j in range(seq // rows):
            rs = slice(j * rows, (j + 1) * rows)
            o_ref[s, rs, :] = ((_dot(cs_ref[rs, :], eo_ref[s]) + sign * mid) * scale).astype(BF16)


def _dft_tables(seq):
    def cs(n, cols):
        k = np.arange(n, dtype=np.int64)
        t = np.arange(cols, dtype=np.int64)
        ang = 2.0 * np.pi * ((k[:, None] * t[None, :]) % n).astype(np.float64) / n
        return np.cos(ang), np.sin(ang)
    ct, st = cs(seq, seq // 2)
    cc, sc = cs(FN_CH, FN_CH)
    to = lambda a: jnp.asarray(a.astype(np.float32)).astype(BF16)
    return to(np.concatenate([ct, -st], axis=1)), to(np.concatenate([cc, sc], axis=1))


def _fnet(u, batch, seq):
    cs, ccs = _dft_tables(seq)
    nseq = max(1, FN_TOKENS // seq)
    blk = pl.BlockSpec((nseq, seq, FN_W), lambda b: (b, 0, 0))
    return pl.pallas_call(
        functools.partial(_fnet_kernel, seq=seq, scale=float((seq * FN_CH) ** -0.5)),
        grid=(batch // nseq,),
        in_specs=[blk, _resident(ccs.shape), _resident(cs.shape)],
        out_specs=blk,
        out_shape=jax.ShapeDtypeStruct((batch, seq, FN_W), BF16),
        scratch_shapes=[pltpu.VMEM((nseq, 2, seq + FN_PAD, FN_W), BF16),
                        pltpu.VMEM((nseq, seq, FN_W), BF16)],
        compiler_params=_params(("arbitrary",)),
        name="fnet",
    )(u, ccs, cs)


def _rope128(x, cos, sin_signed, odd):
    back = pltpu.roll(x, 16, 1)
    fwd = pltpu.roll(x, 112, 1)
    return x * cos + jnp.where(odd, back, fwd) * sin_signed


def _l1_in_kernel(*refs, rope):
    if rope:
        x_ref, m_ref, g_ref, w_ref, cos_ref, sin_ref, q_ref, k_ref, v_ref = refs
        odd = (lax.broadcasted_iota(jnp.int32, (SUB_PROJ, LANES), 1) // 16) % 2 == 1
    else:
        x_ref, m_ref, g_ref, w_ref, q_ref, k_ref, v_ref = refs

    def project(h, rows, dst, col0, width, scale):
        for j in range(width // 256):
            y = _dot(h, w_ref[:, col0 + 256 * j:col0 + 256 * (j + 1)])
            if scale != 1.0:
                y = y * scale
            for half in range(2):
                yh = y[:, LANES * half:LANES * (half + 1)]
                if rope:
                    yh = _rope128(yh, cos_ref[rows, :], sin_ref[rows, :], odd)
                lo = 256 * j + LANES * half
                dst[rows, lo:lo + LANES] = yh.astype(dst.dtype)

    for r0 in range(0, x_ref.shape[0], SUB_PROJ):
        rows = slice(r0, r0 + SUB_PROJ)
        h = _ada_norm(x_ref[rows, :], g_ref[...], m_ref[0, 3:4, :], m_ref[0, 4:5, :]).astype(BF16)
        project(h, rows, q_ref, 0, SWA_Q, SWA_HD ** -0.5 * LOG2E)
        project(h, rows, k_ref, SWA_Q, SWA_KV, 1.0)
        v_ref[rows, :] = _dot(h, w_ref[:, SWA_Q + SWA_KV:SWA_Q + 2 * SWA_KV])


def _l1_in_proj(x, mods, bfn, g, w, rope_tabs=None, tiles_per_seq=None):
    m, d = x.shape
    row = lambda i: (i, 0)
    ins = [x, mods, g, w]
    specs = [pl.BlockSpec((TM, d), row),
             pl.BlockSpec((1, MOD_ROWS, d), lambda i: (bfn(i), 0, 0)),
             _resident(g.shape), _resident(w.shape)]
    if rope_tabs is not None:
        ins += list(rope_tabs)
        specs += [pl.BlockSpec((TM, LANES), lambda i: (i % tiles_per_seq, 0))] * 2
    return pl.pallas_call(
        functools.partial(_l1_in_kernel, rope=rope_tabs is not None),
        grid=(m // TM,),
        in_specs=specs,
        out_specs=[pl.BlockSpec((TM, SWA_Q), row), pl.BlockSpec((TM, SWA_KV), row), pl.BlockSpec((TM, SWA_KV), row)],
        out_shape=[jax.ShapeDtypeStruct((m, SWA_Q), BF16),
                   jax.ShapeDtypeStruct((m, SWA_KV), F32),
                   jax.ShapeDtypeStruct((m, SWA_KV), F32)],
        compiler_params=_params(("arbitrary",)),
        name="l1_in_proj",
    )(*ins)


def _rope_tables(seq):
    rows = seq // GRID_W
    row = np.repeat(np.arange(rows), GRID_W).astype(np.float64)
    col = (np.arange(rows * GRID_W) % GRID_W).astype(np.float64)
    n_freq = SWA_HD // 4
    inv = ROPE_BASE ** (-np.arange(n_freq, dtype=np.float64) / n_freq)
    ar = row[:, None] * inv
    ac = col[:, None] * inv
    ang = np.concatenate([ar, ar, ac, ac], axis=-1)
    sign = np.where((np.arange(SWA_HD) // n_freq) % 2 == 1, 1.0, -1.0)
    two = lambda t: jnp.asarray(np.concatenate([t, t], axis=-1).astype(np.float32))
    return two(np.cos(ang)), two(np.sin(ang) * sign)


def _attend(q_ref, ks, vs, biases, sink_ref, o_ref):
    kcat = ks[0] if len(ks) == 1 else jnp.concatenate(ks, axis=0)
    vcat = vs[0] if len(vs) == 1 else jnp.concatenate(vs, axis=0)
    n = kcat.shape[0]
    r = q_ref.shape[0]
    nblk = n // LANES
    low = lax.broadcasted_iota(jnp.int32, (n, LANES), 1) < SWA_HD
    ones_lo = jnp.where(low, 1.0, 0.0)
    ones_hi = jnp.where(low, 0.0, 1.0)
    low_out = lax.broadcasted_iota(jnp.int32, (2 * r, LANES), 1) < SWA_HD
    top = lax.broadcasted_iota(jnp.int32, (2 * r, 1), 0) < r
    biases = {t: jnp.concatenate([b, b], axis=0) for t, b in biases.items()}
    nt_dims = (((1,), (1,)), ((), ()))

    def scores(g):
        sl = slice(LANES * (g // 2), LANES * (g // 2 + 1))
        kcol, vcol = kcat[:, sl], vcat[:, sl]
        ksw, vsw = pltpu.roll(kcol, SWA_HD, 1), pltpu.roll(vcol, SWA_HD, 1)
        if g % 2 == 0:
            k_lo, k_hi = jnp.where(low, kcol, 0.0), jnp.where(low, 0.0, ksw)
            v_lo, v_hi = jnp.where(low, vcol, 0.0), jnp.where(low, 0.0, vsw)
        else:
            k_lo, k_hi = jnp.where(low, ksw, 0.0), jnp.where(low, 0.0, kcol)
            v_lo, v_hi = jnp.where(low, vsw, 0.0), jnp.where(low, 0.0, vcol)
        k_bd = jnp.concatenate([k_lo, k_hi], axis=0).astype(BF16)
        v_bd = jnp.concatenate([jnp.concatenate([v_lo, ones_lo], axis=1),
                                jnp.concatenate([v_hi, ones_hi], axis=1)], axis=0).astype(BF16)
        qs = jnp.concatenate([q_ref[:, LANES * (2 * g):LANES * (2 * g + 1)],
                              q_ref[:, LANES * (2 * g + 1):LANES * (2 * g + 2)]], axis=0)
        return lax.dot_general(qs, k_bd, nt_dims, preferred_element_type=F32), v_bd

    nxt = scores(0)
    for g in range(SWA_KV_HEADS):
        s, v_bd = nxt
        if g + 1 < SWA_KV_HEADS:
            nxt = scores(g + 1)
        probs, sink_terms = [], []
        for e in range(2):
            blocks = [s[:, e * n + LANES * t:e * n + LANES * (t + 1)] for t in range(nblk)]
            for t, b in biases.items():
                blocks[t] = blocks[t] + b
            mx = blocks[0]
            for b in blocks[1:]:
                mx = jnp.maximum(mx, b)
            sk = jnp.where(top, sink_ref[4 * g + e], sink_ref[4 * g + 2 + e]) * LOG2E
            m = jnp.maximum(jnp.max(mx, axis=-1, keepdims=True), sk)
            probs += [jnp.exp2(b - m).astype(BF16) for b in blocks]
            sink_terms.append(jnp.exp2(sk - m))
        res = _dot(jnp.concatenate(probs, axis=1), v_bd)
        denom = res[:, LANES:2 * LANES] + jnp.where(low_out, sink_terms[0], sink_terms[1])
        out = (res[:, 0:LANES] / denom).astype(o_ref.dtype)
        o_ref[:, LANES * (2 * g):LANES * (2 * g + 1)] = out[0:r]
        o_ref[:, LANES * (2 * g + 1):LANES * (2 * g + 2)] = out[r:2 * r]


def _ctx_attn_kernel(sink_ref, q_ref, k_ref, v_ref, o_ref):
    _attend(q_ref, [k_ref[...]], [v_ref[...]], {}, sink_ref, o_ref)


def _ctx_attention(q, k, v, sink, batch, seq):
    blk = lambda w: pl.BlockSpec((seq, w), lambda b: (b, 0))
    return pl.pallas_call(
        _ctx_attn_kernel,
        grid=(batch,),
        in_specs=[pl.BlockSpec(memory_space=pltpu.SMEM), blk(SWA_Q), blk(SWA_KV), blk(SWA_KV)],
        out_specs=blk(SWA_Q),
        out_shape=jax.ShapeDtypeStruct((batch * seq, SWA_Q), BF16),
        compiler_params=_params(("arbitrary",)),
        name="ctx_attention",
    )(sink, q, k, v)


def _lat_attn_kernel(sink_ref, q_ref, ck_ref, cv_ref, k0_ref, k1_ref, k2_ref, k3_ref,
                     v0_ref, v1_ref, v2_ref, v3_ref, o_ref, *, npair, n_ctx):
    p = pl.program_id(1)
    blk = ATTN_BLOCK
    row = lax.broadcasted_iota(jnp.int32, (blk, blk), 0)
    col = lax.broadcasted_iota(jnp.int32, (blk, blk), 1)
    off_first = jnp.where(p > 0, 0, blk)
    off_last = jnp.where(p < npair - 1, 0, blk)
    first = n_ctx // LANES
    ck, cv = ck_ref[...], cv_ref[...]
    kb = [k0_ref[...], k1_ref[...], k2_ref[...], k3_ref[...]]
    vb = [v0_ref[...], v1_ref[...], v2_ref[...], v3_ref[...]]
    for half, (off_prev, off_next) in enumerate(((off_first, 0), (0, off_last))):
        rows = slice(half * blk, (half + 1) * blk)
        bias_prev = jnp.where(col - row >= off_prev, 0.0, -jnp.inf)
        bias_next = jnp.where(row - col >= off_next, 0.0, -jnp.inf)
        _attend(q_ref.at[rows, :], [ck] + kb[half:half + 3], [cv] + vb[half:half + 3],
                {first: bias_prev, first + 2: bias_next}, sink_ref, o_ref.at[rows, :])


def _lat_attention(q, k, v, ctx_k, ctx_v, sink, batch, seq, n_ctx):
    nb = seq // ATTN_BLOCK
    npair = nb // 2
    kv = lambda f: pl.BlockSpec((ATTN_BLOCK, SWA_KV), lambda b, p: (b * nb + f(p), 0))
    blocks = [kv(lambda p: jnp.maximum(2 * p - 1, 0)), kv(lambda p: 2 * p), kv(lambda p: 2 * p + 1),
              kv(lambda p: jnp.minimum(2 * p + 2, nb - 1))]
    ctx = pl.BlockSpec((n_ctx, SWA_KV), lambda b, p: (b, 0))
    qo = pl.BlockSpec((2 * ATTN_BLOCK, SWA_Q), lambda b, p: (b * npair + p, 0))
    return pl.pallas_call(
        functools.partial(_lat_attn_kernel, npair=npair, n_ctx=n_ctx),
        grid=(batch, npair),
        in_specs=[pl.BlockSpec(memory_space=pltpu.SMEM), qo, ctx, ctx] + blocks + blocks,
        out_specs=qo,
        out_shape=jax.ShapeDtypeStruct((batch * seq, SWA_Q), BF16),
        compiler_params=_params(("arbitrary", "arbitrary")),
        name="lat_attention",
    )(sink, q, ctx_k, ctx_v, k, k, k, k, v, v, v, v)


def kernel(x_prompt, x_sample, state_l0_gla_fwd, state_l0_gla_bwd, cache_l1_k, cache_l1_v, c, c_ctx, mod_w, mod_b, norm_g, ffn_w1, ffn_w3, ffn_w2, l0_w_in, l0_w_gf, l0_b_gf, l0_w_gb, l0_b_gb, l0_g_head, l0_w_out, l1_w_in, l1_sink, l1_w_out, final_g):
    bp, tp, d = x_prompt.shape
    bs, ts, _ = x_sample.shape
    n_ctx = cache_l1_k.shape[1]
    depth = mod_w.shape[0]
    ctx_row = bs

    ffn_stacks = (ffn_w1, ffn_w3, ffn_w2)
    ffn_w = tuple(s[0, 0].astype(BF16) for s in ffn_stacks)
    gate_lo = 2 * GLA_QK + 2 * GLA_V
    gate_hi = gate_lo + 2 * GLA_GATE_RANK
    l0_wm = jnp.concatenate([l0_w_in[:, :gate_lo], l0_w_in[:, gate_hi:]], axis=1).astype(BF16)
    l0_wg = jnp.pad(l0_w_in[:, gate_lo:gate_hi], ((0, 0), (0, LANES - 2 * GLA_GATE_RANK))).astype(BF16)
    wg2 = jnp.zeros((LANES, 2 * GLA_QK), F32)
    wg2 = wg2.at[:GLA_GATE_RANK, :GLA_QK].set(l0_w_gf)
    wg2 = wg2.at[GLA_GATE_RANK:2 * GLA_GATE_RANK, GLA_QK:].set(l0_w_gb).astype(BF16)
    bg = jnp.concatenate([l0_b_gf, l0_b_gb])[None, :]
    g_head = l0_g_head[None, :]
    sink = l1_sink.reshape(-1)
    fin_g = final_g[None, :]

    cond = jnp.zeros((MOD_ROWS, d), F32).at[:bs].set(c).at[ctx_row].set(c_ctx)
    mods = _modulation(cond, mod_w, mod_b)
    mods = mods.reshape(depth, MOD_ROWS, N_MOD, d)
    mods = jnp.pad(mods, ((0, 0), (0, 0), (0, MOD_ROWS - N_MOD), (0, 0)))

    xp = x_prompt.reshape(bp * tp, d)
    xs = x_sample.reshape(bs * ts, d)
    tiles_s = ts // TM
    bfn_p = lambda i: ctx_row
    bfn_s = lambda i: i // tiles_s
    zero_state = jnp.zeros((bp, GLA_QK, GLA_DV), F32)
    s0f = state_l0_gla_fwd.reshape(bs, GLA_QK, GLA_DV)
    s0b = state_l0_gla_bwd.reshape(bs, GLA_QK, GLA_DV)
    rope_tabs = _rope_tables(ts)

    new_state = []
    for layer in range(depth):
        ml = mods[layer]
        ga, gm, gb = (norm_g[layer, r][None, :] for r in range(3))
        last = layer == depth - 1
        xp, proj_w = _half_ffn(xp, ml, bfn_p, ga, *ffn_w, mrow=0,
                               cast=[(w, ()) for w in (l0_w_out, l1_w_in, l1_w_out)] if layer == 0 else None)
        if layer == 0:
            l0_wo, l1_wi, l1_wo = proj_w
        xs, ffn_w_b = _half_ffn(xs, ml, bfn_s, ga, *ffn_w, mrow=0, cast=[(s, (layer, 1)) for s in ffn_stacks])
        if layer % 2 == 0:
            outs = []
            for x, bfn, nb, t, sf0, sb0 in ((xp, bfn_p, bp, tp, zero_state, zero_state), (xs, bfn_s, bs, ts, s0f, s0b)):
                q, k, v, og, la, u = _l0_in_proj(x, ml, bfn, gm, l0_wm, l0_wg, wg2, bg)
                seq3 = lambda a: a.reshape(nb, t, a.shape[-1])
                gla_out, s_f, s_b = _gla(seq3(q), seq3(k), seq3(v), seq3(la), seq3(og), sf0, sb0, g_head, nb, t)
                gla_out = gla_out.reshape(nb * t, GLA_V)
                fn_out = _fnet(seq3(u), nb, t).reshape(nb * t, FN_W)
                outs.append(((gla_out, fn_out), s_f, s_b))
            (mix_p, s_f, s_b), (mix_s, _, _) = outs
            new_state += [s_f.reshape(bp, GLA_HEADS, GLA_DK, GLA_DV), s_b.reshape(bp, GLA_HEADS, GLA_DK, GLA_DV)]
            w_out = l0_wo
        else:
            qp, kp, vp = _l1_in_proj(xp, ml, bfn_p, gm, l1_wi)
            mix_p = (_ctx_attention(qp, kp, vp, sink, bp, tp),)
            qs, ks, vs = _l1_in_proj(xs, ml, bfn_s, gm, l1_wi, rope_tabs, tiles_s)
            mix_s = (_lat_attention(qs, ks, vs, cache_l1_k.reshape(bs * n_ctx, SWA_KV),
                                    cache_l1_v.reshape(bs * n_ctx, SWA_KV), sink, bs, ts, n_ctx),)
            new_state += [kp.reshape(bp, tp, SWA_KV_HEADS, SWA_HD), vp.reshape(bp, tp, SWA_KV_HEADS, SWA_HD)]
            w_out = l1_wo
        fin = fin_g if last else None
        xp, _ = _half_ffn(xp, ml, bfn_p, gb, *ffn_w_b, mrow=6, mixes=mix_p, w_out=w_out, final_g=fin)
        xs, ffn_w = _half_ffn(xs, ml, bfn_s, gb, *ffn_w_b, mrow=6, mixes=mix_s, w_out=w_out, final_g=fin,
                              cast=None if last else [(s, (layer + 1, 0)) for s in ffn_stacks])
    return (xp.reshape(bp, tp, d), xs.reshape(bs, ts, d), *new_state)
```

```python
import functools

import numpy as np
import jax
import jax.numpy as jnp
from jax import lax
from jax.experimental import pallas as pl
from jax.experimental.pallas import tpu as pltpu

F32 = jnp.float32
BF16 = jnp.bfloat16

D_MODEL = 1024
FFN_DIM = 2816
N_MOD = 9
EPS = 1e-6
GRID_W = 64
GLA_HEADS = 4
GLA_DK = 64
GLA_DV = 128
GLA_GATE_RANK = 16
GLA_TAU = 16.0
GLA_CHUNK = 64
GLA_QK = GLA_HEADS * GLA_DK
GLA_V = GLA_HEADS * GLA_DV
FN_GROUPS = 4
FN_CH = 128
FN_W = FN_GROUPS * FN_CH
SWA_HEADS = 16
SWA_KV_HEADS = 4
SWA_HD = 64
SWA_Q = SWA_HEADS * SWA_HD
SWA_KV = SWA_KV_HEADS * SWA_HD
WINDOW = 128
ATTN_BLOCK = 128
ATTN_QBLOCKS = 4
CTX_SEQS = 2
ROPE_BASE = 10000.0
LOG2E = float(np.log2(np.e))

LANES = 128
VMEM_LIMIT = 56 * 1024 * 1024

TM = 1024
SUB = 512
SUB_PROJ = 256
TF = 256
GLA_TILE = 256
GLA_GROUP = 4
FN_ROWS = 256
FN_TOKENS = 2048
FN_PAD = 16
FN_FOLD = 128
MOD_ROWS = 16
MOD_TN = 1024


def _params(sem):
    return pltpu.CompilerParams(dimension_semantics=sem, vmem_limit_bytes=VMEM_LIMIT)


def _resident(shape):
    nd = len(shape)
    return pl.BlockSpec(shape, lambda *_: (0,) * nd, pipeline_mode=pl.Buffered(1))


def _dot(a, b):
    return jnp.dot(a, b, preferred_element_type=F32)


def _silu(x):
    return x * jax.nn.sigmoid(x)


def _ada_norm(x, g, shift, scale):
    ms = jnp.mean(x * x, axis=-1, keepdims=True)
    return (x * lax.rsqrt(ms + EPS)) * (g * (1.0 + scale)) + shift


def _mod_kernel(c_ref, w_ref, b_ref, o_ref):
    c = c_ref[...]
    s = _silu(c).astype(BF16)
    o_ref[0] = _dot(s, w_ref[0].astype(BF16)) + b_ref[0]


def _modulation(cond, mod_w, mod_b):
    depth, d, n = mod_w.shape
    return pl.pallas_call(
        _mod_kernel,
        grid=(depth, n // MOD_TN),
        in_specs=[
            pl.BlockSpec((MOD_ROWS, d), lambda l, j: (0, 0)),
            pl.BlockSpec((1, d, MOD_TN), lambda l, j: (l, 0, j)),
            pl.BlockSpec((1, 1, MOD_TN), lambda l, j: (l, 0, j)),
        ],
        out_specs=pl.BlockSpec((1, MOD_ROWS, MOD_TN), lambda l, j: (l, 0, j)),
        out_shape=jax.ShapeDtypeStruct((depth, MOD_ROWS, n), F32),
        compiler_params=_params(("arbitrary", "arbitrary")),
        name="modulation",
    )(cond, mod_w, mod_b.reshape(depth, 1, n))


def _ffn_kernel(*refs, n_mix, mrow, final, n_cast):
    x_ref = refs[0]
    mix_refs = refs[1:1 + n_mix]
    pos = 1 + n_mix
    if n_mix:
        wo_ref = refs[pos]
        pos += 1
    m_ref, g_ref, w1_ref, w3_ref, w2_ref = refs[pos:pos + 5]
    pos += 5
    if final:
        fg_ref = refs[pos]
        pos += 1
    cast_in = refs[pos:pos + n_cast]
    pos += n_cast
    o_ref = refs[pos]
    cast_out = refs[pos + 1:pos + 1 + n_cast]
    pos += 1 + n_cast
    acc_ref, h_ref = refs[pos], refs[pos + 1]
    for src, dst in zip(cast_in, cast_out):
        dst[...] = src[...].astype(BF16)

    def prologue(rows):
        x = x_ref[rows, :]
        if n_mix:
            off = 0
            mix = None
            for r in mix_refs:
                w = r.shape[1]
                t = _dot(r[rows, :], wo_ref[off:off + w, :])
                mix = t if mix is None else mix + t
                off += w
            x = x + m_ref[0, 5:6, :] * mix
        o_ref[rows, :] = x
        h = _ada_norm(x, g_ref[...], m_ref[0, mrow:mrow + 1, :], m_ref[0, mrow + 1:mrow + 2, :])
        h_ref[rows, :] = h.astype(BF16)

    def chunk(rows, j):
        h = h_ref[rows, :]
        a = _dot(h, w1_ref[:, j * TF:(j + 1) * TF])
        b = _dot(h, w3_ref[:, j * TF:(j + 1) * TF])
        t = _dot((_silu(a) * b).astype(BF16), w2_ref[j * TF:(j + 1) * TF, :])
        if j == 0:
            acc_ref[rows, :] = t
        else:
            acc_ref[rows, :] += t

    def epilogue(rows):
        y = o_ref[rows, :] + (0.5 * m_ref[0, mrow + 2:mrow + 3, :]) * acc_ref[rows, :]
        if final:
            ms = jnp.mean(y * y, axis=-1, keepdims=True)
            y = y * lax.rsqrt(ms + EPS) * fg_ref[...]
        o_ref[rows, :] = y

    subs = [slice(r0, r0 + SUB) for r0 in range(0, x_ref.shape[0], SUB)]
    for rows in subs:
        prologue(rows)
    for j in range(FFN_DIM // TF):
        for rows in subs:
            chunk(rows, j)
    for rows in subs:
        epilogue(rows)


def _half_ffn(x, mods, bfn, g, w1, w3, w2, *, mrow, mixes=(), w_out=None, final_g=None, cast=None):
    m, d = x.shape
    steps = m // TM
    row = lambda i: (i, 0)
    ins = [x]
    specs = [pl.BlockSpec((TM, d), row)]
    for a in mixes:
        ins.append(a)
        specs.append(pl.BlockSpec((TM, a.shape[1]), row))
    if mixes:
        ins.append(w_out)
        specs.append(_resident(w_out.shape))
    ins += [mods, g, w1, w3, w2]
    specs += [pl.BlockSpec((1, MOD_ROWS, d), lambda i: (bfn(i), 0, 0)),
              _resident(g.shape), _resident(w1.shape), _resident(w3.shape), _resident(w2.shape)]
    if final_g is not None:
        ins.append(final_g)
        specs.append(_resident(final_g.shape))
    out_specs = [pl.BlockSpec((TM, d), row)]
    out_shape = [jax.ShapeDtypeStruct((m, d), F32)]
    cast = tuple(cast or ())
    for s, lead in cast:
        rows, cols = s.shape[-2] // steps, s.shape[-1]
        ins.append(s)
        specs.append(pl.BlockSpec((None,) * len(lead) + (rows, cols), lambda i, lead=tuple(lead): lead + (i, 0)))
        out_specs.append(pl.BlockSpec((rows, cols), row))
        out_shape.append(jax.ShapeDtypeStruct(s.shape[-2:], BF16))
    outs = pl.pallas_call(
        functools.partial(_ffn_kernel, n_mix=len(mixes), mrow=mrow, final=final_g is not None,
                          n_cast=len(cast)),
        grid=(steps,),
        in_specs=specs,
        out_specs=out_specs,
        out_shape=out_shape,
        scratch_shapes=[pltpu.VMEM((TM, d), F32), pltpu.VMEM((TM, d), BF16)],
        compiler_params=_params(("arbitrary",)),
        name="half_ffn",
    )(*ins)
    return outs[0], tuple(outs[1:])


def _l0_in_kernel(x_ref, m_ref, g_ref, wm_ref, wg_ref, wg2_ref, bg_ref,
                  q_ref, k_ref, v_ref, og_ref, la_ref, u_ref):
    for r0 in range(0, x_ref.shape[0], SUB_PROJ):
        rows = slice(r0, r0 + SUB_PROJ)
        h = _ada_norm(x_ref[rows, :], g_ref[...], m_ref[0, 3:4, :], m_ref[0, 4:5, :]).astype(BF16)
        q_ref[rows, :] = _dot(h, wm_ref[:, 0:256]) * (GLA_DK ** -0.5)
        lr = _dot(h, wg_ref[...]).astype(BF16)
        k_ref[rows, :] = _dot(h, wm_ref[:, 256:512])
        z = _dot(lr, wg2_ref[...]) + bg_ref[...]
        log_sig = jnp.minimum(z, 0.0) - jnp.log(1.0 + jnp.exp(-jnp.abs(z)))
        la_ref[rows, :] = log_sig * (1.0 / GLA_TAU)
        v_ref[rows, :] = _dot(h, wm_ref[:, 512:1024]).astype(BF16)
        og_ref[rows, :] = _dot(h, wm_ref[:, 1024:1536])
        u_ref[rows, :] = _dot(h, wm_ref[:, 1536:2048]).astype(BF16)


def _l0_in_proj(x, mods, bfn, g, wm, wg, wg2, bg):
    m, d = x.shape
    row = lambda i: (i, 0)
    outs = [(GLA_QK, F32), (GLA_QK, F32), (GLA_V, BF16), (GLA_V, F32), (2 * GLA_QK, F32), (FN_W, BF16)]
    return pl.pallas_call(
        _l0_in_kernel,
        grid=(m // TM,),
        in_specs=[pl.BlockSpec((TM, d), row),
                  pl.BlockSpec((1, MOD_ROWS, d), lambda i: (bfn(i), 0, 0)),
                  _resident(g.shape), _resident(wm.shape), _resident(wg.shape),
                  _resident(wg2.shape), _resident(bg.shape)],
        out_specs=[pl.BlockSpec((TM, w), row) for w, _ in outs],
        out_shape=[jax.ShapeDtypeStruct((m, w), dt) for w, dt in outs],
        compiler_params=_params(("arbitrary",)),
        name="l0_in_proj",
    )(x, mods, g, wm, wg, wg2, bg)


def _gla_tiles(qs, ks, vs, gs, ss, reverse):
    tt = GLA_TILE
    nch = tt // GLA_CHUNK
    seqs = range(len(qs))
    heads = range(GLA_HEADS)
    r = lax.broadcasted_iota(jnp.int32, (tt, tt), 0)
    c = lax.broadcasted_iota(jnp.int32, (tt, tt), 1)
    same = (r // GLA_CHUNK) == (c // GLA_CHUNK)
    m_intra = jnp.logical_and(same, (c >= r) if reverse else (c <= r))
    tri = jnp.where(m_intra, 1.0, 0.0).astype(BF16)
    cc = lax.broadcasted_iota(jnp.int32, (GLA_QK, tt), 1) // GLA_CHUNK
    lane = lax.broadcasted_iota(jnp.int32, (tt, GLA_QK), 1) // GLA_DK
    end_col = [ci * GLA_CHUNK + (0 if reverse else GLA_CHUNK - 1) for ci in range(nch)]
    zero_v = jnp.zeros((GLA_CHUNK, GLA_DV), BF16)
    zero_s = jnp.zeros((GLA_DK, GLA_DV), BF16)

    g_hi = [g.astype(BF16) for g in gs]
    g_lo = [(gs[i] - g_hi[i].astype(F32)).astype(BF16) for i in seqs]
    b = [_dot(tri, g_hi[i]) + _dot(tri, g_lo[i]) for i in seqs]
    qd = [qs[i] * jnp.exp(b[i]) for i in seqs]
    b_t = [x.T for x in b]
    k_t = [x.T for x in ks]
    ends = [[b_t[i][:, e:e + 1] for e in end_col] for i in seqs]
    ki_t, ke_t = [], []
    for i in seqs:
        bl_t = ends[i][nch - 1]
        for ci in range(nch - 2, -1, -1):
            bl_t = jnp.where(cc == ci, ends[i][ci], bl_t)
        ki_t.append((k_t[i] * jnp.exp(-b_t[i])).astype(BF16))
        ke_t.append(k_t[i] * jnp.exp(bl_t - b_t[i]))

    o_intra = [[None] * GLA_HEADS for _ in seqs]
    for h in heads:
        a = [_dot(jnp.where(lane == h, qd[i], 0.0).astype(BF16), ki_t[i]) for i in seqs]
        for i in seqs:
            am = jnp.where(m_intra, a[i], 0.0).astype(BF16)
            o_intra[i][h] = _dot(am, vs[i][:, h * GLA_DV:(h + 1) * GLA_DV])

    deltas = [[None] * GLA_HEADS for _ in seqs]
    for h in heads:
        for i in seqs:
            v_h = vs[i][:, h * GLA_DV:(h + 1) * GLA_DV]
            v_bd = jnp.concatenate(
                [jnp.concatenate([v_h[ci * GLA_CHUNK:(ci + 1) * GLA_CHUNK] if cj == ci else zero_v
                                  for cj in range(nch)], axis=1) for ci in range(nch)], axis=0)
            deltas[i][h] = _dot(ke_t[i][h * GLA_DK:(h + 1) * GLA_DK, :].astype(BF16), v_bd)

    s_h = [[ss[i][h * GLA_DK:(h + 1) * GLA_DK, :] for h in heads] for i in seqs]
    o_inter = [[None] * nch for _ in seqs]
    for ci in (range(nch - 1, -1, -1) if reverse else range(nch)):
        for i in seqs:
            s_bd = jnp.concatenate(
                [jnp.concatenate([s_h[i][h].astype(BF16) if hj == h else zero_s for hj in heads], axis=1)
                 for h in heads], axis=0)
            o_inter[i][ci] = _dot(qd[i][ci * GLA_CHUNK:(ci + 1) * GLA_CHUNK, :].astype(BF16), s_bd)
            decay = jnp.exp(ends[i][ci])
            s_h[i] = [decay[h * GLA_DK:(h + 1) * GLA_DK] * s_h[i][h]
                      + deltas[i][h][:, ci * GLA_DV:(ci + 1) * GLA_DV] for h in heads]
    outs = [jnp.concatenate(o_intra[i], axis=1) + jnp.concatenate(o_inter[i], axis=0) for i in seqs]
    return outs, [jnp.concatenate(s_h[i], axis=0) for i in seqs]


def _gla_kernel(q_ref, k_ref, v_ref, la_ref, og_ref, s0f_ref, s0b_ref, gh_ref,
                o_ref, sf_ref, sb_ref, s_ref, ob_ref, *, nt):
    p = pl.program_id(1)
    i = pl.program_id(2)
    grp = range(q_ref.shape[0])

    @pl.when(p == 0)
    def _backward():
        @pl.when(i == 0)
        def _():
            s_ref[...] = s0b_ref[...]

        start = pl.multiple_of((nt - 1 - i) * GLA_TILE, GLA_TILE)
        outs, states = _gla_tiles(*[[ref[gi] for gi in grp] for ref in (q_ref, k_ref, v_ref, la_ref, s_ref)], True)
        for gi in grp:
            s_ref[gi] = states[gi]
            ob_ref[gi, pl.ds(start, GLA_TILE), :] = outs[gi]

        @pl.when(i == nt - 1)
        def _():
            sb_ref[...] = s_ref[...]

    @pl.when(p == 1)
    def _forward():
        @pl.when(i == 0)
        def _():
            s_ref[...] = s0f_ref[...]

        start = pl.multiple_of(i * GLA_TILE, GLA_TILE)
        outs, states = _gla_tiles(*[[ref[gi] for gi in grp] for ref in (q_ref, k_ref, v_ref, la_ref, s_ref)], False)
        for gi in grp:
            s_ref[gi] = states[gi]
            o = outs[gi] + ob_ref[gi, pl.ds(start, GLA_TILE), :]
            for h in range(GLA_HEADS):
                sl = slice(h * GLA_DV, (h + 1) * GLA_DV)
                oh = o[:, sl]
                ms = jnp.mean(oh * oh, axis=-1, keepdims=True)
                y = oh * lax.rsqrt(ms + EPS) * gh_ref[...]
                o_ref[gi, :, sl] = (y * _silu(og_ref[gi, :, sl])).astype(BF16)

        @pl.when(i == nt - 1)
        def _():
            sf_ref[...] = s_ref[...]


def _gla(q, k, v, la, og, s0f, s0b, g_head, batch, seq):
    nt = seq // GLA_TILE
    gg = GLA_GROUP
    tile = lambda b, p, i: (b, p * i + (1 - p) * (nt - 1 - i), 0)
    fwd_tile = lambda b, p, i: (b, p * i, 0)
    st = pl.BlockSpec((gg, GLA_QK, GLA_DV), lambda b, p, i: (b, 0, 0))
    return pl.pallas_call(
        functools.partial(_gla_kernel, nt=nt),
        grid=(batch // gg, 2, nt),
        in_specs=[pl.BlockSpec((gg, GLA_TILE, GLA_QK), tile),
                  pl.BlockSpec((gg, GLA_TILE, GLA_QK), tile),
                  pl.BlockSpec((gg, GLA_TILE, GLA_V), tile),
                  pl.BlockSpec((gg, GLA_TILE, GLA_QK), lambda b, p, i: tile(b, p, i)[:2] + (1 - p,)),
                  pl.BlockSpec((gg, GLA_TILE, GLA_V), fwd_tile),
                  st, st, _resident(g_head.shape)],
        out_specs=[pl.BlockSpec((gg, GLA_TILE, GLA_V), fwd_tile), st, st],
        out_shape=[jax.ShapeDtypeStruct((batch, seq, GLA_V), BF16),
                   jax.ShapeDtypeStruct((batch, GLA_QK, GLA_DV), F32),
                   jax.ShapeDtypeStruct((batch, GLA_QK, GLA_DV), F32)],
        scratch_shapes=[pltpu.VMEM((gg, GLA_QK, GLA_DV), F32), pltpu.VMEM((gg, seq, GLA_V), F32)],
        compiler_params=_params(("arbitrary", "arbitrary", "arbitrary")),
        name="gla",
    )(q, k, v, la, og, s0f, s0b, g_head)


def _fnet_kernel(u_ref, ccs_ref, cs_ref, o_ref, ucs_ref, eo_ref, *, seq, scale):
    nseq = u_ref.shape[0]
    half = seq // 2
    blk = min(FN_FOLD, half)
    rows = min(FN_ROWS, seq)
    rr = lax.broadcasted_iota(jnp.int32, (blk, blk + FN_PAD), 0)
    cc = lax.broadcasted_iota(jnp.int32, (blk, blk + FN_PAD), 1)
    mirror = jnp.where(cc == blk - rr, 1.0, 0.0).astype(BF16)
    sign = 1.0 - 2.0 * (lax.broadcasted_iota(jnp.int32, (rows, 1), 0) % 2).astype(F32)
    for s in range(nseq):
        for part in range(2):
            ucs_ref[s, part, seq:seq + FN_PAD, :] = jnp.zeros((FN_PAD, FN_W), BF16)
        for gi in range(FN_GROUPS):
            sl = slice(gi * FN_CH, (gi + 1) * FN_CH)
            t = _dot(u_ref[s, :, sl], ccs_ref[...])
            ucs_ref[s, 0, 0:seq, sl] = t[:, :FN_CH].astype(BF16)
            ucs_ref[s, 1, 0:seq, sl] = t[:, FN_CH:].astype(BF16)
    for s in range(nseq):
        for j in range(half // blk):
            w0 = seq - (j + 1) * blk
            for part, sgn in ((0, 1.0), (1, -1.0)):
                own = ucs_ref[s, part, j * blk:(j + 1) * blk, :].astype(F32)
                partner = _dot(mirror, ucs_ref[s, part, w0:w0 + blk + FN_PAD, :])
                eo_ref[s, part * half + j * blk:part * half + (j + 1) * blk, :] = (own + sgn * partner).astype(BF16)
    for s in range(nseq):
        mid = ucs_ref[s, 0, half:half + 1, :].astype(F32)
        for j in range(seq // rows):
            rs = slice(j * rows, (j + 1) * rows)
            o_ref[s, rs, :] = ((_dot(cs_ref[rs, :], eo_ref[s]) + sign * mid) * scale).astype(BF16)


def _dft_tables(seq):
    def cs(n, cols):
        k = np.arange(n, dtype=np.int64)
        t = np.arange(cols, dtype=np.int64)
        ang = 2.0 * np.pi * ((k[:, None] * t[None, :]) % n).astype(np.float64) / n
        return np.cos(ang), np.sin(ang)
    ct, st = cs(seq, seq // 2)
    cc, sc = cs(FN_CH, FN_CH)
    to = lambda a: jnp.asarray(a.astype(np.float32)).astype(BF16)
    return to(np.concatenate([ct, -st], axis=1)), to(np.concatenate([cc, sc], axis=1))


def _fnet(u, batch, seq):
    cs, ccs = _dft_tables(seq)
    nseq = max(1, FN_TOKENS // seq)
    blk = pl.BlockSpec((nseq, seq, FN_W), lambda b: (b, 0, 0))
    return pl.pallas_call(
        functools.partial(_fnet_kernel, seq=seq, scale=float((seq * FN_CH) ** -0.5)),
        grid=(batch // nseq,),
        in_specs=[blk, _resident(ccs.shape), _resident(cs.shape)],
        out_specs=blk,
        out_shape=jax.ShapeDtypeStruct((batch, seq, FN_W), BF16),
        scratch_shapes=[pltpu.VMEM((nseq, 2, seq + FN_PAD, FN_W), BF16),
                        pltpu.VMEM((nseq, seq, FN_W), BF16)],
        compiler_params=_params(("arbitrary",)),
        name="fnet",
    )(u, ccs, cs)


def _rope128(x, cos, sin_signed, odd):
    back = pltpu.roll(x, 16, 1)
    fwd = pltpu.roll(x, 112, 1)
    return x * cos + jnp.where(odd, back, fwd) * sin_signed


def _l1_in_kernel(*refs, rope):
    if rope:
        x_ref, m_ref, g_ref, w_ref, cos_ref, sin_ref, q_ref, k_ref, v_ref = refs
        odd = (lax.broadcasted_iota(jnp.int32, (SUB_PROJ, LANES), 1) // 16) % 2 == 1
    else:
        x_ref, m_ref, g_ref, w_ref, q_ref, k_ref, v_ref = refs

    def project(h, rows, dst, col0, width, scale):
        for j in range(width // 256):
            y = _dot(h, w_ref[:, col0 + 256 * j:col0 + 256 * (j + 1)])
            if scale != 1.0:
                y = y * scale
            for half in range(2):
                yh = y[:, LANES * half:LANES * (half + 1)]
                if rope:
                    yh = _rope128(yh, cos_ref[rows, :], sin_ref[rows, :], odd)
                lo = 256 * j + LANES * half
                dst[rows, lo:lo + LANES] = yh.astype(dst.dtype)

    for r0 in range(0, x_ref.shape[0], SUB_PROJ):
        rows = slice(r0, r0 + SUB_PROJ)
        h = _ada_norm(x_ref[rows, :], g_ref[...], m_ref[0, 3:4, :], m_ref[0, 4:5, :]).astype(BF16)
        project(h, rows, q_ref, 0, SWA_Q, SWA_HD ** -0.5 * LOG2E)
        project(h, rows, k_ref, SWA_Q, SWA_KV, 1.0)
        v_ref[rows, :] = _dot(h, w_ref[:, SWA_Q + SWA_KV:SWA_Q + 2 * SWA_KV])


def _l1_in_proj(x, mods, bfn, g, w, rope_tabs=None, tiles_per_seq=None):
    m, d = x.shape
    row = lambda i: (i, 0)
    ins = [x, mods, g, w]
    specs = [pl.BlockSpec((TM, d), row),
             pl.BlockSpec((1, MOD_ROWS, d), lambda i: (bfn(i), 0, 0)),
             _resident(g.shape), _resident(w.shape)]
    if rope_tabs is not None:
        ins += list(rope_tabs)
        specs += [pl.BlockSpec((TM, LANES), lambda i: (i % tiles_per_seq, 0))] * 2
    return pl.pallas_call(
        functools.partial(_l1_in_kernel, rope=rope_tabs is not None),
        grid=(m // TM,),
        in_specs=specs,
        out_specs=[pl.BlockSpec((TM, SWA_Q), row), pl.BlockSpec((TM, SWA_KV), row), pl.BlockSpec((TM, SWA_KV), row)],
        out_shape=[jax.ShapeDtypeStruct((m, SWA_Q), BF16),
                   jax.ShapeDtypeStruct((m, SWA_KV), F32),
                   jax.ShapeDtypeStruct((m, SWA_KV), F32)],
        compiler_params=_params(("arbitrary",)),
        name="l1_in_proj",
    )(*ins)


def _rope_tables(seq):
    rows = seq // GRID_W
    row = np.repeat(np.arange(rows), GRID_W).astype(np.float64)
    col = (np.arange(rows * GRID_W) % GRID_W).astype(np.float64)
    n_freq = SWA_HD // 4
    inv = ROPE_BASE ** (-np.arange(n_freq, dtype=np.float64) / n_freq)
    ar = row[:, None] * inv
    ac = col[:, None] * inv
    ang = np.concatenate([ar, ar, ac, ac], axis=-1)
    sign = np.where((np.arange(SWA_HD) // n_freq) % 2 == 1, 1.0, -1.0)
    two = lambda t: jnp.asarray(np.concatenate([t, t], axis=-1).astype(np.float32))
    return two(np.cos(ang)), two(np.sin(ang) * sign)


def _attend(q_ref, ks, vs, biases, sink_ref, o_ref):
    kcat = ks[0] if len(ks) == 1 else jnp.concatenate(ks, axis=0)
    vcat = vs[0] if len(vs) == 1 else jnp.concatenate(vs, axis=0)
    n = kcat.shape[0]
    r = q_ref.shape[0]
    nblk = n // LANES
    low = lax.broadcasted_iota(jnp.int32, (n, LANES), 1) < SWA_HD
    ones_lo = jnp.where(low, 1.0, 0.0)
    ones_hi = jnp.where(low, 0.0, 1.0)
    low_out = lax.broadcasted_iota(jnp.int32, (2 * r, LANES), 1) < SWA_HD
    top = lax.broadcasted_iota(jnp.int32, (2 * r, 1), 0) < r
    biases = {t: jnp.concatenate([b, b], axis=0) for t, b in biases.items()}
    nt_dims = (((1,), (1,)), ((), ()))

    def scores(g):
        sl = slice(LANES * (g // 2), LANES * (g // 2 + 1))
        kcol, vcol = kcat[:, sl], vcat[:, sl]
        ksw, vsw = pltpu.roll(kcol, SWA_HD, 1), pltpu.roll(vcol, SWA_HD, 1)
        if g % 2 == 0:
            k_lo, k_hi = jnp.where(low, kcol, 0.0), jnp.where(low, 0.0, ksw)
            v_lo, v_hi = jnp.where(low, vcol, 0.0), jnp.where(low, 0.0, vsw)
        else:
            k_lo, k_hi = jnp.where(low, ksw, 0.0), jnp.where(low, 0.0, kcol)
            v_lo, v_hi = jnp.where(low, vsw, 0.0), jnp.where(low, 0.0, vcol)
        k_bd = jnp.concatenate([k_lo, k_hi], axis=0).astype(BF16)
        v_bd = jnp.concatenate([jnp.concatenate([v_lo, ones_lo], axis=1),
                                jnp.concatenate([v_hi, ones_hi], axis=1)], axis=0).astype(BF16)
        qs = jnp.concatenate([q_ref[:, LANES * (2 * g):LANES * (2 * g + 1)],
                              q_ref[:, LANES * (2 * g + 1):LANES * (2 * g + 2)]], axis=0)
        return lax.dot_general(qs, k_bd, nt_dims, preferred_element_type=F32), v_bd

    nxt = scores(0)
    for g in range(SWA_KV_HEADS):
        s, v_bd = nxt
        if g + 1 < SWA_KV_HEADS:
            nxt = scores(g + 1)
        probs, sink_terms = [], []
        for e in range(2):
            blocks = [s[:, e * n + LANES * t:e * n + LANES * (t + 1)] for t in range(nblk)]
            for t, b in biases.items():
                blocks[t] = blocks[t] + b
            mx = blocks[0]
            for b in blocks[1:]:
                mx = jnp.maximum(mx, b)
            sk = jnp.where(top, sink_ref[4 * g + e], sink_ref[4 * g + 2 + e]) * LOG2E
            m = jnp.maximum(jnp.max(mx, axis=-1, keepdims=True), sk)
            probs += [jnp.exp2(b - m).astype(BF16) for b in blocks]
            sink_terms.append(jnp.exp2(sk - m))
        res = _dot(jnp.concatenate(probs, axis=1), v_bd)
        denom = res[:, LANES:2 * LANES] + jnp.where(low_out, sink_terms[0], sink_terms[1])
        out = (res[:, 0:LANES] / denom).astype(o_ref.dtype)
        o_ref[:, LANES * (2 * g):LANES * (2 * g + 1)] = out[0:r]
        o_ref[:, LANES * (2 * g + 1):LANES * (2 * g + 2)] = out[r:2 * r]


def _ctx_attn_kernel(sink_ref, q_ref, k_ref, v_ref, o_ref, *, seq):
    for si in range(CTX_SEQS):
        rows = slice(si * seq, (si + 1) * seq)
        _attend(q_ref.at[rows, :], [k_ref[rows, :]], [v_ref[rows, :]], {}, sink_ref, o_ref.at[rows, :])


def _ctx_attention(q, k, v, sink, batch, seq):
    blk = lambda w: pl.BlockSpec((CTX_SEQS * seq, w), lambda b: (b, 0))
    return pl.pallas_call(
        functools.partial(_ctx_attn_kernel, seq=seq),
        grid=(batch // CTX_SEQS,),
        in_specs=[pl.BlockSpec(memory_space=pltpu.SMEM), blk(SWA_Q), blk(SWA_KV), blk(SWA_KV)],
        out_specs=blk(SWA_Q),
        out_shape=jax.ShapeDtypeStruct((batch * seq, SWA_Q), BF16),
        compiler_params=_params(("arbitrary",)),
        name="ctx_attention",
    )(sink, q, k, v)


def _lat_attn_kernel(sink_ref, q_ref, ck_ref, cv_ref, *refs, nsteps, n_ctx):
    nq = ATTN_QBLOCKS
    k_refs, v_refs, o_ref = refs[:nq + 2], refs[nq + 2:2 * nq + 4], refs[2 * nq + 4]
    p = pl.program_id(1)
    blk = ATTN_BLOCK
    row = lax.broadcasted_iota(jnp.int32, (blk, blk), 0)
    col = lax.broadcasted_iota(jnp.int32, (blk, blk), 1)
    off_first = jnp.where(p > 0, 0, blk)
    off_last = jnp.where(p < nsteps - 1, 0, blk)
    first = n_ctx // LANES
    ck, cv = ck_ref[...], cv_ref[...]
    kb = [r[...] for r in k_refs]
    vb = [r[...] for r in v_refs]
    for qi in range(nq):
        rows = slice(qi * blk, (qi + 1) * blk)
        off_prev = off_first if qi == 0 else 0
        off_next = off_last if qi == nq - 1 else 0
        bias_prev = jnp.where(col - row >= off_prev, 0.0, -jnp.inf)
        bias_next = jnp.where(row - col >= off_next, 0.0, -jnp.inf)
        _attend(q_ref.at[rows, :], [ck] + kb[qi:qi + 3], [cv] + vb[qi:qi + 3],
                {first: bias_prev, first + 2: bias_next}, sink_ref, o_ref.at[rows, :])


def _lat_attention(q, k, v, ctx_k, ctx_v, sink, batch, seq, n_ctx):
    nb = seq // ATTN_BLOCK
    nq = ATTN_QBLOCKS
    nsteps = nb // nq
    kv = lambda j: pl.BlockSpec((ATTN_BLOCK, SWA_KV),
                                lambda b, p: (b * nb + jnp.clip(nq * p - 1 + j, 0, nb - 1), 0))
    blocks = [kv(j) for j in range(nq + 2)]
    ctx = pl.BlockSpec((n_ctx, SWA_KV), lambda b, p: (b, 0))
    qo = pl.BlockSpec((nq * ATTN_BLOCK, SWA_Q), lambda b, p: (b * nsteps + p, 0))
    return pl.pallas_call(
        functools.partial(_lat_attn_kernel, nsteps=nsteps, n_ctx=n_ctx),
        grid=(batch, nsteps),
        in_specs=[pl.BlockSpec(memory_space=pltpu.SMEM), qo, ctx, ctx] + blocks + blocks,
        out_specs=qo,
        out_shape=jax.ShapeDtypeStruct((batch * seq, SWA_Q), BF16),
        compiler_params=_params(("arbitrary", "arbitrary")),
        name="lat_attention",
    )(sink, q, ctx_k, ctx_v, *([k] * (nq + 2)), *([v] * (nq + 2)))


def kernel(x_prompt, x_sample, state_l0_gla_fwd, state_l0_gla_bwd, cache_l1_k, cache_l1_v, c, c_ctx, mod_w, mod_b, norm_g, ffn_w1, ffn_w3, ffn_w2, l0_w_in, l0_w_gf, l0_b_gf, l0_w_gb, l0_b_gb, l0_g_head, l0_w_out, l1_w_in, l1_sink, l1_w_out, final_g):
    bp, tp, d = x_prompt.shape
    bs, ts, _ = x_sample.shape
    n_ctx = cache_l1_k.shape[1]
    depth = mod_w.shape[0]
    ctx_row = bs

    ffn_stacks = (ffn_w1, ffn_w3, ffn_w2)
    ffn_w = tuple(s[0, 0].astype(BF16) for s in ffn_stacks)
    gate_lo = 2 * GLA_QK + 2 * GLA_V
    gate_hi = gate_lo + 2 * GLA_GATE_RANK
    l0_wm = jnp.concatenate([l0_w_in[:, :gate_lo], l0_w_in[:, gate_hi:]], axis=1).astype(BF16)
    l0_wg = jnp.pad(l0_w_in[:, gate_lo:gate_hi], ((0, 0), (0, LANES - 2 * GLA_GATE_RANK))).astype(BF16)
    wg2 = jnp.zeros((LANES, 2 * GLA_QK), F32)
    wg2 = wg2.at[:GLA_GATE_RANK, :GLA_QK].set(l0_w_gf)
    wg2 = wg2.at[GLA_GATE_RANK:2 * GLA_GATE_RANK, GLA_QK:].set(l0_w_gb).astype(BF16)
    bg = jnp.concatenate([l0_b_gf, l0_b_gb])[None, :]
    g_head = l0_g_head[None, :]
    sink = l1_sink.reshape(-1)
    fin_g = final_g[None, :]

    cond = jnp.zeros((MOD_ROWS, d), F32).at[:bs].set(c).at[ctx_row].set(c_ctx)
    mods = _modulation(cond, mod_w, mod_b)
    mods = mods.reshape(depth, MOD_ROWS, N_MOD, d)
    mods = jnp.pad(mods, ((0, 0), (0, 0), (0, MOD_ROWS - N_MOD), (0, 0)))

    xp = x_prompt.reshape(bp * tp, d)
    xs = x_sample.reshape(bs * ts, d)
    tiles_s = ts // TM
    bfn_p = lambda i: ctx_row
    bfn_s = lambda i: i // tiles_s
    zero_state = jnp.zeros((bp, GLA_QK, GLA_DV), F32)
    s0f = state_l0_gla_fwd.reshape(bs, GLA_QK, GLA_DV)
    s0b = state_l0_gla_bwd.reshape(bs, GLA_QK, GLA_DV)
    rope_tabs = _rope_tables(ts)

    new_state = []
    for layer in range(depth):
        ml = mods[layer]
        ga, gm, gb = (norm_g[layer, r][None, :] for r in range(3))
        last = layer == depth - 1
        xp, proj_w = _half_ffn(xp, ml, bfn_p, ga, *ffn_w, mrow=0,
                               cast=[(w, ()) for w in (l0_w_out, l1_w_in, l1_w_out)] if layer == 0 else None)
        if layer == 0:
            l0_wo, l1_wi, l1_wo = proj_w
        xs, ffn_w_b = _half_ffn(xs, ml, bfn_s, ga, *ffn_w, mrow=0, cast=[(s, (layer, 1)) for s in ffn_stacks])
        if layer % 2 == 0:
            outs = []
            for x, bfn, nb, t, sf0, sb0 in ((xp, bfn_p, bp, tp, zero_state, zero_state), (xs, bfn_s, bs, ts, s0f, s0b)):
                q, k, v, og, la, u = _l0_in_proj(x, ml, bfn, gm, l0_wm, l0_wg, wg2, bg)
                seq3 = lambda a: a.reshape(nb, t, a.shape[-1])
                gla_out, s_f, s_b = _gla(seq3(q), seq3(k), seq3(v), seq3(la), seq3(og), sf0, sb0, g_head, nb, t)
                gla_out = gla_out.reshape(nb * t, GLA_V)
                fn_out = _fnet(seq3(u), nb, t).reshape(nb * t, FN_W)
                outs.append(((gla_out, fn_out), s_f, s_b))
            (mix_p, s_f, s_b), (mix_s, _, _) = outs
            new_state += [s_f.reshape(bp, GLA_HEADS, GLA_DK, GLA_DV), s_b.reshape(bp, GLA_HEADS, GLA_DK, GLA_DV)]
            w_out = l0_wo
        else:
            qp, kp, vp = _l1_in_proj(xp, ml, bfn_p, gm, l1_wi)
            mix_p = (_ctx_attention(qp, kp, vp, sink, bp, tp),)
            qs, ks, vs = _l1_in_proj(xs, ml, bfn_s, gm, l1_wi, rope_tabs, tiles_s)
            mix_s = (_lat_attention(qs, ks, vs, cache_l1_k.reshape(bs * n_ctx, SWA_KV),
                                    cache_l1_v.reshape(bs * n_ctx, SWA_KV), sink, bs, ts, n_ctx),)
            new_state += [kp.reshape(bp, tp, SWA_KV_HEADS, SWA_HD), vp.reshape(bp, tp, SWA_KV_HEADS, SWA_HD)]
            w_out = l1_wo
        fin = fin_g if last else None
        xp, _ = _half_ffn(xp, ml, bfn_p, gb, *ffn_w_b, mrow=6, mixes=mix_p, w_out=w_out, final_g=fin)
        xs, ffn_w = _half_ffn(xs, ml, bfn_s, gb, *ffn_w_b, mrow=6, mixes=mix_s, w_out=w_out, final_g=fin,
                              cast=None if last else [(s, (layer + 1, 0)) for s in ffn_stacks])
    return (xp.reshape(bp, tp, d), xs.reshape(bs, ts, d), *new_state)
```

```python
import functools

import numpy as np
import jax
import jax.numpy as jnp
from jax import lax
from jax.experimental import pallas as pl
from jax.experimental.pallas import tpu as pltpu

F32 = jnp.float32
BF16 = jnp.bfloat16

D_MODEL = 1024
FFN_DIM = 2816
N_MOD = 9
EPS = 1e-6
GRID_W = 64
GLA_HEADS = 4
GLA_DK = 64
GLA_DV = 128
GLA_GATE_RANK = 16
GLA_TAU = 16.0
GLA_CHUNK = 64
GLA_QK = GLA_HEADS * GLA_DK
GLA_V = GLA_HEADS * GLA_DV
FN_GROUPS = 4
FN_CH = 128
FN_W = FN_GROUPS * FN_CH
SWA_HEADS = 16
SWA_KV_HEADS = 4
SWA_HD = 64
SWA_Q = SWA_HEADS * SWA_HD
SWA_KV = SWA_KV_HEADS * SWA_HD
WINDOW = 128
ATTN_BLOCK = 128
ATTN_QBLOCKS = 8
CTX_SEQS = 2
ROPE_BASE = 10000.0
LOG2E = float(np.log2(np.e))

LANES = 128
VMEM_LIMIT = 56 * 1024 * 1024

TM = 1024
SUB = 512
SUB_PROJ = 256
SUB_PROJ_L0 = 512
TF = 256
GLA_TILE = 256
GLA_GROUP = 4
FN_ROWS = 256
FN_TOKENS = 2048
FN_PAD = 16
FN_FOLD = 128
MOD_ROWS = 16
MOD_TN = 1024


def _params(sem):
    return pltpu.CompilerParams(dimension_semantics=sem, vmem_limit_bytes=VMEM_LIMIT)


def _resident(shape):
    nd = len(shape)
    return pl.BlockSpec(shape, lambda *_: (0,) * nd, pipeline_mode=pl.Buffered(1))


def _dot(a, b):
    return jnp.dot(a, b, preferred_element_type=F32)


def _silu(x):
    return x * jax.nn.sigmoid(x)


def _ada_norm(x, g, shift, scale):
    ms = jnp.mean(x * x, axis=-1, keepdims=True)
    return (x * lax.rsqrt(ms + EPS)) * (g * (1.0 + scale)) + shift


def _mod_kernel(c_ref, w_ref, b_ref, o_ref):
    c = c_ref[...]
    s = _silu(c).astype(BF16)
    o_ref[0] = _dot(s, w_ref[0].astype(BF16)) + b_ref[0]


def _modulation(cond, mod_w, mod_b):
    depth, d, n = mod_w.shape
    return pl.pallas_call(
        _mod_kernel,
        grid=(depth, n // MOD_TN),
        in_specs=[
            pl.BlockSpec((MOD_ROWS, d), lambda l, j: (0, 0)),
            pl.BlockSpec((1, d, MOD_TN), lambda l, j: (l, 0, j)),
            pl.BlockSpec((1, 1, MOD_TN), lambda l, j: (l, 0, j)),
        ],
        out_specs=pl.BlockSpec((1, MOD_ROWS, MOD_TN), lambda l, j: (l, 0, j)),
        out_shape=jax.ShapeDtypeStruct((depth, MOD_ROWS, n), F32),
        compiler_params=_params(("arbitrary", "arbitrary")),
        name="modulation",
    )(cond, mod_w, mod_b.reshape(depth, 1, n))


def _ffn_kernel(*refs, n_mix, mrow, final, n_cast):
    x_ref = refs[0]
    mix_refs = refs[1:1 + n_mix]
    pos = 1 + n_mix
    if n_mix:
        wo_ref = refs[pos]
        pos += 1
    m_ref, g_ref, w1_ref, w3_ref, w2_ref = refs[pos:pos + 5]
    pos += 5
    if final:
        fg_ref = refs[pos]
        pos += 1
    cast_in = refs[pos:pos + n_cast]
    pos += n_cast
    o_ref = refs[pos]
    cast_out = refs[pos + 1:pos + 1 + n_cast]
    pos += 1 + n_cast
    acc_ref, h_ref = refs[pos], refs[pos + 1]
    for src, dst in zip(cast_in, cast_out):
        dst[...] = src[...].astype(BF16)

    def prologue(rows):
        x = x_ref[rows, :]
        if n_mix:
            off = 0
            mix = None
            for r in mix_refs:
                w = r.shape[1]
                t = _dot(r[rows, :], wo_ref[off:off + w, :])
                mix = t if mix is None else mix + t
                off += w
            x = x + m_ref[0, 5:6, :] * mix
        o_ref[rows, :] = x
        h = _ada_norm(x, g_ref[...], m_ref[0, mrow:mrow + 1, :], m_ref[0, mrow + 1:mrow + 2, :])
        h_ref[rows, :] = h.astype(BF16)

    def chunk(rows, j):
        h = h_ref[rows, :]
        a = _dot(h, w1_ref[:, j * TF:(j + 1) * TF])
        b = _dot(h, w3_ref[:, j * TF:(j + 1) * TF])
        t = _dot((_silu(a) * b).astype(BF16), w2_ref[j * TF:(j + 1) * TF, :])
        if j == 0:
            acc_ref[rows, :] = t
        else:
            acc_ref[rows, :] += t

    def epilogue(rows):
        y = o_ref[rows, :] + (0.5 * m_ref[0, mrow + 2:mrow + 3, :]) * acc_ref[rows, :]
        if final:
            ms = jnp.mean(y * y, axis=-1, keepdims=True)
            y = y * lax.rsqrt(ms + EPS) * fg_ref[...]
        o_ref[rows, :] = y

    subs = [slice(r0, r0 + SUB) for r0 in range(0, x_ref.shape[0], SUB)]
    for rows in subs:
        prologue(rows)
    for j in range(FFN_DIM // TF):
        for rows in subs:
            chunk(rows, j)
    for rows in subs:
        epilogue(rows)


def _half_ffn(x, mods, bfn, g, w1, w3, w2, *, mrow, mixes=(), w_out=None, final_g=None, cast=None):
    m, d = x.shape
    steps = m // TM
    row = lambda i: (i, 0)
    ins = [x]
    specs = [pl.BlockSpec((TM, d), row)]
    for a in mixes:
        ins.append(a)
        specs.append(pl.BlockSpec((TM, a.shape[1]), row))
    if mixes:
        ins.append(w_out)
        specs.append(_resident(w_out.shape))
    ins += [mods, g, w1, w3, w2]
    specs += [pl.BlockSpec((1, MOD_ROWS, d), lambda i: (bfn(i), 0, 0)),
              _resident(g.shape), _resident(w1.shape), _resident(w3.shape), _resident(w2.shape)]
    if final_g is not None:
        ins.append(final_g)
        specs.append(_resident(final_g.shape))
    out_specs = [pl.BlockSpec((TM, d), row)]
    out_shape = [jax.ShapeDtypeStruct((m, d), F32)]
    cast = tuple(cast or ())
    for s, lead in cast:
        rows, cols = s.shape[-2] // steps, s.shape[-1]
        ins.append(s)
        specs.append(pl.BlockSpec((None,) * len(lead) + (rows, cols), lambda i, lead=tuple(lead): lead + (i, 0)))
        out_specs.append(pl.BlockSpec((rows, cols), row))
        out_shape.append(jax.ShapeDtypeStruct(s.shape[-2:], BF16))
    outs = pl.pallas_call(
        functools.partial(_ffn_kernel, n_mix=len(mixes), mrow=mrow, final=final_g is not None,
                          n_cast=len(cast)),
        grid=(steps,),
        in_specs=specs,
        out_specs=out_specs,
        out_shape=out_shape,
        scratch_shapes=[pltpu.VMEM((TM, d), F32), pltpu.VMEM((TM, d), BF16)],
        compiler_params=_params(("arbitrary",)),
        name="half_ffn",
    )(*ins)
    return outs[0], tuple(outs[1:])


def _l0_in_kernel(x_ref, m_ref, g_ref, wm_ref, wg_ref, wg2_ref, bg_ref,
                  q_ref, k_ref, v_ref, og_ref, la_ref, u_ref):
    for r0 in range(0, x_ref.shape[0], SUB_PROJ_L0):
        rows = slice(r0, r0 + SUB_PROJ_L0)
        h = _ada_norm(x_ref[rows, :], g_ref[...], m_ref[0, 3:4, :], m_ref[0, 4:5, :]).astype(BF16)
        q_ref[rows, :] = _dot(h, wm_ref[:, 0:256]) * (GLA_DK ** -0.5)
        lr = _dot(h, wg_ref[...]).astype(BF16)
        k_ref[rows, :] = _dot(h, wm_ref[:, 256:512])
        z = _dot(lr, wg2_ref[...]) + bg_ref[...]
        log_sig = jnp.minimum(z, 0.0) - jnp.log(1.0 + jnp.exp(-jnp.abs(z)))
        la_ref[rows, :] = log_sig * (1.0 / GLA_TAU)
        v_ref[rows, :] = _dot(h, wm_ref[:, 512:1024]).astype(BF16)
        og_ref[rows, :] = _dot(h, wm_ref[:, 1024:1536])
        u_ref[rows, :] = _dot(h, wm_ref[:, 1536:2048]).astype(BF16)


def _l0_in_proj(x, mods, bfn, g, wm, wg, wg2, bg):
    m, d = x.shape
    row = lambda i: (i, 0)
    outs = [(GLA_QK, F32), (GLA_QK, F32), (GLA_V, BF16), (GLA_V, F32), (2 * GLA_QK, F32), (FN_W, BF16)]
    return pl.pallas_call(
        _l0_in_kernel,
        grid=(m // TM,),
        in_specs=[pl.BlockSpec((TM, d), row),
                  pl.BlockSpec((1, MOD_ROWS, d), lambda i: (bfn(i), 0, 0)),
                  _resident(g.shape), _resident(wm.shape), _resident(wg.shape),
                  _resident(wg2.shape), _resident(bg.shape)],
        out_specs=[pl.BlockSpec((TM, w), row) for w, _ in outs],
        out_shape=[jax.ShapeDtypeStruct((m, w), dt) for w, dt in outs],
        compiler_params=_params(("arbitrary",)),
        name="l0_in_proj",
    )(x, mods, g, wm, wg, wg2, bg)


def _gla_tiles(qs, ks, vs, gs, ss, reverse):
    tt = GLA_TILE
    nch = tt // GLA_CHUNK
    seqs = range(len(qs))
    heads = range(GLA_HEADS)
    r = lax.broadcasted_iota(jnp.int32, (tt, tt), 0)
    c = lax.broadcasted_iota(jnp.int32, (tt, tt), 1)
    same = (r // GLA_CHUNK) == (c // GLA_CHUNK)
    m_intra = jnp.logical_and(same, (c >= r) if reverse else (c <= r))
    tri = jnp.where(m_intra, 1.0, 0.0).astype(BF16)
    cc = lax.broadcasted_iota(jnp.int32, (GLA_QK, tt), 1) // GLA_CHUNK
    lane = lax.broadcasted_iota(jnp.int32, (tt, GLA_QK), 1) // GLA_DK
    end_col = [ci * GLA_CHUNK + (0 if reverse else GLA_CHUNK - 1) for ci in range(nch)]
    zero_v = jnp.zeros((GLA_CHUNK, GLA_DV), BF16)
    zero_s = jnp.zeros((GLA_DK, GLA_DV), BF16)

    g_hi = [g.astype(BF16) for g in gs]
    g_lo = [(gs[i] - g_hi[i].astype(F32)).astype(BF16) for i in seqs]
    b = [_dot(tri, g_hi[i]) + _dot(tri, g_lo[i]) for i in seqs]
    qd = [qs[i] * jnp.exp(b[i]) for i in seqs]
    b_t = [x.T for x in b]
    k_t = [x.T for x in ks]
    ends = [[b_t[i][:, e:e + 1] for e in end_col] for i in seqs]
    ki_t, ke_t = [], []
    for i in seqs:
        bl_t = ends[i][nch - 1]
        for ci in range(nch - 2, -1, -1):
            bl_t = jnp.where(cc == ci, ends[i][ci], bl_t)
        ki_t.append((k_t[i] * jnp.exp(-b_t[i])).astype(BF16))
        ke_t.append(k_t[i] * jnp.exp(bl_t - b_t[i]))

    o_intra = [[None] * GLA_HEADS for _ in seqs]
    for h in heads:
        a = [_dot(jnp.where(lane == h, qd[i], 0.0).astype(BF16), ki_t[i]) for i in seqs]
        for i in seqs:
            am = jnp.where(m_intra, a[i], 0.0).astype(BF16)
            o_intra[i][h] = _dot(am, vs[i][:, h * GLA_DV:(h + 1) * GLA_DV])

    deltas = [[None] * GLA_HEADS for _ in seqs]
    for h in heads:
        for i in seqs:
            v_h = vs[i][:, h * GLA_DV:(h + 1) * GLA_DV]
            v_bd = jnp.concatenate(
                [jnp.concatenate([v_h[ci * GLA_CHUNK:(ci + 1) * GLA_CHUNK] if cj == ci else zero_v
                                  for cj in range(nch)], axis=1) for ci in range(nch)], axis=0)
            deltas[i][h] = _dot(ke_t[i][h * GLA_DK:(h + 1) * GLA_DK, :].astype(BF16), v_bd)

    s_h = [[ss[i][h * GLA_DK:(h + 1) * GLA_DK, :] for h in heads] for i in seqs]
    o_inter = [[None] * nch for _ in seqs]
    for ci in (range(nch - 1, -1, -1) if reverse else range(nch)):
        for i in seqs:
            s_bd = jnp.concatenate(
                [jnp.concatenate([s_h[i][h].astype(BF16) if hj == h else zero_s for hj in heads], axis=1)
                 for h in heads], axis=0)
            o_inter[i][ci] = _dot(qd[i][ci * GLA_CHUNK:(ci + 1) * GLA_CHUNK, :].astype(BF16), s_bd)
            decay = jnp.exp(ends[i][ci])
            s_h[i] = [decay[h * GLA_DK:(h + 1) * GLA_DK] * s_h[i][h]
                      + deltas[i][h][:, ci * GLA_DV:(ci + 1) * GLA_DV] for h in heads]
    outs = [jnp.concatenate(o_intra[i], axis=1) + jnp.concatenate(o_inter[i], axis=0) for i in seqs]
    return outs, [jnp.concatenate(s_h[i], axis=0) for i in seqs]


def _gla_kernel(q_ref, k_ref, v_ref, la_ref, og_ref, s0f_ref, s0b_ref, gh_ref,
                o_ref, sf_ref, sb_ref, s_ref, ob_ref, *, nt):
    p = pl.program_id(1)
    i = pl.program_id(2)
    grp = range(q_ref.shape[0])

    @pl.when(p == 0)
    def _backward():
        @pl.when(i == 0)
        def _():
            s_ref[...] = s0b_ref[...]

        start = pl.multiple_of((nt - 1 - i) * GLA_TILE, GLA_TILE)
        outs, states = _gla_tiles(*[[ref[gi] for gi in grp] for ref in (q_ref, k_ref, v_ref, la_ref, s_ref)], True)
        for gi in grp:
            s_ref[gi] = states[gi]
            ob_ref[gi, pl.ds(start, GLA_TILE), :] = outs[gi]

        @pl.when(i == nt - 1)
        def _():
            sb_ref[...] = s_ref[...]

    @pl.when(p == 1)
    def _forward():
        @pl.when(i == 0)
        def _():
            s_ref[...] = s0f_ref[...]

        start = pl.multiple_of(i * GLA_TILE, GLA_TILE)
        outs, states = _gla_tiles(*[[ref[gi] for gi in grp] for ref in (q_ref, k_ref, v_ref, la_ref, s_ref)], False)
        for gi in grp:
            s_ref[gi] = states[gi]
            o = outs[gi] + ob_ref[gi, pl.ds(start, GLA_TILE), :]
            for h in range(GLA_HEADS):
                sl = slice(h * GLA_DV, (h + 1) * GLA_DV)
                oh = o[:, sl]
                ms = jnp.mean(oh * oh, axis=-1, keepdims=True)
                y = oh * lax.rsqrt(ms + EPS) * gh_ref[...]
                o_ref[gi, :, sl] = (y * _silu(og_ref[gi, :, sl])).astype(BF16)

        @pl.when(i == nt - 1)
        def _():
            sf_ref[...] = s_ref[...]


def _gla(q, k, v, la, og, s0f, s0b, g_head, batch, seq):
    nt = seq // GLA_TILE
    gg = GLA_GROUP
    tile = lambda b, p, i: (b, p * i + (1 - p) * (nt - 1 - i), 0)
    fwd_tile = lambda b, p, i: (b, p * i, 0)
    st = pl.BlockSpec((gg, GLA_QK, GLA_DV), lambda b, p, i: (b, 0, 0))
    return pl.pallas_call(
        functools.partial(_gla_kernel, nt=nt),
        grid=(batch // gg, 2, nt),
        in_specs=[pl.BlockSpec((gg, GLA_TILE, GLA_QK), tile),
                  pl.BlockSpec((gg, GLA_TILE, GLA_QK), tile),
                  pl.BlockSpec((gg, GLA_TILE, GLA_V), tile),
                  pl.BlockSpec((gg, GLA_TILE, GLA_QK), lambda b, p, i: tile(b, p, i)[:2] + (1 - p,)),
                  pl.BlockSpec((gg, GLA_TILE, GLA_V), fwd_tile),
                  st, st, _resident(g_head.shape)],
        out_specs=[pl.BlockSpec((gg, GLA_TILE, GLA_V), fwd_tile), st, st],
        out_shape=[jax.ShapeDtypeStruct((batch, seq, GLA_V), BF16),
                   jax.ShapeDtypeStruct((batch, GLA_QK, GLA_DV), F32),
                   jax.ShapeDtypeStruct((batch, GLA_QK, GLA_DV), F32)],
        scratch_shapes=[pltpu.VMEM((gg, GLA_QK, GLA_DV), F32), pltpu.VMEM((gg, seq, GLA_V), F32)],
        compiler_params=_params(("arbitrary", "arbitrary", "arbitrary")),
        name="gla",
    )(q, k, v, la, og, s0f, s0b, g_head)


def _fnet_kernel(u_ref, ccs_ref, cs_ref, o_ref, ucs_ref, eo_ref, *, seq, scale):
    nseq = u_ref.shape[0]
    half = seq // 2
    blk = min(FN_FOLD, half)
    rows = min(FN_ROWS, seq)
    rr = lax.broadcasted_iota(jnp.int32, (blk, blk + FN_PAD), 0)
    cc = lax.broadcasted_iota(jnp.int32, (blk, blk + FN_PAD), 1)
    mirror = jnp.where(cc == blk - rr, 1.0, 0.0).astype(BF16)
    sign = 1.0 - 2.0 * (lax.broadcasted_iota(jnp.int32, (rows, 1), 0) % 2).astype(F32)
    for s in range(nseq):
        for part in range(2):
            ucs_ref[s, part, seq:seq + FN_PAD, :] = jnp.zeros((FN_PAD, FN_W), BF16)
        for gi in range(FN_GROUPS):
            sl = slice(gi * FN_CH, (gi + 1) * FN_CH)
            t = _dot(u_ref[s, :, sl], ccs_ref[...])
            ucs_ref[s, 0, 0:seq, sl] = t[:, :FN_CH].astype(BF16)
            ucs_ref[s, 1, 0:seq, sl] = t[:, FN_CH:].astype(BF16)
    for s in range(nseq):
        for j in range(half // blk):
            w0 = seq - (j + 1) * blk
            for part, sgn in ((0, 1.0), (1, -1.0)):
                own = ucs_ref[s, part, j * blk:(j + 1) * blk, :].astype(F32)
                partner = _dot(mirror, ucs_ref[s, part, w0:w0 + blk + FN_PAD, :])
                eo_ref[s, part * half + j * blk:part * half + (j + 1) * blk, :] = (own + sgn * partner).astype(BF16)
    for s in range(nseq):
        mid = ucs_ref[s, 0, half:half + 1, :].astype(F32)
        for j in range(seq // rows):
            rs = slice(j * rows, (j + 1) * rows)
            o_ref[s, rs, :] = ((_dot(cs_ref[rs, :], eo_ref[s]) + sign * mid) * scale).astype(BF16)


def _dft_tables(seq):
    def cs(n, cols):
        k = np.arange(n, dtype=np.int64)
        t = np.arange(cols, dtype=np.int64)
        ang = 2.0 * np.pi * ((k[:, None] * t[None, :]) % n).astype(np.float64) / n
        return np.cos(ang), np.sin(ang)
    ct, st = cs(seq, seq // 2)
    cc, sc = cs(FN_CH, FN_CH)
    to = lambda a: jnp.asarray(a.astype(np.float32)).astype(BF16)
    return to(np.concatenate([ct, -st], axis=1)), to(np.concatenate([cc, sc], axis=1))


def _fnet(u, batch, seq):
    cs, ccs = _dft_tables(seq)
    nseq = max(1, FN_TOKENS // seq)
    blk = pl.BlockSpec((nseq, seq, FN_W), lambda b: (b, 0, 0))
    return pl.pallas_call(
        functools.partial(_fnet_kernel, seq=seq, scale=float((seq * FN_CH) ** -0.5)),
        grid=(batch // nseq,),
        in_specs=[blk, _resident(ccs.shape), _resident(cs.shape)],
        out_specs=blk,
        out_shape=jax.ShapeDtypeStruct((batch, seq, FN_W), BF16),
        scratch_shapes=[pltpu.VMEM((nseq, 2, seq + FN_PAD, FN_W), BF16),
                        pltpu.VMEM((nseq, seq, FN_W), BF16)],
        compiler_params=_params(("arbitrary",)),
        name="fnet",
    )(u, ccs, cs)


def _rope128(x, cos, sin_signed, odd):
    back = pltpu.roll(x, 16, 1)
    fwd = pltpu.roll(x, 112, 1)
    return x * cos + jnp.where(odd, back, fwd) * sin_signed


def _l1_in_kernel(*refs, rope):
    if rope:
        x_ref, m_ref, g_ref, w_ref, cos_ref, sin_ref, q_ref, k_ref, v_ref = refs
        odd = (lax.broadcasted_iota(jnp.int32, (SUB_PROJ, LANES), 1) // 16) % 2 == 1
    else:
        x_ref, m_ref, g_ref, w_ref, q_ref, k_ref, v_ref = refs

    def project(h, rows, dst, col0, width, scale):
        for j in range(width // 256):
            y = _dot(h, w_ref[:, col0 + 256 * j:col0 + 256 * (j + 1)])
            if scale != 1.0:
                y = y * scale
            for half in range(2):
                yh = y[:, LANES * half:LANES * (half + 1)]
                if rope:
                    yh = _rope128(yh, cos_ref[rows, :], sin_ref[rows, :], odd)
                lo = 256 * j + LANES * half
                dst[rows, lo:lo + LANES] = yh.astype(dst.dtype)

    for r0 in range(0, x_ref.shape[0], SUB_PROJ):
        rows = slice(r0, r0 + SUB_PROJ)
        h = _ada_norm(x_ref[rows, :], g_ref[...], m_ref[0, 3:4, :], m_ref[0, 4:5, :]).astype(BF16)
        project(h, rows, q_ref, 0, SWA_Q, SWA_HD ** -0.5 * LOG2E)
        project(h, rows, k_ref, SWA_Q, SWA_KV, 1.0)
        v_ref[rows, :] = _dot(h, w_ref[:, SWA_Q + SWA_KV:SWA_Q + 2 * SWA_KV])


def _l1_in_proj(x, mods, bfn, g, w, rope_tabs=None, tiles_per_seq=None):
    m, d = x.shape
    row = lambda i: (i, 0)
    ins = [x, mods, g, w]
    specs = [pl.BlockSpec((TM, d), row),
             pl.BlockSpec((1, MOD_ROWS, d), lambda i: (bfn(i), 0, 0)),
             _resident(g.shape), _resident(w.shape)]
    if rope_tabs is not None:
        ins += list(rope_tabs)
        specs += [pl.BlockSpec((TM, LANES), lambda i: (i % tiles_per_seq, 0))] * 2
    return pl.pallas_call(
        functools.partial(_l1_in_kernel, rope=rope_tabs is not None),
        grid=(m // TM,),
        in_specs=specs,
        out_specs=[pl.BlockSpec((TM, SWA_Q), row), pl.BlockSpec((TM, SWA_KV), row), pl.BlockSpec((TM, SWA_KV), row)],
        out_shape=[jax.ShapeDtypeStruct((m, SWA_Q), BF16),
                   jax.ShapeDtypeStruct((m, SWA_KV), F32),
                   jax.ShapeDtypeStruct((m, SWA_KV), F32)],
        compiler_params=_params(("arbitrary",)),
        name="l1_in_proj",
    )(*ins)


def _rope_tables(seq):
    rows = seq // GRID_W
    row = np.repeat(np.arange(rows), GRID_W).astype(np.float64)
    col = (np.arange(rows * GRID_W) % GRID_W).astype(np.float64)
    n_freq = SWA_HD // 4
    inv = ROPE_BASE ** (-np.arange(n_freq, dtype=np.float64) / n_freq)
    ar = row[:, None] * inv
    ac = col[:, None] * inv
    ang = np.concatenate([ar, ar, ac, ac], axis=-1)
    sign = np.where((np.arange(SWA_HD) // n_freq) % 2 == 1, 1.0, -1.0)
    two = lambda t: jnp.asarray(np.concatenate([t, t], axis=-1).astype(np.float32))
    return two(np.cos(ang)), two(np.sin(ang) * sign)


def _attend(q_ref, ks, vs, biases, sink_ref, o_ref):
    kcat = ks[0] if len(ks) == 1 else jnp.concatenate(ks, axis=0)
    vcat = vs[0] if len(vs) == 1 else jnp.concatenate(vs, axis=0)
    n = kcat.shape[0]
    r = q_ref.shape[0]
    nblk = n // LANES
    low = lax.broadcasted_iota(jnp.int32, (n, LANES), 1) < SWA_HD
    ones_lo = jnp.where(low, 1.0, 0.0)
    ones_hi = jnp.where(low, 0.0, 1.0)
    low_out = lax.broadcasted_iota(jnp.int32, (2 * r, LANES), 1) < SWA_HD
    top = lax.broadcasted_iota(jnp.int32, (2 * r, 1), 0) < r
    biases = {t: jnp.concatenate([b, b], axis=0) for t, b in biases.items()}
    nt_dims = (((1,), (1,)), ((), ()))

    def scores(g):
        sl = slice(LANES * (g // 2), LANES * (g // 2 + 1))
        kcol, vcol = kcat[:, sl], vcat[:, sl]
        ksw, vsw = pltpu.roll(kcol, SWA_HD, 1), pltpu.roll(vcol, SWA_HD, 1)
        if g % 2 == 0:
            k_lo, k_hi = jnp.where(low, kcol, 0.0), jnp.where(low, 0.0, ksw)
            v_lo, v_hi = jnp.where(low, vcol, 0.0), jnp.where(low, 0.0, vsw)
        else:
            k_lo, k_hi = jnp.where(low, ksw, 0.0), jnp.where(low, 0.0, kcol)
            v_lo, v_hi = jnp.where(low, vsw, 0.0), jnp.where(low, 0.0, vcol)
        k_bd = jnp.concatenate([k_lo, k_hi], axis=0).astype(BF16)
        v_bd = jnp.concatenate([jnp.concatenate([v_lo, ones_lo], axis=1),
                                jnp.concatenate([v_hi, ones_hi], axis=1)], axis=0).astype(BF16)
        qs = jnp.concatenate([q_ref[:, LANES * (2 * g):LANES * (2 * g + 1)],
                              q_ref[:, LANES * (2 * g + 1):LANES * (2 * g + 2)]], axis=0)
        return lax.dot_general(qs, k_bd, nt_dims, preferred_element_type=F32), v_bd

    nxt = scores(0)
    for g in range(SWA_KV_HEADS):
        s, v_bd = nxt
        if g + 1 < SWA_KV_HEADS:
            nxt = scores(g + 1)
        probs, sink_terms = [], []
        for e in range(2):
            blocks = [s[:, e * n + LANES * t:e * n + LANES * (t + 1)] for t in range(nblk)]
            for t, b in biases.items():
                blocks[t] = blocks[t] + b
            mx = blocks[0]
            for b in blocks[1:]:
                mx = jnp.maximum(mx, b)
            sk = jnp.where(top, sink_ref[4 * g + e], sink_ref[4 * g + 2 + e]) * LOG2E
            m = jnp.maximum(jnp.max(mx, axis=-1, keepdims=True), sk)
            probs += [jnp.exp2(b - m).astype(BF16) for b in blocks]
            sink_terms.append(jnp.exp2(sk - m))
        res = _dot(jnp.concatenate(probs, axis=1), v_bd)
        denom = res[:, LANES:2 * LANES] + jnp.where(low_out, sink_terms[0], sink_terms[1])
        out = (res[:, 0:LANES] / denom).astype(o_ref.dtype)
        o_ref[:, LANES * (2 * g):LANES * (2 * g + 1)] = out[0:r]
        o_ref[:, LANES * (2 * g + 1):LANES * (2 * g + 2)] = out[r:2 * r]


def _ctx_attn_kernel(sink_ref, q_ref, k_ref, v_ref, o_ref, *, seq):
    for si in range(CTX_SEQS):
        rows = slice(si * seq, (si + 1) * seq)
        _attend(q_ref.at[rows, :], [k_ref[rows, :]], [v_ref[rows, :]], {}, sink_ref, o_ref.at[rows, :])


def _ctx_attention(q, k, v, sink, batch, seq):
    blk = lambda w: pl.BlockSpec((CTX_SEQS * seq, w), lambda b: (b, 0))
    return pl.pallas_call(
        functools.partial(_ctx_attn_kernel, seq=seq),
        grid=(batch // CTX_SEQS,),
        in_specs=[pl.BlockSpec(memory_space=pltpu.SMEM), blk(SWA_Q), blk(SWA_KV), blk(SWA_KV)],
        out_specs=blk(SWA_Q),
        out_shape=jax.ShapeDtypeStruct((batch * seq, SWA_Q), BF16),
        compiler_params=_params(("arbitrary",)),
        name="ctx_attention",
    )(sink, q, k, v)


def _lat_attn_kernel(sink_ref, q_ref, ck_ref, cv_ref, *refs, nsteps, n_ctx):
    nq = ATTN_QBLOCKS
    k_refs, v_refs, o_ref = refs[:nq + 2], refs[nq + 2:2 * nq + 4], refs[2 * nq + 4]
    p = pl.program_id(1)
    blk = ATTN_BLOCK
    row = lax.broadcasted_iota(jnp.int32, (blk, blk), 0)
    col = lax.broadcasted_iota(jnp.int32, (blk, blk), 1)
    off_first = jnp.where(p > 0, 0, blk)
    off_last = jnp.where(p < nsteps - 1, 0, blk)
    first = n_ctx // LANES
    ck, cv = ck_ref[...], cv_ref[...]
    kb = [r[...] for r in k_refs]
    vb = [r[...] for r in v_refs]
    for qi in range(nq):
        rows = slice(qi * blk, (qi + 1) * blk)
        off_prev = off_first if qi == 0 else 0
        off_next = off_last if qi == nq - 1 else 0
        bias_prev = jnp.where(col - row >= off_prev, 0.0, -jnp.inf)
        bias_next = jnp.where(row - col >= off_next, 0.0, -jnp.inf)
        _attend(q_ref.at[rows, :], [ck] + kb[qi:qi + 3], [cv] + vb[qi:qi + 3],
                {first: bias_prev, first + 2: bias_next}, sink_ref, o_ref.at[rows, :])


def _lat_attention(q, k, v, ctx_k, ctx_v, sink, batch, seq, n_ctx):
    nb = seq // ATTN_BLOCK
    nq = ATTN_QBLOCKS
    nsteps = nb // nq
    kv = lambda j: pl.BlockSpec((ATTN_BLOCK, SWA_KV),
                                lambda b, p: (b * nb + jnp.clip(nq * p - 1 + j, 0, nb - 1), 0))
    blocks = [kv(j) for j in range(nq + 2)]
    ctx = pl.BlockSpec((n_ctx, SWA_KV), lambda b, p: (b, 0))
    qo = pl.BlockSpec((nq * ATTN_BLOCK, SWA_Q), lambda b, p: (b * nsteps + p, 0))
    return pl.pallas_call(
        functools.partial(_lat_attn_kernel, nsteps=nsteps, n_ctx=n_ctx),
        grid=(batch, nsteps),
        in_specs=[pl.BlockSpec(memory_space=pltpu.SMEM), qo, ctx, ctx] + blocks + blocks,
        out_specs=qo,
        out_shape=jax.ShapeDtypeStruct((batch * seq, SWA_Q), BF16),
        compiler_params=_params(("arbitrary", "arbitrary")),
        name="lat_attention",
    )(sink, q, ctx_k, ctx_v, *([k] * (nq + 2)), *([v] * (nq + 2)))


def kernel(x_prompt, x_sample, state_l0_gla_fwd, state_l0_gla_bwd, cache_l1_k, cache_l1_v, c, c_ctx, mod_w, mod_b, norm_g, ffn_w1, ffn_w3, ffn_w2, l0_w_in, l0_w_gf, l0_b_gf, l0_w_gb, l0_b_gb, l0_g_head, l0_w_out, l1_w_in, l1_sink, l1_w_out, final_g):
    bp, tp, d = x_prompt.shape
    bs, ts, _ = x_sample.shape
    n_ctx = cache_l1_k.shape[1]
    depth = mod_w.shape[0]
    ctx_row = bs

    ffn_stacks = (ffn_w1, ffn_w3, ffn_w2)
    ffn_w = tuple(s[0, 0].astype(BF16) for s in ffn_stacks)
    gate_lo = 2 * GLA_QK + 2 * GLA_V
    gate_hi = gate_lo + 2 * GLA_GATE_RANK
    l0_wm = jnp.concatenate([l0_w_in[:, :gate_lo], l0_w_in[:, gate_hi:]], axis=1).astype(BF16)
    l0_wg = jnp.pad(l0_w_in[:, gate_lo:gate_hi], ((0, 0), (0, LANES - 2 * GLA_GATE_RANK))).astype(BF16)
    wg2 = jnp.zeros((LANES, 2 * GLA_QK), F32)
    wg2 = wg2.at[:GLA_GATE_RANK, :GLA_QK].set(l0_w_gf)
    wg2 = wg2.at[GLA_GATE_RANK:2 * GLA_GATE_RANK, GLA_QK:].set(l0_w_gb).astype(BF16)
    bg = jnp.concatenate([l0_b_gf, l0_b_gb])[None, :]
    g_head = l0_g_head[None, :]
    sink = l1_sink.reshape(-1)
    fin_g = final_g[None, :]

    cond = jnp.zeros((MOD_ROWS, d), F32).at[:bs].set(c).at[ctx_row].set(c_ctx)
    mods = _modulation(cond, mod_w, mod_b)
    mods = mods.reshape(depth, MOD_ROWS, N_MOD, d)
    mods = jnp.pad(mods, ((0, 0), (0, 0), (0, MOD_ROWS - N_MOD), (0, 0)))

    xp = x_prompt.reshape(bp * tp, d)
    xs = x_sample.reshape(bs * ts, d)
    tiles_s = ts // TM
    bfn_p = lambda i: ctx_row
    bfn_s = lambda i: i // tiles_s
    zero_state = jnp.zeros((bp, GLA_QK, GLA_DV), F32)
    s0f = state_l0_gla_fwd.reshape(bs, GLA_QK, GLA_DV)
    s0b = state_l0_gla_bwd.reshape(bs, GLA_QK, GLA_DV)
    rope_tabs = _rope_tables(ts)

    new_state = []
    for layer in range(depth):
        ml = mods[layer]
        ga, gm, gb = (norm_g[layer, r][None, :] for r in range(3))
        last = layer == depth - 1
        xp, proj_w = _half_ffn(xp, ml, bfn_p, ga, *ffn_w, mrow=0,
                               cast=[(w, ()) for w in (l0_w_out, l1_w_in, l1_w_out)] if layer == 0 else None)
        if layer == 0:
            l0_wo, l1_wi, l1_wo = proj_w
        xs, ffn_w_b = _half_ffn(xs, ml, bfn_s, ga, *ffn_w, mrow=0, cast=[(s, (layer, 1)) for s in ffn_stacks])
        if layer % 2 == 0:
            outs = []
            for x, bfn, nb, t, sf0, sb0 in ((xp, bfn_p, bp, tp, zero_state, zero_state), (xs, bfn_s, bs, ts, s0f, s0b)):
                q, k, v, og, la, u = _l0_in_proj(x, ml, bfn, gm, l0_wm, l0_wg, wg2, bg)
                seq3 = lambda a: a.reshape(nb, t, a.shape[-1])
                gla_out, s_f, s_b = _gla(seq3(q), seq3(k), seq3(v), seq3(la), seq3(og), sf0, sb0, g_head, nb, t)
                gla_out = gla_out.reshape(nb * t, GLA_V)
                fn_out = _fnet(seq3(u), nb, t).reshape(nb * t, FN_W)
                outs.append(((gla_out, fn_out), s_f, s_b))
            (mix_p, s_f, s_b), (mix_s, _, _) = outs
            new_state += [s_f.reshape(bp, GLA_HEADS, GLA_DK, GLA_DV), s_b.reshape(bp, GLA_HEADS, GLA_DK, GLA_DV)]
            w_out = l0_wo
        else:
            qp, kp, vp = _l1_in_proj(xp, ml, bfn_p, gm, l1_wi)
            mix_p = (_ctx_attention(qp, kp, vp, sink, bp, tp),)
            qs, ks, vs = _l1_in_proj(xs, ml, bfn_s, gm, l1_wi, rope_tabs, tiles_s)
            mix_s = (_lat_attention(qs, ks, vs, cache_l1_k.reshape(bs * n_ctx, SWA_KV),
                                    cache_l1_v.reshape(bs * n_ctx, SWA_KV), sink, bs, ts, n_ctx),)
            new_state += [kp.reshape(bp, tp, SWA_KV_HEADS, SWA_HD), vp.reshape(bp, tp, SWA_KV_HEADS, SWA_HD)]
            w_out = l1_wo
        fin = fin_g if last else None
        xp, _ = _half_ffn(xp, ml, bfn_p, gb, *ffn_w_b, mrow=6, mixes=mix_p, w_out=w_out, final_g=fin)
        xs, ffn_w = _half_ffn(xs, ml, bfn_s, gb, *ffn_w_b, mrow=6, mixes=mix_s, w_out=w_out, final_g=fin,
                              cast=None if last else [(s, (layer + 1, 0)) for s in ffn_stacks])
    return (xp.reshape(bp, tp, d), xs.reshape(bs, ts, d), *new_state)
```

```python
import functools

import numpy as np
import jax
import jax.numpy as jnp
from jax import lax
from jax.experimental import pallas as pl
from jax.experimental.pallas import tpu as pltpu

F32 = jnp.float32
BF16 = jnp.bfloat16

D_MODEL = 1024
FFN_DIM = 2816
N_MOD = 9
EPS = 1e-6
GRID_W = 64
GLA_HEADS = 4
GLA_DK = 64
GLA_DV = 128
GLA_GATE_RANK = 16
GLA_TAU = 16.0
GLA_CHUNK = 64
GLA_QK = GLA_HEADS * GLA_DK
GLA_V = GLA_HEADS * GLA_DV
FN_GROUPS = 4
FN_CH = 128
FN_W = FN_GROUPS * FN_CH
SWA_HEADS = 16
SWA_KV_HEADS = 4
SWA_HD = 64
SWA_Q = SWA_HEADS * SWA_HD
SWA_KV = SWA_KV_HEADS * SWA_HD
WINDOW = 128
ATTN_BLOCK = 128
ATTN_QBLOCKS = 8
CTX_SEQS = 2
ROPE_BASE = 10000.0
LOG2E = float(np.log2(np.e))

LANES = 128
VMEM_LIMIT = 56 * 1024 * 1024

TM = 1024
SUB = 512
SUB_PROJ = 256
SUB_PROJ_L0 = 512
TF = 256
GLA_TILE = 256
GLA_GROUP = 4
FN_ROWS = 256
FN_TOKENS = 2048
FN_PAD = 16
FN_FOLD = 128
MOD_ROWS = 16
MOD_TN = 1024


def _params(sem):
    return pltpu.CompilerParams(dimension_semantics=sem, vmem_limit_bytes=VMEM_LIMIT)


def _resident(shape):
    nd = len(shape)
    return pl.BlockSpec(shape, lambda *_: (0,) * nd, pipeline_mode=pl.Buffered(1))


def _dot(a, b):
    return jnp.dot(a, b, preferred_element_type=F32)


def _silu(x):
    return x * jax.nn.sigmoid(x)


def _ada_norm(x, g, shift, scale):
    ms = jnp.mean(x * x, axis=-1, keepdims=True)
    return (x * lax.rsqrt(ms + EPS)) * (g * (1.0 + scale)) + shift


def _mod_kernel(c_ref, w_ref, b_ref, o_ref):
    c = c_ref[...]
    s = _silu(c).astype(BF16)
    o_ref[0] = _dot(s, w_ref[0].astype(BF16)) + b_ref[0]


def _modulation(cond, mod_w, mod_b):
    depth, d, n = mod_w.shape
    return pl.pallas_call(
        _mod_kernel,
        grid=(depth, n // MOD_TN),
        in_specs=[
            pl.BlockSpec((MOD_ROWS, d), lambda l, j: (0, 0)),
            pl.BlockSpec((1, d, MOD_TN), lambda l, j: (l, 0, j)),
            pl.BlockSpec((1, 1, MOD_TN), lambda l, j: (l, 0, j)),
        ],
        out_specs=pl.BlockSpec((1, MOD_ROWS, MOD_TN), lambda l, j: (l, 0, j)),
        out_shape=jax.ShapeDtypeStruct((depth, MOD_ROWS, n), F32),
        compiler_params=_params(("arbitrary", "arbitrary")),
        name="modulation",
    )(cond, mod_w, mod_b.reshape(depth, 1, n))


def _ffn_kernel(*refs, n_mix, mrow, final, n_cast):
    x_ref = refs[0]
    mix_refs = refs[1:1 + n_mix]
    pos = 1 + n_mix
    if n_mix:
        wo_ref = refs[pos]
        pos += 1
    m_ref, g_ref, w1_ref, w3_ref, w2_ref = refs[pos:pos + 5]
    pos += 5
    if final:
        fg_ref = refs[pos]
        pos += 1
    cast_in = refs[pos:pos + n_cast]
    pos += n_cast
    o_ref = refs[pos]
    cast_out = refs[pos + 1:pos + 1 + n_cast]
    pos += 1 + n_cast
    acc_ref, h_ref = refs[pos], refs[pos + 1]
    for src, dst in zip(cast_in, cast_out):
        dst[...] = src[...].astype(BF16)

    def prologue(rows):
        x = x_ref[rows, :]
        if n_mix:
            off = 0
            mix = None
            for r in mix_refs:
                w = r.shape[1]
                t = _dot(r[rows, :], wo_ref[off:off + w, :])
                mix = t if mix is None else mix + t
                off += w
            x = x + m_ref[0, 5:6, :] * mix
        o_ref[rows, :] = x
        h = _ada_norm(x, g_ref[...], m_ref[0, mrow:mrow + 1, :], m_ref[0, mrow + 1:mrow + 2, :])
        h_ref[rows, :] = h.astype(BF16)

    def chunk(rows, j):
        h = h_ref[rows, :]
        a = _dot(h, w1_ref[:, j * TF:(j + 1) * TF])
        b = _dot(h, w3_ref[:, j * TF:(j + 1) * TF])
        t = _dot((_silu(a) * b).astype(BF16), w2_ref[j * TF:(j + 1) * TF, :])
        if j == 0:
            acc_ref[rows, :] = t
        else:
            acc_ref[rows, :] += t

    def epilogue(rows):
        y = o_ref[rows, :] + (0.5 * m_ref[0, mrow + 2:mrow + 3, :]) * acc_ref[rows, :]
        if final:
            ms = jnp.mean(y * y, axis=-1, keepdims=True)
            y = y * lax.rsqrt(ms + EPS) * fg_ref[...]
        o_ref[rows, :] = y

    subs = [slice(r0, r0 + SUB) for r0 in range(0, x_ref.shape[0], SUB)]
    for rows in subs:
        prologue(rows)
    for j in range(FFN_DIM // TF):
        for rows in subs:
            chunk(rows, j)
    for rows in subs:
        epilogue(rows)


def _half_ffn(x, mods, bfn, g, w1, w3, w2, *, mrow, mixes=(), w_out=None, final_g=None, cast=None):
    m, d = x.shape
    steps = m // TM
    row = lambda i: (i, 0)
    ins = [x]
    specs = [pl.BlockSpec((TM, d), row)]
    for a in mixes:
        ins.append(a)
        specs.append(pl.BlockSpec((TM, a.shape[1]), row))
    if mixes:
        ins.append(w_out)
        specs.append(_resident(w_out.shape))
    ins += [mods, g, w1, w3, w2]
    specs += [pl.BlockSpec((1, MOD_ROWS, d), lambda i: (bfn(i), 0, 0)),
              _resident(g.shape), _resident(w1.shape), _resident(w3.shape), _resident(w2.shape)]
    if final_g is not None:
        ins.append(final_g)
        specs.append(_resident(final_g.shape))
    out_specs = [pl.BlockSpec((TM, d), row)]
    out_shape = [jax.ShapeDtypeStruct((m, d), F32)]
    cast = tuple(cast or ())
    for s, lead, *opt in cast:
        rows, cols = s.shape[-2] // steps, (opt[0] if opt else s.shape[-1])
        ins.append(s)
        specs.append(pl.BlockSpec((None,) * len(lead) + (rows, cols), lambda i, lead=tuple(lead): lead + (i, 0)))
        out_specs.append(pl.BlockSpec((rows, cols), row))
        out_shape.append(jax.ShapeDtypeStruct((s.shape[-2], cols), BF16))
    outs = pl.pallas_call(
        functools.partial(_ffn_kernel, n_mix=len(mixes), mrow=mrow, final=final_g is not None,
                          n_cast=len(cast)),
        grid=(steps,),
        in_specs=specs,
        out_specs=out_specs,
        out_shape=out_shape,
        scratch_shapes=[pltpu.VMEM((TM, d), F32), pltpu.VMEM((TM, d), BF16)],
        compiler_params=_params(("arbitrary",)),
        name="half_ffn",
    )(*ins)
    return outs[0], tuple(outs[1:])


def _l0_in_kernel(x_ref, m_ref, g_ref, wm_ref, wu_ref, wg_ref, wg2_ref, bg_ref,
                  q_ref, k_ref, v_ref, og_ref, la_ref, u_ref):
    for r0 in range(0, x_ref.shape[0], SUB_PROJ_L0):
        rows = slice(r0, r0 + SUB_PROJ_L0)
        h = _ada_norm(x_ref[rows, :], g_ref[...], m_ref[0, 3:4, :], m_ref[0, 4:5, :]).astype(BF16)
        q_ref[rows, :] = _dot(h, wm_ref[:, 0:256]) * (GLA_DK ** -0.5)
        lr = _dot(h, wg_ref[...]).astype(BF16)
        k_ref[rows, :] = _dot(h, wm_ref[:, 256:512])
        z = _dot(lr, wg2_ref[...]) + bg_ref[...]
        log_sig = jnp.minimum(z, 0.0) - jnp.log(1.0 + jnp.exp(-jnp.abs(z)))
        la_ref[rows, :] = log_sig * (1.0 / GLA_TAU)
        v_ref[rows, :] = _dot(h, wm_ref[:, 512:1024]).astype(BF16)
        og_ref[rows, :] = _dot(h, wm_ref[:, 1024:1536])
        u_ref[rows, :] = _dot(h, wu_ref[...]).astype(BF16)


def _l0_in_proj(x, mods, bfn, g, wm, wu, wg, wg2, bg):
    m, d = x.shape
    row = lambda i: (i, 0)
    outs = [(GLA_QK, F32), (GLA_QK, F32), (GLA_V, BF16), (GLA_V, F32), (2 * GLA_QK, F32), (FN_W, BF16)]
    return pl.pallas_call(
        _l0_in_kernel,
        grid=(m // TM,),
        in_specs=[pl.BlockSpec((TM, d), row),
                  pl.BlockSpec((1, MOD_ROWS, d), lambda i: (bfn(i), 0, 0)),
                  _resident(g.shape), _resident(wm.shape), _resident(wu.shape), _resident(wg.shape),
                  _resident(wg2.shape), _resident(bg.shape)],
        out_specs=[pl.BlockSpec((TM, w), row) for w, _ in outs],
        out_shape=[jax.ShapeDtypeStruct((m, w), dt) for w, dt in outs],
        compiler_params=_params(("arbitrary",)),
        name="l0_in_proj",
    )(x, mods, g, wm, wu, wg, wg2, bg)


def _gla_tiles(qs, ks, vs, gs, ss, reverse):
    tt = GLA_TILE
    nch = tt // GLA_CHUNK
    seqs = range(len(qs))
    heads = range(GLA_HEADS)
    r = lax.broadcasted_iota(jnp.int32, (tt, tt), 0)
    c = lax.broadcasted_iota(jnp.int32, (tt, tt), 1)
    same = (r // GLA_CHUNK) == (c // GLA_CHUNK)
    m_intra = jnp.logical_and(same, (c >= r) if reverse else (c <= r))
    tri = jnp.where(m_intra, 1.0, 0.0).astype(BF16)
    cc = lax.broadcasted_iota(jnp.int32, (GLA_QK, tt), 1) // GLA_CHUNK
    lane = lax.broadcasted_iota(jnp.int32, (tt, GLA_QK), 1) // GLA_DK
    end_col = [ci * GLA_CHUNK + (0 if reverse else GLA_CHUNK - 1) for ci in range(nch)]
    zero_v = jnp.zeros((GLA_CHUNK, GLA_DV), BF16)
    zero_s = jnp.zeros((GLA_DK, GLA_DV), BF16)

    g_hi = [g.astype(BF16) for g in gs]
    g_lo = [(gs[i] - g_hi[i].astype(F32)).astype(BF16) for i in seqs]
    b = [_dot(tri, g_hi[i]) + _dot(tri, g_lo[i]) for i in seqs]
    qd = [qs[i] * jnp.exp(b[i]) for i in seqs]
    b_t = [x.T for x in b]
    k_t = [x.T for x in ks]
    ends = [[b_t[i][:, e:e + 1] for e in end_col] for i in seqs]
    ki_t, ke_t = [], []
    for i in seqs:
        bl_t = ends[i][nch - 1]
        for ci in range(nch - 2, -1, -1):
            bl_t = jnp.where(cc == ci, ends[i][ci], bl_t)
        ki_t.append((k_t[i] * jnp.exp(-b_t[i])).astype(BF16))
        ke_t.append(k_t[i] * jnp.exp(bl_t - b_t[i]))

    o_intra = [[None] * GLA_HEADS for _ in seqs]
    for h in heads:
        a = [_dot(jnp.where(lane == h, qd[i], 0.0).astype(BF16), ki_t[i]) for i in seqs]
        for i in seqs:
            am = jnp.where(m_intra, a[i], 0.0).astype(BF16)
            o_intra[i][h] = _dot(am, vs[i][:, h * GLA_DV:(h + 1) * GLA_DV])

    deltas = [[None] * GLA_HEADS for _ in seqs]
    for h in heads:
        for i in seqs:
            v_h = vs[i][:, h * GLA_DV:(h + 1) * GLA_DV]
            v_bd = jnp.concatenate(
                [jnp.concatenate([v_h[ci * GLA_CHUNK:(ci + 1) * GLA_CHUNK] if cj == ci else zero_v
                                  for cj in range(nch)], axis=1) for ci in range(nch)], axis=0)
            deltas[i][h] = _dot(ke_t[i][h * GLA_DK:(h + 1) * GLA_DK, :].astype(BF16), v_bd)

    s_h = [[ss[i][h * GLA_DK:(h + 1) * GLA_DK, :] for h in heads] for i in seqs]
    o_inter = [[None] * nch for _ in seqs]
    for ci in (range(nch - 1, -1, -1) if reverse else range(nch)):
        for i in seqs:
            s_bd = jnp.concatenate(
                [jnp.concatenate([s_h[i][h].astype(BF16) if hj == h else zero_s for hj in heads], axis=1)
                 for h in heads], axis=0)
            o_inter[i][ci] = _dot(qd[i][ci * GLA_CHUNK:(ci + 1) * GLA_CHUNK, :].astype(BF16), s_bd)
            decay = jnp.exp(ends[i][ci])
            s_h[i] = [decay[h * GLA_DK:(h + 1) * GLA_DK] * s_h[i][h]
                      + deltas[i][h][:, ci * GLA_DV:(ci + 1) * GLA_DV] for h in heads]
    outs = [jnp.concatenate(o_intra[i], axis=1) + jnp.concatenate(o_inter[i], axis=0) for i in seqs]
    return outs, [jnp.concatenate(s_h[i], axis=0) for i in seqs]


def _gla_kernel(q_ref, k_ref, v_ref, la_ref, og_ref, s0f_ref, s0b_ref, gh_ref,
                o_ref, sf_ref, sb_ref, s_ref, ob_ref, *, nt):
    p = pl.program_id(1)
    i = pl.program_id(2)
    grp = range(q_ref.shape[0])

    @pl.when(p == 0)
    def _backward():
        @pl.when(i == 0)
        def _():
            s_ref[...] = s0b_ref[...]

        start = pl.multiple_of((nt - 1 - i) * GLA_TILE, GLA_TILE)
        outs, states = _gla_tiles(*[[ref[gi] for gi in grp] for ref in (q_ref, k_ref, v_ref, la_ref, s_ref)], True)
        for gi in grp:
            s_ref[gi] = states[gi]
            ob_ref[gi, pl.ds(start, GLA_TILE), :] = outs[gi]

        @pl.when(i == nt - 1)
        def _():
            sb_ref[...] = s_ref[...]

    @pl.when(p == 1)
    def _forward():
        @pl.when(i == 0)
        def _():
            s_ref[...] = s0f_ref[...]

        start = pl.multiple_of(i * GLA_TILE, GLA_TILE)
        outs, states = _gla_tiles(*[[ref[gi] for gi in grp] for ref in (q_ref, k_ref, v_ref, la_ref, s_ref)], False)
        for gi in grp:
            s_ref[gi] = states[gi]
            o = outs[gi] + ob_ref[gi, pl.ds(start, GLA_TILE), :]
            for h in range(GLA_HEADS):
                sl = slice(h * GLA_DV, (h + 1) * GLA_DV)
                oh = o[:, sl]
                ms = jnp.mean(oh * oh, axis=-1, keepdims=True)
                y = oh * lax.rsqrt(ms + EPS) * gh_ref[...]
                o_ref[gi, :, sl] = (y * _silu(og_ref[gi, :, sl])).astype(BF16)

        @pl.when(i == nt - 1)
        def _():
            sf_ref[...] = s_ref[...]


def _gla(q, k, v, la, og, s0f, s0b, g_head, batch, seq):
    nt = seq // GLA_TILE
    gg = GLA_GROUP
    tile = lambda b, p, i: (b, p * i + (1 - p) * (nt - 1 - i), 0)
    fwd_tile = lambda b, p, i: (b, p * i, 0)
    st = pl.BlockSpec((gg, GLA_QK, GLA_DV), lambda b, p, i: (b, 0, 0))
    return pl.pallas_call(
        functools.partial(_gla_kernel, nt=nt),
        grid=(batch // gg, 2, nt),
        in_specs=[pl.BlockSpec((gg, GLA_TILE, GLA_QK), tile),
                  pl.BlockSpec((gg, GLA_TILE, GLA_QK), tile),
                  pl.BlockSpec((gg, GLA_TILE, GLA_V), tile),
                  pl.BlockSpec((gg, GLA_TILE, GLA_QK), lambda b, p, i: tile(b, p, i)[:2] + (1 - p,)),
                  pl.BlockSpec((gg, GLA_TILE, GLA_V), fwd_tile),
                  st, st, _resident(g_head.shape)],
        out_specs=[pl.BlockSpec((gg, GLA_TILE, GLA_V), fwd_tile), st, st],
        out_shape=[jax.ShapeDtypeStruct((batch, seq, GLA_V), BF16),
                   jax.ShapeDtypeStruct((batch, GLA_QK, GLA_DV), F32),
                   jax.ShapeDtypeStruct((batch, GLA_QK, GLA_DV), F32)],
        scratch_shapes=[pltpu.VMEM((gg, GLA_QK, GLA_DV), F32), pltpu.VMEM((gg, seq, GLA_V), F32)],
        compiler_params=_params(("arbitrary", "arbitrary", "arbitrary")),
        name="gla",
    )(q, k, v, la, og, s0f, s0b, g_head)


def _fnet_kernel(u_ref, ccs_ref, cs_ref, o_ref, ucs_ref, eo_ref, *, seq, scale):
    nseq = u_ref.shape[0]
    half = seq // 2
    blk = min(FN_FOLD, half)
    rows = min(FN_ROWS, seq)
    rr = lax.broadcasted_iota(jnp.int32, (blk, blk + FN_PAD), 0)
    cc = lax.broadcasted_iota(jnp.int32, (blk, blk + FN_PAD), 1)
    mirror = jnp.where(cc == blk - rr, 1.0, 0.0).astype(BF16)
    sign = 1.0 - 2.0 * (lax.broadcasted_iota(jnp.int32, (rows, 1), 0) % 2).astype(F32)
    for s in range(nseq):
        for part in range(2):
            ucs_ref[s, part, seq:seq + FN_PAD, :] = jnp.zeros((FN_PAD, FN_W), BF16)
        for gi in range(FN_GROUPS):
            sl = slice(gi * FN_CH, (gi + 1) * FN_CH)
            t = _dot(u_ref[s, :, sl], ccs_ref[...])
            ucs_ref[s, 0, 0:seq, sl] = t[:, :FN_CH].astype(BF16)
            ucs_ref[s, 1, 0:seq, sl] = t[:, FN_CH:].astype(BF16)
    for s in range(nseq):
        for j in range(half // blk):
            w0 = seq - (j + 1) * blk
            for part, sgn in ((0, 1.0), (1, -1.0)):
                own = ucs_ref[s, part, j * blk:(j + 1) * blk, :].astype(F32)
                partner = _dot(mirror, ucs_ref[s, part, w0:w0 + blk + FN_PAD, :])
                eo_ref[s, part * half + j * blk:part * half + (j + 1) * blk, :] = (own + sgn * partner).astype(BF16)
    for s in range(nseq):
        mid = ucs_ref[s, 0, half:half + 1, :].astype(F32)
        for j in range(seq // rows):
            rs = slice(j * rows, (j + 1) * rows)
            o_ref[s, rs, :] = ((_dot(cs_ref[rs, :], eo_ref[s]) + sign * mid) * scale).astype(BF16)


def _dft_tables(seq):
    def cs(n, cols):
        k = np.arange(n, dtype=np.int64)
        t = np.arange(cols, dtype=np.int64)
        ang = 2.0 * np.pi * ((k[:, None] * t[None, :]) % n).astype(np.float64) / n
        return np.cos(ang), np.sin(ang)
    ct, st = cs(seq, seq // 2)
    cc, sc = cs(FN_CH, FN_CH)
    to = lambda a: jnp.asarray(a.astype(np.float32)).astype(BF16)
    return to(np.concatenate([ct, -st], axis=1)), to(np.concatenate([cc, sc], axis=1))


def _fnet(u, batch, seq):
    cs, ccs = _dft_tables(seq)
    nseq = max(1, FN_TOKENS // seq)
    blk = pl.BlockSpec((nseq, seq, FN_W), lambda b: (b, 0, 0))
    return pl.pallas_call(
        functools.partial(_fnet_kernel, seq=seq, scale=float((seq * FN_CH) ** -0.5)),
        grid=(batch // nseq,),
        in_specs=[blk, _resident(ccs.shape), _resident(cs.shape)],
        out_specs=blk,
        out_shape=jax.ShapeDtypeStruct((batch, seq, FN_W), BF16),
        scratch_shapes=[pltpu.VMEM((nseq, 2, seq + FN_PAD, FN_W), BF16),
                        pltpu.VMEM((nseq, seq, FN_W), BF16)],
        compiler_params=_params(("arbitrary",)),
        name="fnet",
    )(u, ccs, cs)


def _rope128(x, cos, sin_signed, odd):
    back = pltpu.roll(x, 16, 1)
    fwd = pltpu.roll(x, 112, 1)
    return x * cos + jnp.where(odd, back, fwd) * sin_signed


def _l1_in_kernel(*refs, rope):
    if rope:
        x_ref, m_ref, g_ref, w_ref, cos_ref, sin_ref, q_ref, k_ref, v_ref = refs
        odd = (lax.broadcasted_iota(jnp.int32, (SUB_PROJ, LANES), 1) // 16) % 2 == 1
    else:
        x_ref, m_ref, g_ref, w_ref, q_ref, k_ref, v_ref = refs

    def project(h, rows, dst, col0, width, scale):
        for j in range(width // 256):
            y = _dot(h, w_ref[:, col0 + 256 * j:col0 + 256 * (j + 1)])
            if scale != 1.0:
                y = y * scale
            for half in range(2):
                yh = y[:, LANES * half:LANES * (half + 1)]
                if rope:
                    yh = _rope128(yh, cos_ref[rows, :], sin_ref[rows, :], odd)
                lo = 256 * j + LANES * half
                dst[rows, lo:lo + LANES] = yh.astype(dst.dtype)

    for r0 in range(0, x_ref.shape[0], SUB_PROJ):
        rows = slice(r0, r0 + SUB_PROJ)
        h = _ada_norm(x_ref[rows, :], g_ref[...], m_ref[0, 3:4, :], m_ref[0, 4:5, :]).astype(BF16)
        project(h, rows, q_ref, 0, SWA_Q, SWA_HD ** -0.5 * LOG2E)
        project(h, rows, k_ref, SWA_Q, SWA_KV, 1.0)
        v_ref[rows, :] = _dot(h, w_ref[:, SWA_Q + SWA_KV:SWA_Q + 2 * SWA_KV])


def _l1_in_proj(x, mods, bfn, g, w, rope_tabs=None, tiles_per_seq=None):
    m, d = x.shape
    row = lambda i: (i, 0)
    ins = [x, mods, g, w]
    specs = [pl.BlockSpec((TM, d), row),
             pl.BlockSpec((1, MOD_ROWS, d), lambda i: (bfn(i), 0, 0)),
             _resident(g.shape), _resident(w.shape)]
    if rope_tabs is not None:
        ins += list(rope_tabs)
        specs += [pl.BlockSpec((TM, LANES), lambda i: (i % tiles_per_seq, 0))] * 2
    return pl.pallas_call(
        functools.partial(_l1_in_kernel, rope=rope_tabs is not None),
        grid=(m // TM,),
        in_specs=specs,
        out_specs=[pl.BlockSpec((TM, SWA_Q), row), pl.BlockSpec((TM, SWA_KV), row), pl.BlockSpec((TM, SWA_KV), row)],
        out_shape=[jax.ShapeDtypeStruct((m, SWA_Q), BF16),
                   jax.ShapeDtypeStruct((m, SWA_KV), F32),
                   jax.ShapeDtypeStruct((m, SWA_KV), F32)],
        compiler_params=_params(("arbitrary",)),
        name="l1_in_proj",
    )(*ins)


def _rope_tables(seq):
    rows = seq // GRID_W
    row = np.repeat(np.arange(rows), GRID_W).astype(np.float64)
    col = (np.arange(rows * GRID_W) % GRID_W).astype(np.float64)
    n_freq = SWA_HD // 4
    inv = ROPE_BASE ** (-np.arange(n_freq, dtype=np.float64) / n_freq)
    ar = row[:, None] * inv
    ac = col[:, None] * inv
    ang = np.concatenate([ar, ar, ac, ac], axis=-1)
    sign = np.where((np.arange(SWA_HD) // n_freq) % 2 == 1, 1.0, -1.0)
    two = lambda t: jnp.asarray(np.concatenate([t, t], axis=-1).astype(np.float32))
    return two(np.cos(ang)), two(np.sin(ang) * sign)


def _attend(q_ref, ks, vs, biases, sink_ref, o_ref):
    kcat = ks[0] if len(ks) == 1 else jnp.concatenate(ks, axis=0)
    vcat = vs[0] if len(vs) == 1 else jnp.concatenate(vs, axis=0)
    n = kcat.shape[0]
    r = q_ref.shape[0]
    nblk = n // LANES
    low = lax.broadcasted_iota(jnp.int32, (n, LANES), 1) < SWA_HD
    ones_lo = jnp.where(low, 1.0, 0.0)
    ones_hi = jnp.where(low, 0.0, 1.0)
    low_out = lax.broadcasted_iota(jnp.int32, (2 * r, LANES), 1) < SWA_HD
    top = lax.broadcasted_iota(jnp.int32, (2 * r, 1), 0) < r
    biases = {t: jnp.concatenate([b, b], axis=0) for t, b in biases.items()}
    nt_dims = (((1,), (1,)), ((), ()))

    def scores(g):
        sl = slice(LANES * (g // 2), LANES * (g // 2 + 1))
        kcol, vcol = kcat[:, sl], vcat[:, sl]
        ksw, vsw = pltpu.roll(kcol, SWA_HD, 1), pltpu.roll(vcol, SWA_HD, 1)
        if g % 2 == 0:
            k_lo, k_hi = jnp.where(low, kcol, 0.0), jnp.where(low, 0.0, ksw)
            v_lo, v_hi = jnp.where(low, vcol, 0.0), jnp.where(low, 0.0, vsw)
        else:
            k_lo, k_hi = jnp.where(low, ksw, 0.0), jnp.where(low, 0.0, kcol)
            v_lo, v_hi = jnp.where(low, vsw, 0.0), jnp.where(low, 0.0, vcol)
        k_bd = jnp.concatenate([k_lo, k_hi], axis=0).astype(BF16)
        v_bd = jnp.concatenate([jnp.concatenate([v_lo, ones_lo], axis=1),
                                jnp.concatenate([v_hi, ones_hi], axis=1)], axis=0).astype(BF16)
        qs = jnp.concatenate([q_ref[:, LANES * (2 * g):LANES * (2 * g + 1)],
                              q_ref[:, LANES * (2 * g + 1):LANES * (2 * g + 2)]], axis=0)
        return lax.dot_general(qs, k_bd, nt_dims, preferred_element_type=F32), v_bd

    nxt = scores(0)
    for g in range(SWA_KV_HEADS):
        s, v_bd = nxt
        if g + 1 < SWA_KV_HEADS:
            nxt = scores(g + 1)
        probs, sink_terms = [], []
        for e in range(2):
            blocks = [s[:, e * n + LANES * t:e * n + LANES * (t + 1)] for t in range(nblk)]
            for t, b in biases.items():
                blocks[t] = blocks[t] + b
            mx = blocks[0]
            for b in blocks[1:]:
                mx = jnp.maximum(mx, b)
            sk = jnp.where(top, sink_ref[4 * g + e], sink_ref[4 * g + 2 + e]) * LOG2E
            m = jnp.maximum(jnp.max(mx, axis=-1, keepdims=True), sk)
            probs += [jnp.exp2(b - m).astype(BF16) for b in blocks]
            sink_terms.append(jnp.exp2(sk - m))
        res = _dot(jnp.concatenate(probs, axis=1), v_bd)
        denom = res[:, LANES:2 * LANES] + jnp.where(low_out, sink_terms[0], sink_terms[1])
        out = (res[:, 0:LANES] / denom).astype(o_ref.dtype)
        o_ref[:, LANES * (2 * g):LANES * (2 * g + 1)] = out[0:r]
        o_ref[:, LANES * (2 * g + 1):LANES * (2 * g + 2)] = out[r:2 * r]


def _ctx_attn_kernel(sink_ref, q_ref, k_ref, v_ref, o_ref, *, seq):
    for si in range(CTX_SEQS):
        rows = slice(si * seq, (si + 1) * seq)
        _attend(q_ref.at[rows, :], [k_ref[rows, :]], [v_ref[rows, :]], {}, sink_ref, o_ref.at[rows, :])


def _ctx_attention(q, k, v, sink, batch, seq):
    blk = lambda w: pl.BlockSpec((CTX_SEQS * seq, w), lambda b: (b, 0))
    return pl.pallas_call(
        functools.partial(_ctx_attn_kernel, seq=seq),
        grid=(batch // CTX_SEQS,),
        in_specs=[pl.BlockSpec(memory_space=pltpu.SMEM), blk(SWA_Q), blk(SWA_KV), blk(SWA_KV)],
        out_specs=blk(SWA_Q),
        out_shape=jax.ShapeDtypeStruct((batch * seq, SWA_Q), BF16),
        compiler_params=_params(("arbitrary",)),
        name="ctx_attention",
    )(sink, q, k, v)


def _lat_attn_kernel(sink_ref, q_ref, ck_ref, cv_ref, *refs, nsteps, n_ctx):
    nq = ATTN_QBLOCKS
    k_refs, v_refs, o_ref = refs[:nq + 2], refs[nq + 2:2 * nq + 4], refs[2 * nq + 4]
    p = pl.program_id(1)
    blk = ATTN_BLOCK
    row = lax.broadcasted_iota(jnp.int32, (blk, blk), 0)
    col = lax.broadcasted_iota(jnp.int32, (blk, blk), 1)
    off_first = jnp.where(p > 0, 0, blk)
    off_last = jnp.where(p < nsteps - 1, 0, blk)
    first = n_ctx // LANES
    ck, cv = ck_ref[...], cv_ref[...]
    kb = [r[...] for r in k_refs]
    vb = [r[...] for r in v_refs]
    for qi in range(nq):
        rows = slice(qi * blk, (qi + 1) * blk)
        off_prev = off_first if qi == 0 else 0
        off_next = off_last if qi == nq - 1 else 0
        bias_prev = jnp.where(col - row >= off_prev, 0.0, -jnp.inf)
        bias_next = jnp.where(row - col >= off_next, 0.0, -jnp.inf)
        _attend(q_ref.at[rows, :], [ck] + kb[qi:qi + 3], [cv] + vb[qi:qi + 3],
                {first: bias_prev, first + 2: bias_next}, sink_ref, o_ref.at[rows, :])


def _lat_attention(q, k, v, ctx_k, ctx_v, sink, batch, seq, n_ctx):
    nb = seq // ATTN_BLOCK
    nq = ATTN_QBLOCKS
    nsteps = nb // nq
    kv = lambda j: pl.BlockSpec((ATTN_BLOCK, SWA_KV),
                                lambda b, p: (b * nb + jnp.clip(nq * p - 1 + j, 0, nb - 1), 0))
    blocks = [kv(j) for j in range(nq + 2)]
    ctx = pl.BlockSpec((n_ctx, SWA_KV), lambda b, p: (b, 0))
    qo = pl.BlockSpec((nq * ATTN_BLOCK, SWA_Q), lambda b, p: (b * nsteps + p, 0))
    return pl.pallas_call(
        functools.partial(_lat_attn_kernel, nsteps=nsteps, n_ctx=n_ctx),
        grid=(batch, nsteps),
        in_specs=[pl.BlockSpec(memory_space=pltpu.SMEM), qo, ctx, ctx] + blocks + blocks,
        out_specs=qo,
        out_shape=jax.ShapeDtypeStruct((batch * seq, SWA_Q), BF16),
        compiler_params=_params(("arbitrary", "arbitrary")),
        name="lat_attention",
    )(sink, q, ctx_k, ctx_v, *([k] * (nq + 2)), *([v] * (nq + 2)))


def kernel(x_prompt, x_sample, state_l0_gla_fwd, state_l0_gla_bwd, cache_l1_k, cache_l1_v, c, c_ctx, mod_w, mod_b, norm_g, ffn_w1, ffn_w3, ffn_w2, l0_w_in, l0_w_gf, l0_b_gf, l0_w_gb, l0_b_gb, l0_g_head, l0_w_out, l1_w_in, l1_sink, l1_w_out, final_g):
    bp, tp, d = x_prompt.shape
    bs, ts, _ = x_sample.shape
    n_ctx = cache_l1_k.shape[1]
    depth = mod_w.shape[0]
    ctx_row = bs

    ffn_stacks = (ffn_w1, ffn_w3, ffn_w2)
    ffn_w = tuple(s[0, 0].astype(BF16) for s in ffn_stacks)
    gate_lo = 2 * GLA_QK + 2 * GLA_V
    gate_hi = gate_lo + 2 * GLA_GATE_RANK
    l0_wu = l0_w_in[:, gate_hi:].astype(BF16)
    l0_wg =jnp.pad(l0_w_in[:, gate_lo:gate_hi], ((0, 0), (0, LANES - 2 * GLA_GATE_RANK))).astype(BF16)
    wg2 = jnp.zeros((LANES, 2 * GLA_QK), F32)
    wg2 = wg2.at[:GLA_GATE_RANK, :GLA_QK].set(l0_w_gf)
    wg2 = wg2.at[GLA_GATE_RANK:2 * GLA_GATE_RANK, GLA_QK:].set(l0_w_gb).astype(BF16)
    bg = jnp.concatenate([l0_b_gf, l0_b_gb])[None, :]
    g_head = l0_g_head[None, :]
    sink = l1_sink.reshape(-1)
    fin_g = final_g[None, :]

    cond = jnp.zeros((MOD_ROWS, d), F32).at[:bs].set(c).at[ctx_row].set(c_ctx)
    mods = _modulation(cond, mod_w, mod_b)
    mods = mods.reshape(depth, MOD_ROWS, N_MOD, d)
    mods = jnp.pad(mods, ((0, 0), (0, 0), (0, MOD_ROWS - N_MOD), (0, 0)))

    xp = x_prompt.reshape(bp * tp, d)
    xs = x_sample.reshape(bs * ts, d)
    tiles_s = ts // TM
    bfn_p = lambda i: ctx_row
    bfn_s = lambda i: i // tiles_s
    zero_state = jnp.zeros((bp, GLA_QK, GLA_DV), F32)
    s0f = state_l0_gla_fwd.reshape(bs, GLA_QK, GLA_DV)
    s0b = state_l0_gla_bwd.reshape(bs, GLA_QK, GLA_DV)
    rope_tabs = _rope_tables(ts)

    new_state = []
    for layer in range(depth):
        ml = mods[layer]
        ga, gm, gb = (norm_g[layer, r][None, :] for r in range(3))
        last = layer == depth - 1
        xp, proj_w = _half_ffn(xp, ml, bfn_p, ga, *ffn_w, mrow=0,
                               cast=[(l0_w_in, (), gate_lo)] + [(w, ()) for w in (l0_w_out, l1_w_in, l1_w_out)]
                               if layer == 0 else None)
        if layer == 0:
            l0_wm, l0_wo, l1_wi, l1_wo = proj_w
        xs, ffn_w_b = _half_ffn(xs, ml, bfn_s, ga, *ffn_w, mrow=0, cast=[(s, (layer, 1)) for s in ffn_stacks])
        if layer % 2 == 0:
            outs = []
            for x, bfn, nb, t, sf0, sb0 in ((xp, bfn_p, bp, tp, zero_state, zero_state), (xs, bfn_s, bs, ts, s0f, s0b)):
                q, k, v, og, la, u = _l0_in_proj(x, ml, bfn, gm, l0_wm, l0_wu, l0_wg, wg2, bg)
                seq3 = lambda a: a.reshape(nb, t, a.shape[-1])
                gla_out, s_f, s_b = _gla(seq3(q), seq3(k), seq3(v), seq3(la), seq3(og), sf0, sb0, g_head, nb, t)
                gla_out = gla_out.reshape(nb * t, GLA_V)
                fn_out = _fnet(seq3(u), nb, t).reshape(nb * t, FN_W)
                outs.append(((gla_out, fn_out), s_f, s_b))
            (mix_p, s_f, s_b), (mix_s, _, _) = outs
            new_state += [s_f.reshape(bp, GLA_HEADS, GLA_DK, GLA_DV), s_b.reshape(bp, GLA_HEADS, GLA_DK, GLA_DV)]
            w_out = l0_wo
        else:
            qp, kp, vp = _l1_in_proj(xp, ml, bfn_p, gm, l1_wi)
            mix_p = (_ctx_attention(qp, kp, vp, sink, bp, tp),)
            qs, ks, vs = _l1_in_proj(xs, ml, bfn_s, gm, l1_wi, rope_tabs, tiles_s)
            mix_s = (_lat_attention(qs, ks, vs, cache_l1_k.reshape(bs * n_ctx, SWA_KV),
                                    cache_l1_v.reshape(bs * n_ctx, SWA_KV), sink, bs, ts, n_ctx),)
            new_state += [kp.reshape(bp, tp, SWA_KV_HEADS, SWA_HD), vp.reshape(bp, tp, SWA_KV_HEADS, SWA_HD)]
            w_out = l1_wo
        fin = fin_g if last else None
        xp, _ = _half_ffn(xp, ml, bfn_p, gb, *ffn_w_b, mrow=6, mixes=mix_p, w_out=w_out, final_g=fin)
        xs, ffn_w = _half_ffn(xs, ml, bfn_s, gb, *ffn_w_b, mrow=6, mixes=mix_s, w_out=w_out, final_g=fin,
                              cast=None if last else [(s, (layer + 1, 0)) for s in ffn_stacks])
    return (xp.reshape(bp, tp, d), xs.reshape(bs, ts, d), *new_state)
```

```python
import functools

import numpy as np
import jax
import jax.numpy as jnp
from jax import lax
from jax.experimental import pallas as pl
from jax.experimental.pallas import tpu as pltpu

F32 = jnp.float32
BF16 = jnp.bfloat16

D_MODEL = 1024
FFN_DIM = 2816
N_MOD = 9
EPS = 1e-6
GRID_W = 64
GLA_HEADS = 4
GLA_DK = 64
GLA_DV = 128
GLA_GATE_RANK = 16
GLA_TAU = 16.0
GLA_CHUNK = 64
GLA_QK = GLA_HEADS * GLA_DK
GLA_V = GLA_HEADS * GLA_DV
FN_GROUPS = 4
FN_CH = 128
FN_W = FN_GROUPS * FN_CH
SWA_HEADS = 16
SWA_KV_HEADS = 4
SWA_HD = 64
SWA_Q = SWA_HEADS * SWA_HD
SWA_KV = SWA_KV_HEADS * SWA_HD
WINDOW = 128
ATTN_BLOCK = 128
ATTN_QBLOCKS = 8
CTX_SEQS = 2
ROPE_BASE = 10000.0
LOG2E = float(np.log2(np.e))

LANES = 128
VMEM_LIMIT = 56 * 1024 * 1024

TM = 1024
SUB = 512
SUB_PROJ = 256
SUB_PROJ_L0 = 512
TF = 256
GLA_TILE = 256
GLA_GROUP = 4
FN_ROWS = 256
FN_TOKENS = 2048
FN_PAD = 16
FN_FOLD = 128
MOD_ROWS = 16
MOD_TN = 1024


def _params(sem):
    return pltpu.CompilerParams(dimension_semantics=sem, vmem_limit_bytes=VMEM_LIMIT)


def _resident(shape):
    nd = len(shape)
    return pl.BlockSpec(shape, lambda *_: (0,) * nd, pipeline_mode=pl.Buffered(1))


def _dot(a, b):
    return jnp.dot(a, b, preferred_element_type=F32)


def _silu(x):
    return x * jax.nn.sigmoid(x)


def _ada_norm(x, g, shift, scale):
    ms = jnp.mean(x * x, axis=-1, keepdims=True)
    return (x * lax.rsqrt(ms + EPS)) * (g * (1.0 + scale)) + shift


def _mod_kernel(c_ref, w_ref, b_ref, o_ref):
    c = c_ref[...]
    s = _silu(c).astype(BF16)
    o_ref[0] = _dot(s, w_ref[0].astype(BF16)) + b_ref[0]


def _modulation(cond, mod_w, mod_b):
    depth, d, n = mod_w.shape
    return pl.pallas_call(
        _mod_kernel,
        grid=(depth, n // MOD_TN),
        in_specs=[
            pl.BlockSpec((MOD_ROWS, d), lambda l, j: (0, 0)),
            pl.BlockSpec((1, d, MOD_TN), lambda l, j: (l, 0, j)),
            pl.BlockSpec((1, 1, MOD_TN), lambda l, j: (l, 0, j)),
        ],
        out_specs=pl.BlockSpec((1, MOD_ROWS, MOD_TN), lambda l, j: (l, 0, j)),
        out_shape=jax.ShapeDtypeStruct((depth, MOD_ROWS, n), F32),
        compiler_params=_params(("arbitrary", "arbitrary")),
        name="modulation",
    )(cond, mod_w, mod_b.reshape(depth, 1, n))


def _ffn_kernel(*refs, n_mix, mrow, final, n_cast):
    x_ref = refs[0]
    mix_refs = refs[1:1 + n_mix]
    pos = 1 + n_mix
    if n_mix:
        wo_ref = refs[pos]
        pos += 1
    m_ref, g_ref, w1_ref, w3_ref, w2_ref = refs[pos:pos + 5]
    pos += 5
    if final:
        fg_ref = refs[pos]
        pos += 1
    cast_in = refs[pos:pos + n_cast]
    pos += n_cast
    o_ref = refs[pos]
    cast_out = refs[pos + 1:pos + 1 + n_cast]
    pos += 1 + n_cast
    h_ref = refs[pos]
    for src, dst in zip(cast_in, cast_out):
        dst[...] = src[...].astype(BF16)

    def prologue(rows):
        x = x_ref[rows, :]
        if n_mix:
            off = 0
            mix = None
            for r in mix_refs:
                w = r.shape[1]
                t = _dot(r[rows, :], wo_ref[off:off + w, :])
                mix = t if mix is None else mix + t
                off += w
            x = x + m_ref[0, 5:6, :] * mix
        o_ref[rows, :] = x
        h = _ada_norm(x, g_ref[...], m_ref[0, mrow:mrow + 1, :], m_ref[0, mrow + 1:mrow + 2, :])
        h_ref[rows, :] = h.astype(BF16)

    def chunk(rows, j):
        h = h_ref[rows, :]
        a = _dot(h, w1_ref[:, j * TF:(j + 1) * TF])
        b = _dot(h, w3_ref[:, j * TF:(j + 1) * TF])
        t = _dot((_silu(a) * b).astype(BF16), w2_ref[j * TF:(j + 1) * TF, :])
        o_ref[rows, :] += (0.5 * m_ref[0, mrow + 2:mrow + 3, :]) * t

    def epilogue(rows):
        if final:
            y = o_ref[rows, :]
            ms = jnp.mean(y * y, axis=-1, keepdims=True)
            o_ref[rows, :] = y * lax.rsqrt(ms + EPS) * fg_ref[...]

    subs = [slice(r0, r0 + SUB) for r0 in range(0, x_ref.shape[0], SUB)]
    for rows in subs:
        prologue(rows)
    for j in range(FFN_DIM // TF):
        for rows in subs:
            chunk(rows, j)
    for rows in subs:
        epilogue(rows)


def _half_ffn(x, mods, bfn, g, w1, w3, w2, *, mrow, mixes=(), w_out=None, final_g=None, cast=None):
    m, d = x.shape
    steps = m // TM
    row = lambda i: (i, 0)
    ins = [x]
    specs = [pl.BlockSpec((TM, d), row)]
    for a in mixes:
        ins.append(a)
        specs.append(pl.BlockSpec((TM, a.shape[1]), row))
    if mixes:
        ins.append(w_out)
        specs.append(_resident(w_out.shape))
    ins += [mods, g, w1, w3, w2]
    specs += [pl.BlockSpec((1, MOD_ROWS, d), lambda i: (bfn(i), 0, 0)),
              _resident(g.shape), _resident(w1.shape), _resident(w3.shape), _resident(w2.shape)]
    if final_g is not None:
        ins.append(final_g)
        specs.append(_resident(final_g.shape))
    out_specs = [pl.BlockSpec((TM, d), row)]
    out_shape = [jax.ShapeDtypeStruct((m, d), F32)]
    cast = tuple(cast or ())
    for s, lead in cast:
        rows, cols = s.shape[-2] // steps, s.shape[-1]
        ins.append(s)
        specs.append(pl.BlockSpec((None,) * len(lead) + (rows, cols), lambda i, lead=tuple(lead): lead + (i, 0)))
        out_specs.append(pl.BlockSpec((rows, cols), row))
        out_shape.append(jax.ShapeDtypeStruct(s.shape[-2:], BF16))
    outs = pl.pallas_call(
        functools.partial(_ffn_kernel, n_mix=len(mixes), mrow=mrow, final=final_g is not None,
                          n_cast=len(cast)),
        grid=(steps,),
        in_specs=specs,
        out_specs=out_specs,
        out_shape=out_shape,
        scratch_shapes=[pltpu.VMEM((TM, d), BF16)],
        compiler_params=_params(("arbitrary",)),
        name="half_ffn",
    )(*ins)
    return outs[0], tuple(outs[1:])


def _l0_in_kernel(x_ref, m_ref, g_ref, wm_ref, wg_ref, wg2_ref, bg_ref,
                  q_ref, k_ref, v_ref, og_ref, la_ref, u_ref):
    for r0 in range(0, x_ref.shape[0], SUB_PROJ_L0):
        rows = slice(r0, r0 + SUB_PROJ_L0)
        h = _ada_norm(x_ref[rows, :], g_ref[...], m_ref[0, 3:4, :], m_ref[0, 4:5, :]).astype(BF16)
        q_ref[rows, :] = _dot(h, wm_ref[:, 0:256]) * (GLA_DK ** -0.5)
        lr = _dot(h, wg_ref[...]).astype(BF16)
        k_ref[rows, :] = _dot(h, wm_ref[:, 256:512])
        z = _dot(lr, wg2_ref[...]) + bg_ref[...]
        log_sig = jnp.minimum(z, 0.0) - jnp.log(1.0 + jnp.exp(-jnp.abs(z)))
        la_ref[rows, :] = log_sig * (1.0 / GLA_TAU)
        v_ref[rows, :] = _dot(h, wm_ref[:, 512:1024]).astype(BF16)
        og_ref[rows, :] = _dot(h, wm_ref[:, 1024:1536])
        u_ref[rows, :] = _dot(h, wm_ref[:, 1536:2048]).astype(BF16)


def _l0_in_proj(x, mods, bfn, g, wm, wg, wg2, bg):
    m, d = x.shape
    row = lambda i: (i, 0)
    outs = [(GLA_QK, F32), (GLA_QK, F32), (GLA_V, BF16), (GLA_V, F32), (2 * GLA_QK, F32), (FN_W, BF16)]
    return pl.pallas_call(
        _l0_in_kernel,
        grid=(m // TM,),
        in_specs=[pl.BlockSpec((TM, d), row),
                  pl.BlockSpec((1, MOD_ROWS, d), lambda i: (bfn(i), 0, 0)),
                  _resident(g.shape), _resident(wm.shape), _resident(wg.shape),
                  _resident(wg2.shape), _resident(bg.shape)],
        out_specs=[pl.BlockSpec((TM, w), row) for w, _ in outs],
        out_shape=[jax.ShapeDtypeStruct((m, w), dt) for w, dt in outs],
        compiler_params=_params(("arbitrary",)),
        name="l0_in_proj",
    )(x, mods, g, wm, wg, wg2, bg)


def _gla_tiles(qs, ks, vs, gs, ss, reverse):
    tt = GLA_TILE
    nch = tt // GLA_CHUNK
    seqs = range(len(qs))
    heads = range(GLA_HEADS)
    r = lax.broadcasted_iota(jnp.int32, (tt, tt), 0)
    c = lax.broadcasted_iota(jnp.int32, (tt, tt), 1)
    same = (r // GLA_CHUNK) == (c // GLA_CHUNK)
    m_intra = jnp.logical_and(same, (c >= r) if reverse else (c <= r))
    tri = jnp.where(m_intra, 1.0, 0.0).astype(BF16)
    cc = lax.broadcasted_iota(jnp.int32, (GLA_QK, tt), 1) // GLA_CHUNK
    lane = lax.broadcasted_iota(jnp.int32, (tt, GLA_QK), 1) // GLA_DK
    end_col = [ci * GLA_CHUNK + (0 if reverse else GLA_CHUNK - 1) for ci in range(nch)]
    zero_v = jnp.zeros((GLA_CHUNK, GLA_DV), BF16)
    zero_s = jnp.zeros((GLA_DK, GLA_DV), BF16)

    g_hi = [g.astype(BF16) for g in gs]
    g_lo = [(gs[i] - g_hi[i].astype(F32)).astype(BF16) for i in seqs]
    b = [_dot(tri, g_hi[i]) + _dot(tri, g_lo[i]) for i in seqs]
    qd = [qs[i] * jnp.exp(b[i]) for i in seqs]
    b_t = [x.T for x in b]
    k_t = [x.T for x in ks]
    ends = [[b_t[i][:, e:e + 1] for e in end_col] for i in seqs]
    ki_t, ke_t = [], []
    for i in seqs:
        bl_t = ends[i][nch - 1]
        for ci in range(nch - 2, -1, -1):
            bl_t = jnp.where(cc == ci, ends[i][ci], bl_t)
        ki_t.append((k_t[i] * jnp.exp(-b_t[i])).astype(BF16))
        ke_t.append(k_t[i] * jnp.exp(bl_t - b_t[i]))

    o_intra = [[None] * GLA_HEADS for _ in seqs]
    for h in heads:
        a = [_dot(jnp.where(lane == h, qd[i], 0.0).astype(BF16), ki_t[i]) for i in seqs]
        for i in seqs:
            am = jnp.where(m_intra, a[i], 0.0).astype(BF16)
            o_intra[i][h] = _dot(am, vs[i][:, h * GLA_DV:(h + 1) * GLA_DV])

    deltas = [[None] * GLA_HEADS for _ in seqs]
    for h in heads:
        for i in seqs:
            v_h = vs[i][:, h * GLA_DV:(h + 1) * GLA_DV]
            v_bd = jnp.concatenate(
                [jnp.concatenate([v_h[ci * GLA_CHUNK:(ci + 1) * GLA_CHUNK] if cj == ci else zero_v
                                  for cj in range(nch)], axis=1) for ci in range(nch)], axis=0)
            deltas[i][h] = _dot(ke_t[i][h * GLA_DK:(h + 1) * GLA_DK, :].astype(BF16), v_bd)

    s_h = [[ss[i][h * GLA_DK:(h + 1) * GLA_DK, :] for h in heads] for i in seqs]
    o_inter = [[None] * nch for _ in seqs]
    for ci in (range(nch - 1, -1, -1) if reverse else range(nch)):
        for i in seqs:
            s_bd = jnp.concatenate(
                [jnp.concatenate([s_h[i][h].astype(BF16) if hj == h else zero_s for hj in heads], axis=1)
                 for h in heads], axis=0)
            o_inter[i][ci] = _dot(qd[i][ci * GLA_CHUNK:(ci + 1) * GLA_CHUNK, :].astype(BF16), s_bd)
            decay = jnp.exp(ends[i][ci])
            s_h[i] = [decay[h * GLA_DK:(h + 1) * GLA_DK] * s_h[i][h]
                      + deltas[i][h][:, ci * GLA_DV:(ci + 1) * GLA_DV] for h in heads]
    outs = [jnp.concatenate(o_intra[i], axis=1) + jnp.concatenate(o_inter[i], axis=0) for i in seqs]
    return outs, [jnp.concatenate(s_h[i], axis=0) for i in seqs]


def _gla_kernel(q_ref, k_ref, v_ref, la_ref, og_ref, s0f_ref, s0b_ref, gh_ref,
                o_ref, sf_ref, sb_ref, s_ref, ob_ref, *, nt):
    p = pl.program_id(1)
    i = pl.program_id(2)
    grp = range(q_ref.shape[0])

    @pl.when(p == 0)
    def _backward():
        @pl.when(i == 0)
        def _():
            s_ref[...] = s0b_ref[...]

        start = pl.multiple_of((nt - 1 - i) * GLA_TILE, GLA_TILE)
        outs, states = _gla_tiles(*[[ref[gi] for gi in grp] for ref in (q_ref, k_ref, v_ref, la_ref, s_ref)], True)
        for gi in grp:
            s_ref[gi] = states[gi]
            ob_ref[gi, pl.ds(start, GLA_TILE), :] = outs[gi]

        @pl.when(i == nt - 1)
        def _():
            sb_ref[...] = s_ref[...]

    @pl.when(p == 1)
    def _forward():
        @pl.when(i == 0)
        def _():
            s_ref[...] = s0f_ref[...]

        start = pl.multiple_of(i * GLA_TILE, GLA_TILE)
        outs, states = _gla_tiles(*[[ref[gi] for gi in grp] for ref in (q_ref, k_ref, v_ref, la_ref, s_ref)], False)
        for gi in grp:
            s_ref[gi] = states[gi]
            o = outs[gi] + ob_ref[gi, pl.ds(start, GLA_TILE), :]
            for h in range(GLA_HEADS):
                sl = slice(h * GLA_DV, (h + 1) * GLA_DV)
                oh = o[:, sl]
                ms = jnp.mean(oh * oh, axis=-1, keepdims=True)
                y = oh * lax.rsqrt(ms + EPS) * gh_ref[...]
                o_ref[gi, :, sl] = (y * _silu(og_ref[gi, :, sl])).astype(BF16)

        @pl.when(i == nt - 1)
        def _():
            sf_ref[...] = s_ref[...]


def _gla(q, k, v, la, og, s0f, s0b, g_head, batch, seq):
    nt = seq // GLA_TILE
    gg = GLA_GROUP
    tile = lambda b, p, i: (b, p * i + (1 - p) * (nt - 1 - i), 0)
    fwd_tile = lambda b, p, i: (b, p * i, 0)
    st = pl.BlockSpec((gg, GLA_QK, GLA_DV), lambda b, p, i: (b, 0, 0))
    return pl.pallas_call(
        functools.partial(_gla_kernel, nt=nt),
        grid=(batch // gg, 2, nt),
        in_specs=[pl.BlockSpec((gg, GLA_TILE, GLA_QK), tile),
                  pl.BlockSpec((gg, GLA_TILE, GLA_QK), tile),
                  pl.BlockSpec((gg, GLA_TILE, GLA_V), tile),
                  pl.BlockSpec((gg, GLA_TILE, GLA_QK), lambda b, p, i: tile(b, p, i)[:2] + (1 - p,)),
                  pl.BlockSpec((gg, GLA_TILE, GLA_V), fwd_tile),
                  st, st, _resident(g_head.shape)],
        out_specs=[pl.BlockSpec((gg, GLA_TILE, GLA_V), fwd_tile), st, st],
        out_shape=[jax.ShapeDtypeStruct((batch, seq, GLA_V), BF16),
                   jax.ShapeDtypeStruct((batch, GLA_QK, GLA_DV), F32),
                   jax.ShapeDtypeStruct((batch, GLA_QK, GLA_DV), F32)],
        scratch_shapes=[pltpu.VMEM((gg, GLA_QK, GLA_DV), F32), pltpu.VMEM((gg, seq, GLA_V), F32)],
        compiler_params=_params(("arbitrary", "arbitrary", "arbitrary")),
        name="gla",
    )(q, k, v, la, og, s0f, s0b, g_head)


def _fnet_kernel(u_ref, ccs_ref, cs_ref, o_ref, ucs_ref, eo_ref, *, seq, scale):
    nseq = u_ref.shape[0]
    half = seq // 2
    blk = min(FN_FOLD, half)
    rows = min(FN_ROWS, seq)
    rr = lax.broadcasted_iota(jnp.int32, (blk, blk + FN_PAD), 0)
    cc = lax.broadcasted_iota(jnp.int32, (blk, blk + FN_PAD), 1)
    mirror = jnp.where(cc == blk - rr, 1.0, 0.0).astype(BF16)
    sign = 1.0 - 2.0 * (lax.broadcasted_iota(jnp.int32, (rows, 1), 0) % 2).astype(F32)
    for s in range(nseq):
        for part in range(2):
            ucs_ref[s, part, seq:seq + FN_PAD, :] = jnp.zeros((FN_PAD, FN_W), BF16)
        for gi in range(FN_GROUPS):
            sl = slice(gi * FN_CH, (gi + 1) * FN_CH)
            t = _dot(u_ref[s, :, sl], ccs_ref[...])
            ucs_ref[s, 0, 0:seq, sl] = t[:, :FN_CH].astype(BF16)
            ucs_ref[s, 1, 0:seq, sl] = t[:, FN_CH:].astype(BF16)
    for s in range(nseq):
        for j in range(half // blk):
            w0 = seq - (j + 1) * blk
            for part, sgn in ((0, 1.0), (1, -1.0)):
                own = ucs_ref[s, part, j * blk:(j + 1) * blk, :].astype(F32)
                partner = _dot(mirror, ucs_ref[s, part, w0:w0 + blk + FN_PAD, :])
                eo_ref[s, part * half + j * blk:part * half + (j + 1) * blk, :] = (own + sgn * partner).astype(BF16)
    for s in range(nseq):
        mid = ucs_ref[s, 0, half:half + 1, :].astype(F32)
        for j in range(seq // rows):
            rs = slice(j * rows, (j + 1) * rows)
            o_ref[s, rs, :] = ((_dot(cs_ref[rs, :], eo_ref[s]) + sign * mid) * scale).astype(BF16)


def _dft_tables(seq):
    def cs(n, cols):
        k = np.arange(n, dtype=np.int64)
        t = np.arange(cols, dtype=np.int64)
        ang = 2.0 * np.pi * ((k[:, None] * t[None, :]) % n).astype(np.float64) / n
        return np.cos(ang), np.sin(ang)
    ct, st = cs(seq, seq // 2)
    cc, sc = cs(FN_CH, FN_CH)
    to = lambda a: jnp.asarray(a.astype(np.float32)).astype(BF16)
    return to(np.concatenate([ct, -st], axis=1)), to(np.concatenate([cc, sc], axis=1))


def _fnet(u, batch, seq):
    cs, ccs = _dft_tables(seq)
    nseq = max(1, FN_TOKENS // seq)
    blk = pl.BlockSpec((nseq, seq, FN_W), lambda b: (b, 0, 0))
    return pl.pallas_call(
        functools.partial(_fnet_kernel, seq=seq, scale=float((seq * FN_CH) ** -0.5)),
        grid=(batch // nseq,),
        in_specs=[blk, _resident(ccs.shape), _resident(cs.shape)],
        out_specs=blk,
        out_shape=jax.ShapeDtypeStruct((batch, seq, FN_W), BF16),
        scratch_shapes=[pltpu.VMEM((nseq, 2, seq + FN_PAD, FN_W), BF16),
                        pltpu.VMEM((nseq, seq, FN_W), BF16)],
        compiler_params=_params(("arbitrary",)),
        name="fnet",
    )(u, ccs, cs)


def _rope128(x, cos, sin_signed, odd):
    back = pltpu.roll(x, 16, 1)
    fwd = pltpu.roll(x, 112, 1)
    return x * cos + jnp.where(odd, back, fwd) * sin_signed


def _l1_in_kernel(*refs, rope):
    if rope:
        x_ref, m_ref, g_ref, w_ref, cos_ref, sin_ref, q_ref, k_ref, v_ref = refs
        odd = (lax.broadcasted_iota(jnp.int32, (SUB_PROJ, LANES), 1) // 16) % 2 == 1
    else:
        x_ref, m_ref, g_ref, w_ref, q_ref, k_ref, v_ref = refs

    def project(h, rows, dst, col0, width, scale):
        for j in range(width // 256):
            y = _dot(h, w_ref[:, col0 + 256 * j:col0 + 256 * (j + 1)])
            if scale != 1.0:
                y = y * scale
            for half in range(2):
                yh = y[:, LANES * half:LANES * (half + 1)]
                if rope:
                    yh = _rope128(yh, cos_ref[rows, :], sin_ref[rows, :], odd)
                lo = 256 * j + LANES * half
                dst[rows, lo:lo + LANES] = yh.astype(dst.dtype)

    for r0 in range(0, x_ref.shape[0], SUB_PROJ):
        rows = slice(r0, r0 + SUB_PROJ)
        h = _ada_norm(x_ref[rows, :], g_ref[...], m_ref[0, 3:4, :], m_ref[0, 4:5, :]).astype(BF16)
        project(h, rows, q_ref, 0, SWA_Q, SWA_HD ** -0.5 * LOG2E)
        project(h, rows, k_ref, SWA_Q, SWA_KV, 1.0)
        v_ref[rows, :] = _dot(h, w_ref[:, SWA_Q + SWA_KV:SWA_Q + 2 * SWA_KV])


def _l1_in_proj(x, mods, bfn, g, w, rope_tabs=None, tiles_per_seq=None):
    m, d = x.shape
    row = lambda i: (i, 0)
    ins = [x, mods, g, w]
    specs = [pl.BlockSpec((TM, d), row),
             pl.BlockSpec((1, MOD_ROWS, d), lambda i: (bfn(i), 0, 0)),
             _resident(g.shape), _resident(w.shape)]
    if rope_tabs is not None:
        ins += list(rope_tabs)
        specs += [pl.BlockSpec((TM, LANES), lambda i: (i % tiles_per_seq, 0))] * 2
    return pl.pallas_call(
        functools.partial(_l1_in_kernel, rope=rope_tabs is not None),
        grid=(m // TM,),
        in_specs=specs,
        out_specs=[pl.BlockSpec((TM, SWA_Q), row), pl.BlockSpec((TM, SWA_KV), row), pl.BlockSpec((TM, SWA_KV), row)],
        out_shape=[jax.ShapeDtypeStruct((m, SWA_Q), BF16),
                   jax.ShapeDtypeStruct((m, SWA_KV), F32),
                   jax.ShapeDtypeStruct((m, SWA_KV), F32)],
        compiler_params=_params(("arbitrary",)),
        name="l1_in_proj",
    )(*ins)


def _rope_tables(seq):
    rows = seq // GRID_W
    row = np.repeat(np.arange(rows), GRID_W).astype(np.float64)
    col = (np.arange(rows * GRID_W) % GRID_W).astype(np.float64)
    n_freq = SWA_HD // 4
    inv = ROPE_BASE ** (-np.arange(n_freq, dtype=np.float64) / n_freq)
    ar = row[:, None] * inv
    ac = col[:, None] * inv
    ang = np.concatenate([ar, ar, ac, ac], axis=-1)
    sign = np.where((np.arange(SWA_HD) // n_freq) % 2 == 1, 1.0, -1.0)
    two = lambda t: jnp.asarray(np.concatenate([t, t], axis=-1).astype(np.float32))
    return two(np.cos(ang)), two(np.sin(ang) * sign)


def _attend(q_ref, ks, vs, biases, sink_ref, o_ref):
    kcat = ks[0] if len(ks) == 1 else jnp.concatenate(ks, axis=0)
    vcat = vs[0] if len(vs) == 1 else jnp.concatenate(vs, axis=0)
    n = kcat.shape[0]
    r = q_ref.shape[0]
    nblk = n // LANES
    low = lax.broadcasted_iota(jnp.int32, (n, LANES), 1) < SWA_HD
    ones_lo = jnp.where(low, 1.0, 0.0)
    ones_hi = jnp.where(low, 0.0, 1.0)
    low_out = lax.broadcasted_iota(jnp.int32, (2 * r, LANES), 1) < SWA_HD
    top = lax.broadcasted_iota(jnp.int32, (2 * r, 1), 0) < r
    biases = {t: jnp.concatenate([b, b], axis=0) for t, b in biases.items()}
    nt_dims = (((1,), (1,)), ((), ()))

    def scores(g):
        sl = slice(LANES * (g // 2), LANES * (g // 2 + 1))
        kcol, vcol = kcat[:, sl], vcat[:, sl]
        ksw, vsw = pltpu.roll(kcol, SWA_HD, 1), pltpu.roll(vcol, SWA_HD, 1)
        if g % 2 == 0:
            k_lo, k_hi = jnp.where(low, kcol, 0.0), jnp.where(low, 0.0, ksw)
            v_lo, v_hi = jnp.where(low, vcol, 0.0), jnp.where(low, 0.0, vsw)
        else:
            k_lo, k_hi = jnp.where(low, ksw, 0.0), jnp.where(low, 0.0, kcol)
            v_lo, v_hi = jnp.where(low, vsw, 0.0), jnp.where(low, 0.0, vcol)
        k_bd = jnp.concatenate([k_lo, k_hi], axis=0).astype(BF16)
        v_bd = jnp.concatenate([jnp.concatenate([v_lo, ones_lo], axis=1),
                                jnp.concatenate([v_hi, ones_hi], axis=1)], axis=0).astype(BF16)
        qs = jnp.concatenate([q_ref[:, LANES * (2 * g):LANES * (2 * g + 1)],
                              q_ref[:, LANES * (2 * g + 1):LANES * (2 * g + 2)]], axis=0)
        return lax.dot_general(qs, k_bd, nt_dims, preferred_element_type=F32), v_bd

    nxt = scores(0)
    for g in range(SWA_KV_HEADS):
        s, v_bd = nxt
        if g + 1 < SWA_KV_HEADS:
            nxt = scores(g + 1)
        probs, sink_terms = [], []
        for e in range(2):
            blocks = [s[:, e * n + LANES * t:e * n + LANES * (t + 1)] for t in range(nblk)]
            for t, b in biases.items():
                blocks[t] = blocks[t] + b
            mx = blocks[0]
            for b in blocks[1:]:
                mx = jnp.maximum(mx, b)
            sk = jnp.where(top, sink_ref[4 * g + e], sink_ref[4 * g + 2 + e]) * LOG2E
            m = jnp.maximum(jnp.max(mx, axis=-1, keepdims=True), sk)
            probs += [jnp.exp2(b - m).astype(BF16) for b in blocks]
            sink_terms.append(jnp.exp2(sk - m))
        res = _dot(jnp.concatenate(probs, axis=1), v_bd)
        denom = res[:, LANES:2 * LANES] + jnp.where(low_out, sink_terms[0], sink_terms[1])
        out = (res[:, 0:LANES] / denom).astype(o_ref.dtype)
        o_ref[:, LANES * (2 * g):LANES * (2 * g + 1)] = out[0:r]
        o_ref[:, LANES * (2 * g + 1):LANES * (2 * g + 2)] = out[r:2 * r]


def _ctx_attn_kernel(sink_ref, q_ref, k_ref, v_ref, o_ref, *, seq):
    for si in range(CTX_SEQS):
        rows = slice(si * seq, (si + 1) * seq)
        _attend(q_ref.at[rows, :], [k_ref[rows, :]], [v_ref[rows, :]], {}, sink_ref, o_ref.at[rows, :])


def _ctx_attention(q, k, v, sink, batch, seq):
    blk = lambda w: pl.BlockSpec((CTX_SEQS * seq, w), lambda b: (b, 0))
    return pl.pallas_call(
        functools.partial(_ctx_attn_kernel, seq=seq),
        grid=(batch // CTX_SEQS,),
        in_specs=[pl.BlockSpec(memory_space=pltpu.SMEM), blk(SWA_Q), blk(SWA_KV), blk(SWA_KV)],
        out_specs=blk(SWA_Q),
        out_shape=jax.ShapeDtypeStruct((batch * seq, SWA_Q), BF16),
        compiler_params=_params(("arbitrary",)),
        name="ctx_attention",
    )(sink, q, k, v)


def _lat_attn_kernel(sink_ref, q_ref, ck_ref, cv_ref, *refs, nsteps, n_ctx):
    nq = ATTN_QBLOCKS
    k_refs, v_refs, o_ref = refs[:nq + 2], refs[nq + 2:2 * nq + 4], refs[2 * nq + 4]
    p = pl.program_id(1)
    blk = ATTN_BLOCK
    row = lax.broadcasted_iota(jnp.int32, (blk, blk), 0)
    col = lax.broadcasted_iota(jnp.int32, (blk, blk), 1)
    off_first = jnp.where(p > 0, 0, blk)
    off_last = jnp.where(p < nsteps - 1, 0, blk)
    first = n_ctx // LANES
    ck, cv = ck_ref[...], cv_ref[...]
    kb = [r[...] for r in k_refs]
    vb = [r[...] for r in v_refs]
    for qi in range(nq):
        rows = slice(qi * blk, (qi + 1) * blk)
        off_prev = off_first if qi == 0 else 0
        off_next = off_last if qi == nq - 1 else 0
        bias_prev = jnp.where(col - row >= off_prev, 0.0, -jnp.inf)
        bias_next = jnp.where(row - col >= off_next, 0.0, -jnp.inf)
        _attend(q_ref.at[rows, :], [ck] + kb[qi:qi + 3], [cv] + vb[qi:qi + 3],
                {first: bias_prev, first + 2: bias_next}, sink_ref, o_ref.at[rows, :])


def _lat_attention(q, k, v, ctx_k, ctx_v, sink, batch, seq, n_ctx):
    nb = seq // ATTN_BLOCK
    nq = ATTN_QBLOCKS
    nsteps = nb // nq
    kv = lambda j: pl.BlockSpec((ATTN_BLOCK, SWA_KV),
                                lambda b, p: (b * nb + jnp.clip(nq * p - 1 + j, 0, nb - 1), 0))
    blocks = [kv(j) for j in range(nq + 2)]
    ctx = pl.BlockSpec((n_ctx, SWA_KV), lambda b, p: (b, 0))
    qo = pl.BlockSpec((nq * ATTN_BLOCK, SWA_Q), lambda b, p: (b * nsteps + p, 0))
    return pl.pallas_call(
        functools.partial(_lat_attn_kernel, nsteps=nsteps, n_ctx=n_ctx),
        grid=(batch, nsteps),
        in_specs=[pl.BlockSpec(memory_space=pltpu.SMEM), qo, ctx, ctx] + blocks + blocks,
        out_specs=qo,
        out_shape=jax.ShapeDtypeStruct((batch * seq, SWA_Q), BF16),
        compiler_params=_params(("arbitrary", "arbitrary")),
        name="lat_attention",
    )(sink, q, ctx_k, ctx_v, *([k] * (nq + 2)), *([v] * (nq + 2)))


def kernel(x_prompt, x_sample, state_l0_gla_fwd, state_l0_gla_bwd, cache_l1_k, cache_l1_v, c, c_ctx, mod_w, mod_b, norm_g, ffn_w1, ffn_w3, ffn_w2, l0_w_in, l0_w_gf, l0_b_gf, l0_w_gb, l0_b_gb, l0_g_head, l0_w_out, l1_w_in, l1_sink, l1_w_out, final_g):
    bp, tp, d = x_prompt.shape
    bs, ts, _ = x_sample.shape
    n_ctx = cache_l1_k.shape[1]
    depth = mod_w.shape[0]
    ctx_row = bs

    ffn_stacks = (ffn_w1, ffn_w3, ffn_w2)
    ffn_w = tuple(s[0, 0].astype(BF16) for s in ffn_stacks)
    gate_lo = 2 * GLA_QK + 2 * GLA_V
    gate_hi = gate_lo + 2 * GLA_GATE_RANK
    l0_wm = jnp.concatenate([l0_w_in[:, :gate_lo], l0_w_in[:, gate_hi:]], axis=1).astype(BF16)
    l0_wg = jnp.pad(l0_w_in[:, gate_lo:gate_hi], ((0, 0), (0, LANES - 2 * GLA_GATE_RANK))).astype(BF16)
    wg2 = jnp.zeros((LANES, 2 * GLA_QK), F32)
    wg2 = wg2.at[:GLA_GATE_RANK, :GLA_QK].set(l0_w_gf)
    wg2 = wg2.at[GLA_GATE_RANK:2 * GLA_GATE_RANK, GLA_QK:].set(l0_w_gb).astype(BF16)
    bg = jnp.concatenate([l0_b_gf, l0_b_gb])[None, :]
    g_head = l0_g_head[None, :]
    sink = l1_sink.reshape(-1)
    fin_g = final_g[None, :]

    cond = jnp.zeros((MOD_ROWS, d), F32).at[:bs].set(c).at[ctx_row].set(c_ctx)
    mods = _modulation(cond, mod_w, mod_b)
    mods = mods.reshape(depth, MOD_ROWS, N_MOD, d)
    mods = jnp.pad(mods, ((0, 0), (0, 0), (0, MOD_ROWS - N_MOD), (0, 0)))

    xp = x_prompt.reshape(bp * tp, d)
    xs = x_sample.reshape(bs * ts, d)
    tiles_s = ts // TM
    bfn_p = lambda i: ctx_row
    bfn_s = lambda i: i // tiles_s
    zero_state = jnp.zeros((bp, GLA_QK, GLA_DV), F32)
    s0f = state_l0_gla_fwd.reshape(bs, GLA_QK, GLA_DV)
    s0b = state_l0_gla_bwd.reshape(bs, GLA_QK, GLA_DV)
    rope_tabs = _rope_tables(ts)

    new_state = []
    for layer in range(depth):
        ml = mods[layer]
        ga, gm, gb = (norm_g[layer, r][None, :] for r in range(3))
        last = layer == depth - 1
        xp, proj_w = _half_ffn(xp, ml, bfn_p, ga, *ffn_w, mrow=0,
                               cast=[(w, ()) for w in (l0_w_out, l1_w_in, l1_w_out)] if layer == 0 else None)
        if layer == 0:
            l0_wo, l1_wi, l1_wo = proj_w
        xs, ffn_w_b = _half_ffn(xs, ml, bfn_s, ga, *ffn_w, mrow=0, cast=[(s, (layer, 1)) for s in ffn_stacks])
        if layer % 2 == 0:
            outs = []
            for x, bfn, nb, t, sf0, sb0 in ((xp, bfn_p, bp, tp, zero_state, zero_state), (xs, bfn_s, bs, ts, s0f, s0b)):
                q, k, v, og, la, u = _l0_in_proj(x, ml, bfn, gm, l0_wm, l0_wg, wg2, bg)
                seq3 = lambda a: a.reshape(nb, t, a.shape[-1])
                gla_out, s_f, s_b = _gla(seq3(q), seq3(k), seq3(v), seq3(la), seq3(og), sf0, sb0, g_head, nb, t)
                gla_out = gla_out.reshape(nb * t, GLA_V)
                fn_out = _fnet(seq3(u), nb, t).reshape(nb * t, FN_W)
                outs.append(((gla_out, fn_out), s_f, s_b))
            (mix_p, s_f, s_b), (mix_s, _, _) = outs
            new_state += [s_f.reshape(bp, GLA_HEADS, GLA_DK, GLA_DV), s_b.reshape(bp, GLA_HEADS, GLA_DK, GLA_DV)]
            w_out = l0_wo
        else:
            qp, kp, vp = _l1_in_proj(xp, ml, bfn_p, gm, l1_wi)
            mix_p = (_ctx_attention(qp, kp, vp, sink, bp, tp),)
            qs, ks, vs = _l1_in_proj(xs, ml, bfn_s, gm, l1_wi, rope_tabs, tiles_s)
            mix_s = (_lat_attention(qs, ks, vs, cache_l1_k.reshape(bs * n_ctx, SWA_KV),
                                    cache_l1_v.reshape(bs * n_ctx, SWA_KV), sink, bs, ts, n_ctx),)
            new_state += [kp.reshape(bp, tp, SWA_KV_HEADS, SWA_HD), vp.reshape(bp, tp, SWA_KV_HEADS, SWA_HD)]
            w_out = l1_wo
        fin = fin_g if last else None
        xp, _ = _half_ffn(xp, ml, bfn_p, gb, *ffn_w_b, mrow=6, mixes=mix_p, w_out=w_out, final_g=fin)
        xs, ffn_w = _half_ffn(xs, ml, bfn_s, gb, *ffn_w_b, mrow=6, mixes=mix_s, w_out=w_out, final_g=fin,
                              cast=None if last else [(s, (layer + 1, 0)) for s in ffn_stacks])
    return (xp.reshape(bp, tp, d), xs.reshape(bs, ts, d), *new_state)
```

```python
import functools

import numpy as np
import jax
import jax.numpy as jnp
from jax import lax
from jax.experimental import pallas as pl
from jax.experimental.pallas import tpu as pltpu

F32 = jnp.float32
BF16 = jnp.bfloat16

D_MODEL = 1024
FFN_DIM = 2816
N_MOD = 9
EPS = 1e-6
GRID_W = 64
GLA_HEADS = 4
GLA_DK = 64
GLA_DV = 128
GLA_GATE_RANK = 16
GLA_TAU = 16.0
GLA_CHUNK = 64
GLA_QK = GLA_HEADS * GLA_DK
GLA_V = GLA_HEADS * GLA_DV
FN_GROUPS = 4
FN_CH = 128
FN_W = FN_GROUPS * FN_CH
SWA_HEADS = 16
SWA_KV_HEADS = 4
SWA_HD = 64
SWA_Q = SWA_HEADS * SWA_HD
SWA_KV = SWA_KV_HEADS * SWA_HD
WINDOW = 128
ATTN_BLOCK = 128
ATTN_QBLOCKS = 8
CTX_SEQS = 2
ROPE_BASE = 10000.0
LOG2E = float(np.log2(np.e))

LANES = 128
VMEM_LIMIT = 56 * 1024 * 1024

TM = 1024
SUB = 512
SUB_PROJ = 256
SUB_PROJ_L0 = 512
TF = 256
GLA_TILE = 256
GLA_GROUP = 4
FN_ROWS = 256
FN_TOKENS = 2048
FN_PAD = 16
FN_FOLD = 128
MOD_ROWS = 16
MOD_TN = 1024


def _params(sem):
    return pltpu.CompilerParams(dimension_semantics=sem, vmem_limit_bytes=VMEM_LIMIT)


def _resident(shape):
    nd = len(shape)
    return pl.BlockSpec(shape, lambda *_: (0,) * nd, pipeline_mode=pl.Buffered(1))


def _dot(a, b):
    return jnp.dot(a, b, preferred_element_type=F32)


def _silu(x):
    return x * jax.nn.sigmoid(x)


def _ada_norm(x, g, shift, scale):
    ms = jnp.mean(x * x, axis=-1, keepdims=True)
    return (x * lax.rsqrt(ms + EPS)) * (g * (1.0 + scale)) + shift


def _mod_kernel(c_ref, w_ref, b_ref, o_ref):
    c = c_ref[...]
    s = _silu(c).astype(BF16)
    o_ref[0] = _dot(s, w_ref[0].astype(BF16)) + b_ref[0]


def _modulation(cond, mod_w, mod_b):
    depth, d, n = mod_w.shape
    return pl.pallas_call(
        _mod_kernel,
        grid=(depth, n // MOD_TN),
        in_specs=[
            pl.BlockSpec((MOD_ROWS, d), lambda l, j: (0, 0)),
            pl.BlockSpec((1, d, MOD_TN), lambda l, j: (l, 0, j)),
            pl.BlockSpec((1, 1, MOD_TN), lambda l, j: (l, 0, j)),
        ],
        out_specs=pl.BlockSpec((1, MOD_ROWS, MOD_TN), lambda l, j: (l, 0, j)),
        out_shape=jax.ShapeDtypeStruct((depth, MOD_ROWS, n), F32),
        compiler_params=_params(("arbitrary", "arbitrary")),
        name="modulation",
    )(cond, mod_w, mod_b.reshape(depth, 1, n))


def _ffn_kernel(*refs, n_mix, mrow, final, n_cast):
    x_ref = refs[0]
    mix_refs = refs[1:1 + n_mix]
    pos = 1 + n_mix
    if n_mix:
        wo_ref = refs[pos]
        pos += 1
    m_ref, g_ref, w1_ref, w3_ref, w2_ref = refs[pos:pos + 5]
    pos += 5
    if final:
        fg_ref = refs[pos]
        pos += 1
    cast_in = refs[pos:pos + n_cast]
    pos += n_cast
    o_ref = refs[pos]
    cast_out = refs[pos + 1:pos + 1 + n_cast]
    pos += 1 + n_cast
    acc_ref, h_ref = refs[pos], refs[pos + 1]
    for src, dst in zip(cast_in, cast_out):
        dst[...] = src[...].astype(BF16)

    def prologue(rows):
        x = x_ref[rows, :]
        if n_mix:
            off = 0
            mix = None
            for r in mix_refs:
                w = r.shape[1]
                t = _dot(r[rows, :], wo_ref[off:off + w, :])
                mix = t if mix is None else mix + t
                off += w
            x = x + m_ref[0, 5:6, :] * mix
        o_ref[rows, :] = x
        h = _ada_norm(x, g_ref[...], m_ref[0, mrow:mrow + 1, :], m_ref[0, mrow + 1:mrow + 2, :])
        h_ref[rows, :] = h.astype(BF16)

    def chunk(rows, j):
        h = h_ref[rows, :]
        a = _dot(h, w1_ref[:, j * TF:(j + 1) * TF])
        b = _dot(h, w3_ref[:, j * TF:(j + 1) * TF])
        t = _dot((_silu(a) * b).astype(BF16), w2_ref[j * TF:(j + 1) * TF, :])
        if n_mix:
            o_ref[rows, :] += (0.5 * m_ref[0, mrow + 2:mrow + 3, :]) * t
        elif j == 0:
            acc_ref[rows, :] = t
        else:
            acc_ref[rows, :] += t

    def epilogue(rows):
        if n_mix and not final:
            return
        y = o_ref[rows, :]
        if not n_mix:
            y = y + (0.5 * m_ref[0, mrow + 2:mrow + 3, :]) * acc_ref[rows, :]
        if final:
            ms = jnp.mean(y * y, axis=-1, keepdims=True)
            y = y * lax.rsqrt(ms + EPS) * fg_ref[...]
        o_ref[rows, :] = y

    subs = [slice(r0, r0 + SUB) for r0 in range(0, x_ref.shape[0], SUB)]
    for rows in subs:
        prologue(rows)
    for j in range(FFN_DIM // TF):
        for rows in subs:
            chunk(rows, j)
    for rows in subs:
        epilogue(rows)


def _half_ffn(x, mods, bfn, g, w1, w3, w2, *, mrow, mixes=(), w_out=None, final_g=None, cast=None):
    m, d = x.shape
    steps = m // TM
    row = lambda i: (i, 0)
    ins = [x]
    specs = [pl.BlockSpec((TM, d), row)]
    for a in mixes:
        ins.append(a)
        specs.append(pl.BlockSpec((TM, a.shape[1]), row))
    if mixes:
        ins.append(w_out)
        specs.append(_resident(w_out.shape))
    ins += [mods, g, w1, w3, w2]
    specs += [pl.BlockSpec((1, MOD_ROWS, d), lambda i: (bfn(i), 0, 0)),
              _resident(g.shape), _resident(w1.shape), _resident(w3.shape), _resident(w2.shape)]
    if final_g is not None:
        ins.append(final_g)
        specs.append(_resident(final_g.shape))
    out_specs = [pl.BlockSpec((TM, d), row)]
    out_shape = [jax.ShapeDtypeStruct((m, d), F32)]
    cast = tuple(cast or ())
    for s, lead in cast:
        rows, cols = s.shape[-2] // steps, s.shape[-1]
        ins.append(s)
        specs.append(pl.BlockSpec((None,) * len(lead) + (rows, cols), lambda i, lead=tuple(lead): lead + (i, 0)))
        out_specs.append(pl.BlockSpec((rows, cols), row))
        out_shape.append(jax.ShapeDtypeStruct(s.shape[-2:], BF16))
    outs = pl.pallas_call(
        functools.partial(_ffn_kernel, n_mix=len(mixes), mrow=mrow, final=final_g is not None,
                          n_cast=len(cast)),
        grid=(steps,),
        in_specs=specs,
        out_specs=out_specs,
        out_shape=out_shape,
        scratch_shapes=[pltpu.VMEM((TM, d), F32), pltpu.VMEM((TM, d), BF16)],
        compiler_params=_params(("arbitrary",)),
        name="half_ffn",
    )(*ins)
    return outs[0], tuple(outs[1:])


def _l0_in_kernel(x_ref, m_ref, g_ref, wm_ref, wg_ref, wg2_ref, bg_ref,
                  q_ref, k_ref, v_ref, og_ref, la_ref, u_ref):
    for r0 in range(0, x_ref.shape[0], SUB_PROJ_L0):
        rows = slice(r0, r0 + SUB_PROJ_L0)
        h = _ada_norm(x_ref[rows, :], g_ref[...], m_ref[0, 3:4, :], m_ref[0, 4:5, :]).astype(BF16)
        q_ref[rows, :] = _dot(h, wm_ref[:, 0:256]) * (GLA_DK ** -0.5)
        lr = _dot(h, wg_ref[...]).astype(BF16)
        k_ref[rows, :] = _dot(h, wm_ref[:, 256:512])
        z = _dot(lr, wg2_ref[...]) + bg_ref[...]
        log_sig = jnp.minimum(z, 0.0) - jnp.log(1.0 + jnp.exp(-jnp.abs(z)))
        la_ref[rows, :] = log_sig * (1.0 / GLA_TAU)
        v_ref[rows, :] = _dot(h, wm_ref[:, 512:1024]).astype(BF16)
        og_ref[rows, :] = _dot(h, wm_ref[:, 1024:1536])
        u_ref[rows, :] = _dot(h, wm_ref[:, 1536:2048]).astype(BF16)


def _l0_in_proj(x, mods, bfn, g, wm, wg, wg2, bg):
    m, d = x.shape
    row = lambda i: (i, 0)
    outs = [(GLA_QK, F32), (GLA_QK, F32), (GLA_V, BF16), (GLA_V, F32), (2 * GLA_QK, F32), (FN_W, BF16)]
    return pl.pallas_call(
        _l0_in_kernel,
        grid=(m // TM,),
        in_specs=[pl.BlockSpec((TM, d), row),
                  pl.BlockSpec((1, MOD_ROWS, d), lambda i: (bfn(i), 0, 0)),
                  _resident(g.shape), _resident(wm.shape), _resident(wg.shape),
                  _resident(wg2.shape), _resident(bg.shape)],
        out_specs=[pl.BlockSpec((TM, w), row) for w, _ in outs],
        out_shape=[jax.ShapeDtypeStruct((m, w), dt) for w, dt in outs],
        compiler_params=_params(("arbitrary",)),
        name="l0_in_proj",
    )(x, mods, g, wm, wg, wg2, bg)


def _gla_tiles(qs, ks, vs, gs, ss, reverse):
    tt = GLA_TILE
    nch = tt // GLA_CHUNK
    seqs = range(len(qs))
    heads = range(GLA_HEADS)
    r = lax.broadcasted_iota(jnp.int32, (tt, tt), 0)
    c = lax.broadcasted_iota(jnp.int32, (tt, tt), 1)
    same = (r // GLA_CHUNK) == (c // GLA_CHUNK)
    m_intra = jnp.logical_and(same, (c >= r) if reverse else (c <= r))
    tri = jnp.where(m_intra, 1.0, 0.0).astype(BF16)
    cc = lax.broadcasted_iota(jnp.int32, (GLA_QK, tt), 1) // GLA_CHUNK
    lane = lax.broadcasted_iota(jnp.int32, (tt, GLA_QK), 1) // GLA_DK
    end_col = [ci * GLA_CHUNK + (0 if reverse else GLA_CHUNK - 1) for ci in range(nch)]
    zero_v = jnp.zeros((GLA_CHUNK, GLA_DV), BF16)
    zero_s = jnp.zeros((GLA_DK, GLA_DV), BF16)

    g_hi = [g.astype(BF16) for g in gs]
    g_lo = [(gs[i] - g_hi[i].astype(F32)).astype(BF16) for i in seqs]
    b = [_dot(tri, g_hi[i]) + _dot(tri, g_lo[i]) for i in seqs]
    qd = [qs[i] * jnp.exp(b[i]) for i in seqs]
    b_t = [x.T for x in b]
    k_t = [x.T for x in ks]
    ends = [[b_t[i][:, e:e + 1] for e in end_col] for i in seqs]
    ki_t, ke_t = [], []
    for i in seqs:
        bl_t = ends[i][nch - 1]
        for ci in range(nch - 2, -1, -1):
            bl_t = jnp.where(cc == ci, ends[i][ci], bl_t)
        ki_t.append((k_t[i] * jnp.exp(-b_t[i])).astype(BF16))
        ke_t.append(k_t[i] * jnp.exp(bl_t - b_t[i]))

    o_intra = [[None] * GLA_HEADS for _ in seqs]
    for h in heads:
        a = [_dot(jnp.where(lane == h, qd[i], 0.0).astype(BF16), ki_t[i]) for i in seqs]
        for i in seqs:
            am = jnp.where(m_intra, a[i], 0.0).astype(BF16)
            o_intra[i][h] = _dot(am, vs[i][:, h * GLA_DV:(h + 1) * GLA_DV])

    deltas = [[None] * GLA_HEADS for _ in seqs]
    for h in heads:
        for i in seqs:
            v_h = vs[i][:, h * GLA_DV:(h + 1) * GLA_DV]
            v_bd = jnp.concatenate(
                [jnp.concatenate([v_h[ci * GLA_CHUNK:(ci + 1) * GLA_CHUNK] if cj == ci else zero_v
                                  for cj in range(nch)], axis=1) for ci in range(nch)], axis=0)
            deltas[i][h] = _dot(ke_t[i][h * GLA_DK:(h + 1) * GLA_DK, :].astype(BF16), v_bd)

    s_h = [[ss[i][h * GLA_DK:(h + 1) * GLA_DK, :] for h in heads] for i in seqs]
    o_inter = [[None] * nch for _ in seqs]
    for ci in (range(nch - 1, -1, -1) if reverse else range(nch)):
        for i in seqs:
            s_bd = jnp.concatenate(
                [jnp.concatenate([s_h[i][h].astype(BF16) if hj == h else zero_s for hj in heads], axis=1)
                 for h in heads], axis=0)
            o_inter[i][ci] = _dot(qd[i][ci * GLA_CHUNK:(ci + 1) * GLA_CHUNK, :].astype(BF16), s_bd)
            decay = jnp.exp(ends[i][ci])
            s_h[i] = [decay[h * GLA_DK:(h + 1) * GLA_DK] * s_h[i][h]
                      + deltas[i][h][:, ci * GLA_DV:(ci + 1) * GLA_DV] for h in heads]
    outs = [jnp.concatenate(o_intra[i], axis=1) + jnp.concatenate(o_inter[i], axis=0) for i in seqs]
    return outs, [jnp.concatenate(s_h[i], axis=0) for i in seqs]


def _gla_kernel(q_ref, k_ref, v_ref, la_ref, og_ref, s0f_ref, s0b_ref, gh_ref,
                o_ref, sf_ref, sb_ref, s_ref, ob_ref, *, nt):
    p = pl.program_id(1)
    i = pl.program_id(2)
    grp = range(q_ref.shape[0])

    @pl.when(p == 0)
    def _backward():
        @pl.when(i == 0)
        def _():
            s_ref[...] = s0b_ref[...]

        start = pl.multiple_of((nt - 1 - i) * GLA_TILE, GLA_TILE)
        outs, states = _gla_tiles(*[[ref[gi] for gi in grp] for ref in (q_ref, k_ref, v_ref, la_ref, s_ref)], True)
        for gi in grp:
            s_ref[gi] = states[gi]
            ob_ref[gi, pl.ds(start, GLA_TILE), :] = outs[gi]

        @pl.when(i == nt - 1)
        def _():
            sb_ref[...] = s_ref[...]

    @pl.when(p == 1)
    def _forward():
        @pl.when(i == 0)
        def _():
            s_ref[...] = s0f_ref[...]

        start = pl.multiple_of(i * GLA_TILE, GLA_TILE)
        outs, states = _gla_tiles(*[[ref[gi] for gi in grp] for ref in (q_ref, k_ref, v_ref, la_ref, s_ref)], False)
        for gi in grp:
            s_ref[gi] = states[gi]
            o = outs[gi] + ob_ref[gi, pl.ds(start, GLA_TILE), :]
            for h in range(GLA_HEADS):
                sl = slice(h * GLA_DV, (h + 1) * GLA_DV)
                oh = o[:, sl]
                ms = jnp.mean(oh * oh, axis=-1, keepdims=True)
                y = oh * lax.rsqrt(ms + EPS) * gh_ref[...]
                o_ref[gi, :, sl] = (y * _silu(og_ref[gi, :, sl])).astype(BF16)

        @pl.when(i == nt - 1)
        def _():
            sf_ref[...] = s_ref[...]


def _gla(q, k, v, la, og, s0f, s0b, g_head, batch, seq):
    nt = seq // GLA_TILE
    gg = GLA_GROUP
    tile = lambda b, p, i: (b, p * i + (1 - p) * (nt - 1 - i), 0)
    fwd_tile = lambda b, p, i: (b, p * i, 0)
    st = pl.BlockSpec((gg, GLA_QK, GLA_DV), lambda b, p, i: (b, 0, 0))
    return pl.pallas_call(
        functools.partial(_gla_kernel, nt=nt),
        grid=(batch // gg, 2, nt),
        in_specs=[pl.BlockSpec((gg, GLA_TILE, GLA_QK), tile),
                  pl.BlockSpec((gg, GLA_TILE, GLA_QK), tile),
                  pl.BlockSpec((gg, GLA_TILE, GLA_V), tile),
                  pl.BlockSpec((gg, GLA_TILE, GLA_QK), lambda b, p, i: tile(b, p, i)[:2] + (1 - p,)),
                  pl.BlockSpec((gg, GLA_TILE, GLA_V), fwd_tile),
                  st, st, _resident(g_head.shape)],
        out_specs=[pl.BlockSpec((gg, GLA_TILE, GLA_V), fwd_tile), st, st],
        out_shape=[jax.ShapeDtypeStruct((batch, seq, GLA_V), BF16),
                   jax.ShapeDtypeStruct((batch, GLA_QK, GLA_DV), F32),
                   jax.ShapeDtypeStruct((batch, GLA_QK, GLA_DV), F32)],
        scratch_shapes=[pltpu.VMEM((gg, GLA_QK, GLA_DV), F32), pltpu.VMEM((gg, seq, GLA_V), F32)],
        compiler_params=_params(("arbitrary", "arbitrary", "arbitrary")),
        name="gla",
    )(q, k, v, la, og, s0f, s0b, g_head)


def _fnet_kernel(u_ref, ccs_ref, cs_ref, o_ref, ucs_ref, eo_ref, *, seq, scale):
    nseq = u_ref.shape[0]
    half = seq // 2
    blk = min(FN_FOLD, half)
    rows = min(FN_ROWS, seq)
    rr = lax.broadcasted_iota(jnp.int32, (blk, blk + FN_PAD), 0)
    cc = lax.broadcasted_iota(jnp.int32, (blk, blk + FN_PAD), 1)
    mirror = jnp.where(cc == blk - rr, 1.0, 0.0).astype(BF16)
    sign = 1.0 - 2.0 * (lax.broadcasted_iota(jnp.int32, (rows, 1), 0) % 2).astype(F32)
    for s in range(nseq):
        for part in range(2):
            ucs_ref[s, part, seq:seq + FN_PAD, :] = jnp.zeros((FN_PAD, FN_W), BF16)
        for gi in range(FN_GROUPS):
            sl = slice(gi * FN_CH, (gi + 1) * FN_CH)
            t = _dot(u_ref[s, :, sl], ccs_ref[...])
            ucs_ref[s, 0, 0:seq, sl] = t[:, :FN_CH].astype(BF16)
            ucs_ref[s, 1, 0:seq, sl] = t[:, FN_CH:].astype(BF16)
    for s in range(nseq):
        for j in range(half // blk):
            w0 = seq - (j + 1) * blk
            for part, sgn in ((0, 1.0), (1, -1.0)):
                own = ucs_ref[s, part, j * blk:(j + 1) * blk, :].astype(F32)
                partner = _dot(mirror, ucs_ref[s, part, w0:w0 + blk + FN_PAD, :])
                eo_ref[s, part * half + j * blk:part * half + (j + 1) * blk, :] = (own + sgn * partner).astype(BF16)
    for s in range(nseq):
        mid = ucs_ref[s, 0, half:half + 1, :].astype(F32)
        for j in range(seq // rows):
            rs = slice(j * rows, (j + 1) * rows)
            o_ref[s, rs, :] = ((_dot(cs_ref[rs, :], eo_ref[s]) + sign * mid) * scale).astype(BF16)


def _dft_tables(seq):
    def cs(n, cols):
        k = np.arange(n, dtype=np.int64)
        t = np.arange(cols, dtype=np.int64)
        ang = 2.0 * np.pi * ((k[:, None] * t[None, :]) % n).astype(np.float64) / n
        return np.cos(ang), np.sin(ang)
    ct, st = cs(seq, seq // 2)
    cc, sc = cs(FN_CH, FN_CH)
    to = lambda a: jnp.asarray(a.astype(np.float32)).astype(BF16)
    return to(np.concatenate([ct, -st], axis=1)), to(np.concatenate([cc, sc], axis=1))


def _fnet(u, batch, seq):
    cs, ccs = _dft_tables(seq)
    nseq = max(1, FN_TOKENS // seq)
    blk = pl.BlockSpec((nseq, seq, FN_W), lambda b: (b, 0, 0))
    return pl.pallas_call(
        functools.partial(_fnet_kernel, seq=seq, scale=float((seq * FN_CH) ** -0.5)),
        grid=(batch // nseq,),
        in_specs=[blk, _resident(ccs.shape), _resident(cs.shape)],
        out_specs=blk,
        out_shape=jax.ShapeDtypeStruct((batch, seq, FN_W), BF16),
        scratch_shapes=[pltpu.VMEM((nseq, 2, seq + FN_PAD, FN_W), BF16),
                        pltpu.VMEM((nseq, seq, FN_W), BF16)],
        compiler_params=_params(("arbitrary",)),
        name="fnet",
    )(u, ccs, cs)


def _rope128(x, cos, sin_signed, odd):
    back = pltpu.roll(x, 16, 1)
    fwd = pltpu.roll(x, 112, 1)
    return x * cos + jnp.where(odd, back, fwd) * sin_signed


def _l1_in_kernel(*refs, rope):
    if rope:
        x_ref, m_ref, g_ref, w_ref, cos_ref, sin_ref, q_ref, k_ref, v_ref = refs
        odd = (lax.broadcasted_iota(jnp.int32, (SUB_PROJ, LANES), 1) // 16) % 2 == 1
    else:
        x_ref, m_ref, g_ref, w_ref, q_ref, k_ref, v_ref = refs

    def project(h, rows, dst, col0, width, scale):
        for j in range(width // 256):
            y = _dot(h, w_ref[:, col0 + 256 * j:col0 + 256 * (j + 1)])
            if scale != 1.0:
                y = y * scale
            for half in range(2):
                yh = y[:, LANES * half:LANES * (half + 1)]
                if rope:
                    yh = _rope128(yh, cos_ref[rows, :], sin_ref[rows, :], odd)
                lo = 256 * j + LANES * half
                dst[rows, lo:lo + LANES] = yh.astype(dst.dtype)

    for r0 in range(0, x_ref.shape[0], SUB_PROJ):
        rows = slice(r0, r0 + SUB_PROJ)
        h = _ada_norm(x_ref[rows, :], g_ref[...], m_ref[0, 3:4, :], m_ref[0, 4:5, :]).astype(BF16)
        project(h, rows, q_ref, 0, SWA_Q, SWA_HD ** -0.5 * LOG2E)
        project(h, rows, k_ref, SWA_Q, SWA_KV, 1.0)
        v_ref[rows, :] = _dot(h, w_ref[:, SWA_Q + SWA_KV:SWA_Q + 2 * SWA_KV])


def _l1_in_proj(x, mods, bfn, g, w, rope_tabs=None, tiles_per_seq=None):
    m, d = x.shape
    row = lambda i: (i, 0)
    ins = [x, mods, g, w]
    specs = [pl.BlockSpec((TM, d), row),
             pl.BlockSpec((1, MOD_ROWS, d), lambda i: (bfn(i), 0, 0)),
             _resident(g.shape), _resident(w.shape)]
    if rope_tabs is not None:
        ins += list(rope_tabs)
        specs += [pl.BlockSpec((TM, LANES), lambda i: (i % tiles_per_seq, 0))] * 2
    return pl.pallas_call(
        functools.partial(_l1_in_kernel, rope=rope_tabs is not None),
        grid=(m // TM,),
        in_specs=specs,
        out_specs=[pl.BlockSpec((TM, SWA_Q), row), pl.BlockSpec((TM, SWA_KV), row), pl.BlockSpec((TM, SWA_KV), row)],
        out_shape=[jax.ShapeDtypeStruct((m, SWA_Q), BF16),
                   jax.ShapeDtypeStruct((m, SWA_KV), F32),
                   jax.ShapeDtypeStruct((m, SWA_KV), F32)],
        compiler_params=_params(("arbitrary",)),
        name="l1_in_proj",
    )(*ins)


def _rope_tables(seq):
    rows = seq // GRID_W
    row = np.repeat(np.arange(rows), GRID_W).astype(np.float64)
    col = (np.arange(rows * GRID_W) % GRID_W).astype(np.float64)
    n_freq = SWA_HD // 4
    inv = ROPE_BASE ** (-np.arange(n_freq, dtype=np.float64) / n_freq)
    ar = row[:, None] * inv
    ac = col[:, None] * inv
    ang = np.concatenate([ar, ar, ac, ac], axis=-1)
    sign = np.where((np.arange(SWA_HD) // n_freq) % 2 == 1, 1.0, -1.0)
    two = lambda t: jnp.asarray(np.concatenate([t, t], axis=-1).astype(np.float32))
    return two(np.cos(ang)), two(np.sin(ang) * sign)


def _attend(q_ref, ks, vs, biases, sink_ref, o_ref):
    kcat = ks[0] if len(ks) == 1 else jnp.concatenate(ks, axis=0)
    vcat = vs[0] if len(vs) == 1 else jnp.concatenate(vs, axis=0)
    n = kcat.shape[0]
    r = q_ref.shape[0]
    nblk = n // LANES
    low = lax.broadcasted_iota(jnp.int32, (n, LANES), 1) < SWA_HD
    ones_lo = jnp.where(low, 1.0, 0.0)
    ones_hi = jnp.where(low, 0.0, 1.0)
    low_out = lax.broadcasted_iota(jnp.int32, (2 * r, LANES), 1) < SWA_HD
    top = lax.broadcasted_iota(jnp.int32, (2 * r, 1), 0) < r
    biases = {t: jnp.concatenate([b, b], axis=0) for t, b in biases.items()}
    nt_dims = (((1,), (1,)), ((), ()))

    def scores(g):
        sl = slice(LANES * (g // 2), LANES * (g // 2 + 1))
        kcol, vcol = kcat[:, sl], vcat[:, sl]
        ksw, vsw = pltpu.roll(kcol, SWA_HD, 1), pltpu.roll(vcol, SWA_HD, 1)
        if g % 2 == 0:
            k_lo, k_hi = jnp.where(low, kcol, 0.0), jnp.where(low, 0.0, ksw)
            v_lo, v_hi = jnp.where(low, vcol, 0.0), jnp.where(low, 0.0, vsw)
        else:
            k_lo, k_hi = jnp.where(low, ksw, 0.0), jnp.where(low, 0.0, kcol)
            v_lo, v_hi = jnp.where(low, vsw, 0.0), jnp.where(low, 0.0, vcol)
        k_bd = jnp.concatenate([k_lo, k_hi], axis=0).astype(BF16)
        v_bd = jnp.concatenate([jnp.concatenate([v_lo, ones_lo], axis=1),
                                jnp.concatenate([v_hi, ones_hi], axis=1)], axis=0).astype(BF16)
        qs = jnp.concatenate([q_ref[:, LANES * (2 * g):LANES * (2 * g + 1)],
                              q_ref[:, LANES * (2 * g + 1):LANES * (2 * g + 2)]], axis=0)
        return lax.dot_general(qs, k_bd, nt_dims, preferred_element_type=F32), v_bd

    nxt = scores(0)
    for g in range(SWA_KV_HEADS):
        s, v_bd = nxt
        if g + 1 < SWA_KV_HEADS:
            nxt = scores(g + 1)
        probs, sink_terms = [], []
        for e in range(2):
            blocks = [s[:, e * n + LANES * t:e * n + LANES * (t + 1)] for t in range(nblk)]
            for t, b in biases.items():
                blocks[t] = blocks[t] + b
            mx = blocks[0]
            for b in blocks[1:]:
                mx = jnp.maximum(mx, b)
            sk = jnp.where(top, sink_ref[4 * g + e], sink_ref[4 * g + 2 + e]) * LOG2E
            m = jnp.maximum(jnp.max(mx, axis=-1, keepdims=True), sk)
            probs += [jnp.exp2(b - m).astype(BF16) for b in blocks]
            sink_terms.append(jnp.exp2(sk - m))
        res = _dot(jnp.concatenate(probs, axis=1), v_bd)
        denom = res[:, LANES:2 * LANES] + jnp.where(low_out, sink_terms[0], sink_terms[1])
        out = (res[:, 0:LANES] / denom).astype(o_ref.dtype)
        o_ref[:, LANES * (2 * g):LANES * (2 * g + 1)] = out[0:r]
        o_ref[:, LANES * (2 * g + 1):LANES * (2 * g + 2)] = out[r:2 * r]


def _ctx_attn_kernel(sink_ref, q_ref, k_ref, v_ref, o_ref, *, seq):
    for si in range(CTX_SEQS):
        rows = slice(si * seq, (si + 1) * seq)
        _attend(q_ref.at[rows, :], [k_ref[rows, :]], [v_ref[rows, :]], {}, sink_ref, o_ref.at[rows, :])


def _ctx_attention(q, k, v, sink, batch, seq):
    blk = lambda w: pl.BlockSpec((CTX_SEQS * seq, w), lambda b: (b, 0))
    return pl.pallas_call(
        functools.partial(_ctx_attn_kernel, seq=seq),
        grid=(batch // CTX_SEQS,),
        in_specs=[pl.BlockSpec(memory_space=pltpu.SMEM), blk(SWA_Q), blk(SWA_KV), blk(SWA_KV)],
        out_specs=blk(SWA_Q),
        out_shape=jax.ShapeDtypeStruct((batch * seq, SWA_Q), BF16),
        compiler_params=_params(("arbitrary",)),
        name="ctx_attention",
    )(sink, q, k, v)


def _lat_attn_kernel(sink_ref, q_ref, ck_ref, cv_ref, *refs, nsteps, n_ctx):
    nq = ATTN_QBLOCKS
    k_refs, v_refs, o_ref = refs[:nq + 2], refs[nq + 2:2 * nq + 4], refs[2 * nq + 4]
    p = pl.program_id(1)
    blk = ATTN_BLOCK
    row = lax.broadcasted_iota(jnp.int32, (blk, blk), 0)
    col = lax.broadcasted_iota(jnp.int32, (blk, blk), 1)
    off_first = jnp.where(p > 0, 0, blk)
    off_last = jnp.where(p < nsteps - 1, 0, blk)
    first = n_ctx // LANES
    ck, cv = ck_ref[...], cv_ref[...]
    kb = [r[...] for r in k_refs]
    vb = [r[...] for r in v_refs]
    for qi in range(nq):
        rows = slice(qi * blk, (qi + 1) * blk)
        off_prev = off_first if qi == 0 else 0
        off_next = off_last if qi == nq - 1 else 0
        bias_prev = jnp.where(col - row >= off_prev, 0.0, -jnp.inf)
        bias_next = jnp.where(row - col >= off_next, 0.0, -jnp.inf)
        _attend(q_ref.at[rows, :], [ck] + kb[qi:qi + 3], [cv] + vb[qi:qi + 3],
                {first: bias_prev, first + 2: bias_next}, sink_ref, o_ref.at[rows, :])


def _lat_attention(q, k, v, ctx_k, ctx_v, sink, batch, seq, n_ctx):
    nb = seq // ATTN_BLOCK
    nq = ATTN_QBLOCKS
    nsteps = nb // nq
    kv = lambda j: pl.BlockSpec((ATTN_BLOCK, SWA_KV),
                                lambda b, p: (b * nb + jnp.clip(nq * p - 1 + j, 0, nb - 1), 0))
    blocks = [kv(j) for j in range(nq + 2)]
    ctx = pl.BlockSpec((n_ctx, SWA_KV), lambda b, p: (b, 0))
    qo = pl.BlockSpec((nq * ATTN_BLOCK, SWA_Q), lambda b, p: (b * nsteps + p, 0))
    return pl.pallas_call(
        functools.partial(_lat_attn_kernel, nsteps=nsteps, n_ctx=n_ctx),
        grid=(batch, nsteps),
        in_specs=[pl.BlockSpec(memory_space=pltpu.SMEM), qo, ctx, ctx] + blocks + blocks,
        out_specs=qo,
        out_shape=jax.ShapeDtypeStruct((batch * seq, SWA_Q), BF16),
        compiler_params=_params(("arbitrary", "arbitrary")),
        name="lat_attention",
    )(sink, q, ctx_k, ctx_v, *([k] * (nq + 2)), *([v] * (nq + 2)))


def kernel(x_prompt, x_sample, state_l0_gla_fwd, state_l0_gla_bwd, cache_l1_k, cache_l1_v, c, c_ctx, mod_w, mod_b, norm_g, ffn_w1, ffn_w3, ffn_w2, l0_w_in, l0_w_gf, l0_b_gf, l0_w_gb, l0_b_gb, l0_g_head, l0_w_out, l1_w_in, l1_sink, l1_w_out, final_g):
    bp, tp, d = x_prompt.shape
    bs, ts, _ = x_sample.shape
    n_ctx = cache_l1_k.shape[1]
    depth = mod_w.shape[0]
    ctx_row = bs

    ffn_stacks = (ffn_w1, ffn_w3, ffn_w2)
    ffn_w = tuple(s[0, 0].astype(BF16) for s in ffn_stacks)
    gate_lo = 2 * GLA_QK + 2 * GLA_V
    gate_hi = gate_lo + 2 * GLA_GATE_RANK
    l0_wm = jnp.concatenate([l0_w_in[:, :gate_lo], l0_w_in[:, gate_hi:]], axis=1).astype(BF16)
    l0_wg = jnp.pad(l0_w_in[:, gate_lo:gate_hi], ((0, 0), (0, LANES - 2 * GLA_GATE_RANK))).astype(BF16)
    wg2 = jnp.zeros((LANES, 2 * GLA_QK), F32)
    wg2 = wg2.at[:GLA_GATE_RANK, :GLA_QK].set(l0_w_gf)
    wg2 = wg2.at[GLA_GATE_RANK:2 * GLA_GATE_RANK, GLA_QK:].set(l0_w_gb).astype(BF16)
    bg = jnp.concatenate([l0_b_gf, l0_b_gb])[None, :]
    g_head = l0_g_head[None, :]
    sink = l1_sink.reshape(-1)
    fin_g = final_g[None, :]

    cond = jnp.zeros((MOD_ROWS, d), F32).at[:bs].set(c).at[ctx_row].set(c_ctx)
    mods = _modulation(cond, mod_w, mod_b)
    mods = mods.reshape(depth, MOD_ROWS, N_MOD, d)
    mods = jnp.pad(mods, ((0, 0), (0, 0), (0, MOD_ROWS - N_MOD), (0, 0)))

    xp = x_prompt.reshape(bp * tp, d)
    xs = x_sample.reshape(bs * ts, d)
    tiles_s = ts // TM
    bfn_p = lambda i: ctx_row
    bfn_s = lambda i: i // tiles_s
    zero_state = jnp.zeros((bp, GLA_QK, GLA_DV), F32)
    s0f = state_l0_gla_fwd.reshape(bs, GLA_QK, GLA_DV)
    s0b = state_l0_gla_bwd.reshape(bs, GLA_QK, GLA_DV)
    rope_tabs = _rope_tables(ts)

    new_state = []
    for layer in range(depth):
        ml = mods[layer]
        ga, gm, gb = (norm_g[layer, r][None, :] for r in range(3))
        last = layer == depth - 1
        xp, proj_w = _half_ffn(xp, ml, bfn_p, ga, *ffn_w, mrow=0,
                               cast=[(w, ()) for w in (l0_w_out, l1_w_in, l1_w_out)] if layer == 0 else None)
        if layer == 0:
            l0_wo, l1_wi, l1_wo = proj_w
        xs, ffn_w_b = _half_ffn(xs, ml, bfn_s, ga, *ffn_w, mrow=0, cast=[(s, (layer, 1)) for s in ffn_stacks])
        if layer % 2 == 0:
            outs = []
            for x, bfn, nb, t, sf0, sb0 in ((xp, bfn_p, bp, tp, zero_state, zero_state), (xs, bfn_s, bs, ts, s0f, s0b)):
                q, k, v, og, la, u = _l0_in_proj(x, ml, bfn, gm, l0_wm, l0_wg, wg2, bg)
                seq3 = lambda a: a.reshape(nb, t, a.shape[-1])
                gla_out, s_f, s_b = _gla(seq3(q), seq3(k), seq3(v), seq3(la), seq3(og), sf0, sb0, g_head, nb, t)
                gla_out = gla_out.reshape(nb * t, GLA_V)
                fn_out = _fnet(seq3(u), nb, t).reshape(nb * t, FN_W)
                outs.append(((gla_out, fn_out), s_f, s_b))
            (mix_p, s_f, s_b), (mix_s, _, _) = outs
            new_state += [s_f.reshape(bp, GLA_HEADS, GLA_DK, GLA_DV), s_b.reshape(bp, GLA_HEADS, GLA_DK, GLA_DV)]
            w_out = l0_wo
        else:
            qp, kp, vp = _l1_in_proj(xp, ml, bfn_p, gm, l1_wi)
            mix_p = (_ctx_attention(qp, kp, vp, sink, bp, tp),)
            qs, ks, vs = _l1_in_proj(xs, ml, bfn_s, gm, l1_wi, rope_tabs, tiles_s)
            mix_s = (_lat_attention(qs, ks, vs, cache_l1_k.reshape(bs * n_ctx, SWA_KV),
                                    cache_l1_v.reshape(bs * n_ctx, SWA_KV), sink, bs, ts, n_ctx),)
            new_state += [kp.reshape(bp, tp, SWA_KV_HEADS, SWA_HD), vp.reshape(bp, tp, SWA_KV_HEADS, SWA_HD)]
            w_out = l1_wo
        fin = fin_g if last else None
        xp, _ = _half_ffn(xp, ml, bfn_p, gb, *ffn_w_b, mrow=6, mixes=mix_p, w_out=w_out, final_g=fin)
        xs, ffn_w = _half_ffn(xs, ml, bfn_s, gb, *ffn_w_b, mrow=6, mixes=mix_s, w_out=w_out, final_g=fin,
                              cast=None if last else [(s, (layer + 1, 0)) for s in ffn_stacks])
    return (xp.reshape(bp, tp, d), xs.reshape(bs, ts, d), *new_state)
```
